```python
import math
import jax, jax.numpy as jnp
from jax import lax
import numpy as np

D_MODEL = 2048
BATCH = 2
SEQ = 4096
DEPTH = 4

GRID_W = 64
CTX_LEN = 256
N_MIXERS = 2
NORM_EPS = 1e-6
N_MOD = 6
POOL_GROUPS = 4
POOL_GROUP_DIM = D_MODEL // POOL_GROUPS
POOL_WINDOWS_1D = (2, 4, 8, 16)
POOL_WINDOWS_2D = ((1, 2), (2, 2), (2, 4), (4, 4))
SSD_EXPAND = 2
D_INNER = SSD_EXPAND * D_MODEL
SSD_HEAD_DIM = 64
SSD_HEADS = D_INNER // SSD_HEAD_DIM
SSD_GROUPS = 8
SSD_HEADS_PER_GROUP = SSD_HEADS // SSD_GROUPS
SSD_STATE = 128
SSD_CONV = 4
SSD_CHUNK = 128
SSD_GN = SSD_GROUPS * SSD_STATE
SSD_DIR_COLS = 2 * SSD_GN + SSD_HEADS
SSD_IN_COLS = 2 * D_INNER + 2 * SSD_DIR_COLS
SSD_CONV_CH = D_INNER + 2 * SSD_GN
N_EXPERTS = 16
N_EXPERT_GROUPS = 4
EXPERTS_PER_GROUP = N_EXPERTS // N_EXPERT_GROUPS
TOP_K = 2
D_EXPERT = 512

kernel_name = "hybrid_pool_ssd_moe_dit_prefix"


def _rmsnorm(x, g):
    xf = x.astype(jnp.float32)
    y = xf * lax.rsqrt(jnp.mean(xf * xf, axis=-1, keepdims=True) + NORM_EPS)
    return (y * g.astype(jnp.float32)).astype(x.dtype)


def _adaln(mod_w, mod_b, cond):
    m = jax.nn.silu(cond) @ mod_w + mod_b
    return jnp.split(m, N_MOD, axis=-1)


def _modulate(h, shift, scale):
    return h * (1.0 + scale) + shift


def _box_mean(x, axis, window):
    if window == 1:
        return x
    n = x.shape[axis]
    lo = window // 2
    hi = window - 1 - lo
    xf = x.astype(jnp.float32)
    cs = jnp.cumsum(xf, axis=axis)
    zshape = list(xf.shape)
    zshape[axis] = 1
    cs = jnp.concatenate([jnp.zeros(zshape, jnp.float32), cs], axis=axis)
    pos = jnp.arange(n)
    start = jnp.clip(pos - lo, 0, n)
    end = jnp.clip(pos + hi + 1, 0, n)
    s = jnp.take(cs, end, axis=axis) - jnp.take(cs, start, axis=axis)
    bshape = [1] * xf.ndim
    bshape[axis] = n
    cnt = (end - start).astype(jnp.float32).reshape(bshape)
    return (s / cnt).astype(x.dtype)


def _pool_mixer(h, w, b, scale, on_grid):
    bsz, L, _ = h.shape
    hg = h.reshape(bsz, L, POOL_GROUPS, POOL_GROUP_DIM)
    outs = []
    for g in range(POOL_GROUPS):
        hs = hg[:, :, g]
        if on_grid:
            rows = L // GRID_W
            t = hs.reshape(bsz, rows, GRID_W, POOL_GROUP_DIM)
            wr, wc = POOL_WINDOWS_2D[g]
            t = _box_mean(_box_mean(t, 1, wr), 2, wc).reshape(bsz, L, POOL_GROUP_DIM)
        else:
            t = _box_mean(hs, 1, POOL_WINDOWS_1D[g])
        outs.append((t - hs) @ w[g])
    return (jnp.concatenate(outs, axis=-1) + b) * scale


def _dwconv_centred(u, w, b):
    pad_lo = SSD_CONV // 2
    pad_hi = SSD_CONV - 1 - pad_lo
    out = lax.conv_general_dilated(
        u, w[:, None, :].astype(u.dtype), window_strides=(1,), padding=[(pad_lo, pad_hi)],
        dimension_numbers=("NWC", "WIO", "NWC"), feature_group_count=u.shape[-1])
    return out + b


def _ssd_scan(xdt, a, bm, cm, h0, with_output):
    bsz, L, G, E, P = xdt.shape
    N = bm.shape[-1]
    nc = L // SSD_CHUNK
    xdt = xdt.reshape(bsz, nc, SSD_CHUNK, G, E, P)
    bm = bm.reshape(bsz, nc, SSD_CHUNK, G, N)
    cm = cm.reshape(bsz, nc, SSD_CHUNK, G, N)
    a = a.reshape(bsz, nc, SSD_CHUNK, G, E)
    a_cs = jnp.cumsum(a, axis=2)
    a_tot = a_cs[:, :, -1]
    decay_to_end = jnp.exp(a_tot[:, :, None] - a_cs)
    chunk_states = jnp.einsum("bcqgn,bcqgep->bcgepn", bm, xdt * decay_to_end[..., None])

    def step(hs, inp):
        s, at = inp
        return hs * jnp.exp(at)[..., None, None] + s, hs

    h_final, h_in = lax.scan(step, h0, (jnp.swapaxes(chunk_states, 0, 1), jnp.swapaxes(a_tot, 0, 1)))
    if not with_output:
        return None, h_final
    h_in = jnp.swapaxes(h_in, 0, 1)
    seg = a_cs[:, :, :, None] - a_cs[:, :, None, :]
    qi = jnp.arange(SSD_CHUNK)
    lower = (qi[:, None] >= qi[None, :])[:, :, None, None]
    decay = jnp.exp(jnp.where(lower, seg, -jnp.inf))
    cb = jnp.einsum("bclgn,bcsgn->bclsg", cm, bm)
    y_diag = jnp.einsum("bclsge,bcsgep->bclgep", cb[..., None] * decay, xdt)
    y_off = jnp.einsum("bclgn,bcgepn->bclgep", cm, h_in) * jnp.exp(a_cs)[..., None]
    return (y_diag + y_off).reshape(bsz, L, G, E, P), h_final


def _ssd_mixer(h, in_w, conv_w, conv_b, a_log, dt_bias, d_skip, norm_g, out_w, h0, with_output):
    bsz, L, _ = h.shape
    G, E, P, N = SSD_GROUPS, SSD_HEADS_PER_GROUP, SSD_HEAD_DIM, SSD_STATE
    proj = h @ in_w
    z = proj[..., :D_INNER]
    xs = proj[..., D_INNER:2 * D_INNER]
    ys = []
    finals = []
    for d in range(2):
        base = 2 * D_INNER + d * SSD_DIR_COLS
        bc_raw = proj[..., base:base + 2 * SSD_GN]
        dt_raw = proj[..., base + 2 * SSD_GN:base + SSD_DIR_COLS]
        xbc = jax.nn.silu(_dwconv_centred(jnp.concatenate([xs, bc_raw], axis=-1), conv_w[d], conv_b[d]))
        xc = xbc[..., :D_INNER].reshape(bsz, L, G, E, P)
        bm = xbc[..., D_INNER:D_INNER + SSD_GN].reshape(bsz, L, G, N)
        cm = xbc[..., D_INNER + SSD_GN:].reshape(bsz, L, G, N)
        dt = jax.nn.softplus((dt_raw + dt_bias[d]).astype(jnp.float32)).reshape(bsz, L, G, E)
        a = dt * (-jnp.exp(a_log[d].astype(jnp.float32))).reshape(G, E)
        if d == 1:
            xc, bm, cm, dt, a = (jnp.flip(t, axis=1) for t in (xc, bm, cm, dt, a))
        y, hf = _ssd_scan(xc * dt[..., None], a, bm, cm, h0[d], with_output)
        finals.append(hf)
        if with_output:
            y = y + xc * d_skip[d].reshape(G, E)[:, :, None]
            if d == 1:
                y = jnp.flip(y, axis=1)
            ys.append(y)
    if not with_output:
        return None, finals
    y = (ys[0] + ys[1]).reshape(bsz, L, D_INNER) * jax.nn.silu(z)
    yg = y.astype(jnp.float32).reshape(bsz, L, SSD_GROUPS, D_INNER // SSD_GROUPS)
    yg = yg * lax.rsqrt(jnp.mean(yg * yg, axis=-1, keepdims=True) + NORM_EPS)
    y = (yg.reshape(bsz, L, D_INNER) * norm_g.astype(jnp.float32)).astype(h.dtype)
    return y @ out_w, finals


def _moe(h, router_w, router_bias, w_gate, w_up, w_down):
    shp = h.shape
    t = h.reshape(-1, shp[-1])
    T = t.shape[0]
    scores = jax.nn.sigmoid(t.astype(jnp.float32) @ router_w.astype(jnp.float32))
    sel = scores + router_bias.astype(jnp.float32)
    group_score = lax.top_k(sel.reshape(T, N_EXPERT_GROUPS, EXPERTS_PER_GROUP), 2)[0].sum(-1)
    best = jnp.argmax(group_score, axis=-1)
    in_group = jnp.repeat(jax.nn.one_hot(best, N_EXPERT_GROUPS, dtype=jnp.bool_), EXPERTS_PER_GROUP, axis=-1)
    _, idx = lax.top_k(jnp.where(in_group, sel, -jnp.inf), TOP_K)
    gates = jnp.take_along_axis(scores, idx, axis=-1)
    gates = gates / jnp.sum(gates, axis=-1, keepdims=True)
    combine = jnp.sum(jax.nn.one_hot(idx, N_EXPERTS, dtype=jnp.float32) * gates[..., None], axis=1)
    hg = jnp.einsum("td,edf->tef", t, w_gate)
    hu = jnp.einsum("td,edf->tef", t, w_up)
    act = jax.nn.silu(hg) * hu * combine[:, :, None].astype(t.dtype)
    return jnp.einsum("tef,efd->td", act, w_down).reshape(shp)


def setup_inputs(seed: int = 0) -> dict:
    key = jax.random.key(seed)
    ks = iter(jax.random.split(key, 40))
    f32 = jnp.float32
    n_pool = (DEPTH + 1) // 2
    n_ssd = DEPTH // 2

    def nrm(shape, scale):
        return scale * jax.random.normal(next(ks), shape, f32)

    x = nrm((BATCH, SEQ, D_MODEL), 1.0)
    c = nrm((BATCH, D_MODEL), 1.0)
    ctx = nrm((BATCH, CTX_LEN, D_MODEL), 1.0)
    c_ctx = nrm((D_MODEL,), 1.0)
    mod_w = nrm((DEPTH, D_MODEL, N_MOD * D_MODEL), 0.5 * D_MODEL ** -0.5)
    mod_b = nrm((DEPTH, N_MOD * D_MODEL), 0.02)
    norm_mix_g = 1.0 + nrm((DEPTH, D_MODEL), 0.05)
    norm_ffn_g = 1.0 + nrm((DEPTH, D_MODEL), 0.05)
    pool_w = nrm((n_pool, POOL_GROUPS, POOL_GROUP_DIM, POOL_GROUP_DIM), POOL_GROUP_DIM ** -0.5)
    pool_b = nrm((n_pool, D_MODEL), 0.02)
    pool_scale = 1.0 + nrm((n_pool, D_MODEL), 0.1)
    ssd_in_w = nrm((n_ssd, D_MODEL, SSD_IN_COLS), D_MODEL ** -0.5)
    ssd_conv_w = nrm((n_ssd, 2, SSD_CONV, SSD_CONV_CH), SSD_CONV ** -0.5)
    ssd_conv_b = nrm((n_ssd, 2, SSD_CONV_CH), 0.02)
    ssd_a_log = jnp.log(jax.random.uniform(next(ks), (n_ssd, 2, SSD_HEADS), f32, 1.0, 16.0))
    dt0 = jnp.exp(jax.random.uniform(next(ks), (n_ssd, 2, SSD_HEADS), f32, math.log(1e-3), math.log(1e-1)))
    ssd_dt_bias = dt0 + jnp.log(-jnp.expm1(-dt0))
    ssd_d = 1.0 + nrm((n_ssd, 2, SSD_HEADS), 0.1)
    ssd_norm_g = 1.0 + nrm((n_ssd, D_INNER), 0.05)
    ssd_out_w = nrm((n_ssd, D_INNER, D_MODEL), D_INNER ** -0.5)
    router_w = nrm((D_MODEL, N_EXPERTS), D_MODEL ** -0.5)
    router_bias = nrm((N_EXPERTS,), 0.01)
    moe_w_gate = nrm((DEPTH, N_EXPERTS, D_MODEL, D_EXPERT), D_MODEL ** -0.5)
    moe_w_up = nrm((DEPTH, N_EXPERTS, D_MODEL, D_EXPERT), D_MODEL ** -0.5)
    moe_w_down = nrm((DEPTH, N_EXPERTS, D_EXPERT, D_MODEL), D_EXPERT ** -0.5)
    final_norm_g = 1.0 + nrm((D_MODEL,), 0.05)
    return {"x": x, "c": c, "ctx": ctx, "c_ctx": c_ctx, "mod_w": mod_w, "mod_b": mod_b,
            "norm_mix_g": norm_mix_g, "norm_ffn_g": norm_ffn_g, "pool_w": pool_w, "pool_b": pool_b,
            "pool_scale": pool_scale, "ssd_in_w": ssd_in_w, "ssd_conv_w": ssd_conv_w, "ssd_conv_b": ssd_conv_b,
            "ssd_a_log": ssd_a_log, "ssd_dt_bias": ssd_dt_bias, "ssd_d": ssd_d, "ssd_norm_g": ssd_norm_g,
            "ssd_out_w": ssd_out_w, "router_w": router_w, "router_bias": router_bias,
            "moe_w_gate": moe_w_gate, "moe_w_up": moe_w_up, "moe_w_down": moe_w_down,
            "final_norm_g": final_norm_g}


def reference(x, c, ctx, c_ctx, mod_w, mod_b, norm_mix_g, norm_ffn_g, pool_w, pool_b, pool_scale,
              ssd_in_w, ssd_conv_w, ssd_conv_b, ssd_a_log, ssd_dt_bias, ssd_d, ssd_norm_g, ssd_out_w,
              router_w, router_bias, moe_w_gate, moe_w_up, moe_w_down, final_norm_g):
    for i in range(DEPTH):
        last = i == DEPTH - 1
        j = i // N_MIXERS
        use_pool = (i % N_MIXERS) == 0
        ctx_needed = (not last) or (not use_pool)
        sh1, sc1, g1, sh2, sc2, g2 = [m[:, None, :] for m in _adaln(mod_w[i], mod_b[i], c)]
        h = _modulate(_rmsnorm(x, norm_mix_g[i]), sh1, sc1)
        if ctx_needed:
            csh1, csc1, cg1, csh2, csc2, cg2 = _adaln(mod_w[i], mod_b[i], c_ctx)
            hc = _modulate(_rmsnorm(ctx, norm_mix_g[i]), csh1, csc1)
        if use_pool:
            x = x + g1 * _pool_mixer(h, pool_w[j], pool_b[j], pool_scale[j], True)
            if not last:
                ctx = ctx + cg1 * _pool_mixer(hc, pool_w[j], pool_b[j], pool_scale[j], False)
        else:
            p = (ssd_in_w[j], ssd_conv_w[j], ssd_conv_b[j], ssd_a_log[j], ssd_dt_bias[j], ssd_d[j],
                 ssd_norm_g[j], ssd_out_w[j])
            zero_state = jnp.zeros((ctx.shape[0], SSD_GROUPS, SSD_HEADS_PER_GROUP, SSD_HEAD_DIM, SSD_STATE),
                                   jnp.float32)
            yc, ctx_states = _ssd_mixer(hc, *p, (zero_state, zero_state), not last)
            y, _ = _ssd_mixer(h, *p, ctx_states, True)
            x = x + g1 * y
            if not last:
                ctx = ctx + cg1 * yc
        h2 = _modulate(_rmsnorm(x, norm_ffn_g[i]), sh2, sc2)
        x = x + g2 * _moe(h2, router_w, router_bias, moe_w_gate[i], moe_w_up[i], moe_w_down[i])
        if not last:
            hc2 = _modulate(_rmsnorm(ctx, norm_ffn_g[i]), csh2, csc2)
            ctx = ctx + cg2 * _moe(hc2, router_w, router_bias, moe_w_gate[i], moe_w_up[i], moe_w_down[i])
    return _rmsnorm(x, final_norm_g)
```

```python
import functools

import numpy as np
import jax
import jax.numpy as jnp
from jax import lax
from jax.experimental import pallas as pl
from jax.experimental.pallas import tpu as pltpu

F32 = jnp.float32
BF16 = jnp.bfloat16

GRID_W = 64
NORM_EPS = 1e-6
N_MOD = 6
POOL_GROUPS = 4
POOL_WINDOWS_1D = (2, 4, 8, 16)
POOL_WINDOWS_2D = ((1, 2), (2, 2), (2, 4), (4, 4))
SSD_HEAD_DIM = 64
SSD_GROUPS = 8
SSD_STATE = 128
SSD_CONV = 4
SSD_CHUNK = 128
N_EXPERTS = 16
N_EXPERT_GROUPS = 4
EXPERTS_PER_GROUP = 4
N_PAIR_CLASSES = N_EXPERT_GROUPS * 6

TM = 512
MOE_ROWS = 128
COND_ROWS = 8
VMEM_LIMIT = 56 * 1024 * 1024


def _silu(v):
    return v / (1.0 + jnp.exp(-v))


def _norm_mod(x, g, shift, scale):
    ms = jnp.mean(x * x, axis=-1, keepdims=True)
    y = x * lax.rsqrt(ms + NORM_EPS) * g
    return y * (1.0 + scale) + shift


def _cparams(sem):
    return pltpu.CompilerParams(dimension_semantics=sem, vmem_limit_bytes=VMEM_LIMIT)


def _adaln_kernel(cond_ref, w_ref, b_ref, o_ref):
    s = _silu(cond_ref[...]).astype(BF16)
    o_ref[...] = jnp.dot(s, w_ref[...].astype(BF16), preferred_element_type=F32) + b_ref[...]


def _adaln_table(cond, mod_w, mod_b):
    depth, d, nd = mod_w.shape
    tn = 1024 if nd % 1024 == 0 else nd
    out = pl.pallas_call(
        _adaln_kernel,
        grid=(depth, nd // tn),
        in_specs=[
            pl.BlockSpec((COND_ROWS, d), lambda l, j: (0, 0)),
            pl.BlockSpec((None, d, tn), lambda l, j: (l, 0, j)),
            pl.BlockSpec((None, 1, tn), lambda l, j: (l, 0, j)),
        ],
        out_specs=pl.BlockSpec((None, COND_ROWS, tn), lambda l, j: (l, 0, j)),
        out_shape=jax.ShapeDtypeStruct((depth, COND_ROWS, nd), F32),
        compiler_params=_cparams(("arbitrary", "arbitrary")),
        name="adaln",
    )(cond, mod_w, mod_b.reshape(depth, 1, nd))
    return out.reshape(depth * COND_ROWS * N_MOD, 1, d)


class _Tiles:
    def __init__(self, bsz, ctx_len, seq, d):
        assert bsz * ctx_len == TM and seq % TM == 0
        self.bsz, self.ctx_len, self.seq, self.d = bsz, ctx_len, seq, d
        self.n_ctx = bsz * ctx_len
        self.n_tok = self.n_ctx + bsz * seq
        self.n_tiles = self.n_tok // TM
        self.tiles_per_batch = seq // TM

    def seg(self, i):
        return jnp.where(i == 0, 0, 1 + (i - 1) // self.tiles_per_batch)

    def mod_spec(self, layer, k, tile_of=lambda *idx: idx[0]):
        def imap(*idx):
            return ((layer * COND_ROWS + self.seg(tile_of(*idx))) * N_MOD + k, 0, 0)
        return pl.BlockSpec((None, 1, self.d), imap)


POOL_MARGIN = 8
POOL_PREV = 2 * GRID_W
POOL_NEXT = GRID_W


def _pool_kernel(xp_ref, xc_ref, xn_ref, g_ref, sh_ref, sc_ref, gate_ref, w_ref, b_ref, ps_ref,
                 o_ref, hbuf, cpbuf, dbuf, *, ctx_len, tiles_per_batch, rows_per_batch):
    i = pl.program_id(0)
    gd = w_ref.shape[-1]
    g, sh, sc = g_ref[...], sh_ref[...], sc_ref[...]
    base = POOL_MARGIN + POOL_PREV
    zeros_m = jnp.zeros((POOL_MARGIN, hbuf.shape[1]), F32)
    hbuf[0:POOL_MARGIN, :] = zeros_m
    hbuf[base + TM + POOL_NEXT:base + TM + POOL_NEXT + POOL_MARGIN, :] = zeros_m
    hbuf[POOL_MARGIN:base, :] = _norm_mod(xp_ref[...], g, sh, sc)
    hbuf[base:base + TM, :] = _norm_mod(xc_ref[...], g, sh, sc)
    hbuf[base + TM:base + TM + POOL_NEXT, :] = _norm_mod(xn_ref[...], g, sh, sc)

    def finish(grp):
        cols = slice(grp * gd, (grp + 1) * gd)
        y = jnp.dot(dbuf[...], w_ref[grp].astype(BF16), preferred_element_type=F32)
        y = (y + b_ref[:, cols]) * ps_ref[:, cols]
        o_ref[:, cols] = xc_ref[:, cols] + gate_ref[:, cols] * y

    @pl.when(i == 0)
    def _ctx():
        ch = GRID_W
        for grp in range(POOL_GROUPS):
            cols = slice(grp * gd, (grp + 1) * gd)
            w = POOL_WINDOWS_1D[grp]
            lo, hi = w // 2, w - 1 - w // 2
            for k in range(TM // ch):
                pos = (k * ch) % ctx_len + lax.broadcasted_iota(jnp.int32, (ch, 1), 0)
                acc = jnp.zeros((ch, gd), F32)
                for dd in range(-lo, hi + 1):
                    v = hbuf[base + k * ch + dd:base + k * ch + dd + ch, cols]
                    ok = (pos + dd >= 0) & (pos + dd < ctx_len)
                    acc = acc + jnp.where(ok, v, 0.0)
                cnt = jnp.minimum(pos + hi, ctx_len - 1) - jnp.maximum(pos - lo, 0) + 1
                t = acc / cnt.astype(F32)
                hcur = hbuf[base + k * ch:base + (k + 1) * ch, cols]
                dbuf[k * ch:(k + 1) * ch, :] = (t - hcur).astype(BF16)
            finish(grp)

    @pl.when(i > 0)
    def _latent():
        tile_in_batch = (i - 1) % tiles_per_batch
        row0 = tile_in_batch * (TM // GRID_W)
        col = lax.broadcasted_iota(jnp.int32, (GRID_W, 1), 0)
        n_rows = TM // GRID_W
        for grp in range(POOL_GROUPS):
            cols = slice(grp * gd, (grp + 1) * gd)
            wr, wc = POOL_WINDOWS_2D[grp]
            lo_r, hi_r = wr // 2, wr - 1 - wr // 2
            lo_c, hi_c = wc // 2, wc - 1 - wc // 2
            cnt_c = (jnp.minimum(col + hi_c, GRID_W - 1) - jnp.maximum(col - lo_c, 0) + 1).astype(F32)
            for rr in range(2 - lo_r, 2 + n_rows + hi_r):
                start = POOL_MARGIN + rr * GRID_W
                acc = jnp.zeros((GRID_W, gd), F32)
                for dc in range(-lo_c, hi_c + 1):
                    v = hbuf[start + dc:start + dc + GRID_W, cols]
                    ok = (col + dc >= 0) & (col + dc < GRID_W)
                    acc = acc + jnp.where(ok, v, 0.0)
                cpbuf[rr * GRID_W:(rr + 1) * GRID_W, :] = acc / cnt_c
            for r in range(n_rows):
                acc = jnp.zeros((GRID_W, gd), F32)
                cnt_r = jnp.zeros((GRID_W, 1), F32)
                for dr in range(-lo_r, hi_r + 1):
                    grow = row0 + r + dr
                    ok = (grow >= 0) & (grow < rows_per_batch)
                    v = cpbuf[(2 + r + dr) * GRID_W:(3 + r + dr) * GRID_W, :]
                    acc = acc + jnp.where(ok, v, 0.0)
                    cnt_r = cnt_r + jnp.where(ok, 1.0, 0.0)
                hcur = hbuf[base + r * GRID_W:base + (r + 1) * GRID_W, cols]
                dbuf[r * GRID_W:(r + 1) * GRID_W, :] = (acc / cnt_r - hcur).astype(BF16)
            finish(grp)


def _pool_layer(tl, xall, modtab, layer, norm_g, pool_w, pool_b, pool_scale):
    d = tl.d
    gd = d // POOL_GROUPS
    n_prev_blocks = TM // POOL_PREV
    n_next_blocks = TM // POOL_NEXT
    last_next = tl.n_tok // POOL_NEXT - 1
    kern = functools.partial(_pool_kernel, ctx_len=tl.ctx_len, tiles_per_batch=tl.tiles_per_batch,
                             rows_per_batch=tl.seq // GRID_W)
    vec = pl.BlockSpec((1, d), lambda i: (0, 0))
    buf_rows = 2 * POOL_MARGIN + POOL_PREV + TM + POOL_NEXT
    return pl.pallas_call(
        kern,
        grid=(tl.n_tiles,),
        in_specs=[
            pl.BlockSpec((POOL_PREV, d), lambda i: (jnp.maximum(i * n_prev_blocks - 1, 0), 0)),
            pl.BlockSpec((TM, d), lambda i: (i, 0)),
            pl.BlockSpec((POOL_NEXT, d), lambda i: (jnp.minimum((i + 1) * n_next_blocks, last_next), 0)),
            vec,
            tl.mod_spec(layer, 0), tl.mod_spec(layer, 1), tl.mod_spec(layer, 2),
            pl.BlockSpec((POOL_GROUPS, gd, gd), lambda i: (0, 0, 0)),
            vec, vec,
        ],
        out_specs=pl.BlockSpec((TM, d), lambda i: (i, 0)),
        out_shape=jax.ShapeDtypeStruct((tl.n_tok, d), F32),
        scratch_shapes=[
            pltpu.VMEM((buf_rows, d), F32),
            pltpu.VMEM((POOL_PREV + TM + POOL_NEXT, gd), F32),
            pltpu.VMEM((TM, gd), BF16),
        ],
        compiler_params=_cparams(("arbitrary",)),
        name="pool_layer",
    )(xall, xall, xall, norm_g.reshape(1, d), modtab, modtab, modtab, pool_w,
      pool_b.reshape(1, d), pool_scale.reshape(1, d))


def _router_kernel(x_ref, g_ref, sh_ref, sc_ref, rwt_ref, rb_ref, h_ref, r_ref):
    h = _norm_mod(x_ref[...], g_ref[...], sh_ref[...], sc_ref[...])
    h_ref[...] = h
    logits = lax.dot_general(rwt_ref[...], h, (((1,), (1,)), ((), ())),
                             precision=lax.Precision.HIGHEST, preferred_element_type=F32)
    scores = 1.0 / (1.0 + jnp.exp(-logits))
    sel = scores + rb_ref[...]
    tm = sel.shape[1]
    srow = [scores[e:e + 1, :] for e in range(N_EXPERTS)]
    vrow = [sel[e:e + 1, :] for e in range(N_EXPERTS)]
    best_g = jnp.zeros((1, tm), jnp.int32)
    best_s = None
    for grp in range(N_EXPERT_GROUPS):
        v = vrow[grp * EXPERTS_PER_GROUP:(grp + 1) * EXPERTS_PER_GROUP]
        gs = None
        for a in range(EXPERTS_PER_GROUP):
            for b in range(a + 1, EXPERTS_PER_GROUP):
                p = v[a] + v[b]
                gs = p if gs is None else jnp.maximum(gs, p)
        if best_s is None:
            best_s = gs
        else:
            upd = gs > best_s
            best_s = jnp.where(upd, gs, best_s)
            best_g = jnp.where(upd, grp, best_g)
    neg = jnp.full((1, tm), -jnp.inf, F32)
    masked = [jnp.where(best_g == (e // EXPERTS_PER_GROUP), vrow[e], neg) for e in range(N_EXPERTS)]
    m1, i1, s1 = masked[0], jnp.zeros((1, tm), jnp.int32), srow[0]
    for e in range(1, N_EXPERTS):
        upd = masked[e] > m1
        m1 = jnp.where(upd, masked[e], m1)
        i1 = jnp.where(upd, e, i1)
        s1 = jnp.where(upd, srow[e], s1)
    m2, i2, s2 = neg, jnp.full((1, tm), -1, jnp.int32), jnp.zeros((1, tm), F32)
    for e in range(N_EXPERTS):
        cand = i1 != e
        upd = cand & ((masked[e] > m2) | (i2 < 0))
        m2 = jnp.where(upd, masked[e], m2)
        i2 = jnp.where(upd, e, i2)
        s2 = jnp.where(upd, srow[e], s2)
    tot = s1 + s2
    w1, w2 = s1 / tot, s2 / tot
    first_lo = i1 < i2
    e_lo = jnp.where(first_lo, i1, i2).astype(F32)
    e_hi = jnp.where(first_lo, i2, i1).astype(F32)
    w_lo = jnp.where(first_lo, w1, w2)
    w_hi = jnp.where(first_lo, w2, w1)
    zero = jnp.zeros((1, tm), F32)
    r_ref[...] = jnp.concatenate([e_lo, e_hi, w_lo, w_hi, zero, zero, zero, zero], axis=0)


def _router(tl, xall, modtab, layer, norm_g, router_w, router_bias):
    d = tl.d
    vec = pl.BlockSpec((1, d), lambda i: (0, 0))
    return pl.pallas_call(
        _router_kernel,
        grid=(tl.n_tiles,),
        in_specs=[
            pl.BlockSpec((TM, d), lambda i: (i, 0)),
            vec, tl.mod_spec(layer, 3), tl.mod_spec(layer, 4),
            pl.BlockSpec((N_EXPERTS, d), lambda i: (0, 0)),
            pl.BlockSpec((N_EXPERTS, 1), lambda i: (0, 0)),
        ],
        out_specs=[
            pl.BlockSpec((TM, d), lambda i: (i, 0)),
            pl.BlockSpec((8, TM), lambda i: (0, i)),
        ],
        out_shape=[
            jax.ShapeDtypeStruct((tl.n_tok, d), F32),
            jax.ShapeDtypeStruct((8, tl.n_tok), F32),
        ],
        compiler_params=_cparams(("arbitrary",)),
        name="moe_router",
    )(xall, norm_g.reshape(1, d), modtab, modtab, router_w.T, router_bias.reshape(N_EXPERTS, 1))


def _pair_tables():
    lo, hi = [], []
    for grp in range(N_EXPERT_GROUPS):
        for a in range(EXPERTS_PER_GROUP):
            for b in range(a + 1, EXPERTS_PER_GROUP):
                lo.append(grp * EXPERTS_PER_GROUP + a)
                hi.append(grp * EXPERTS_PER_GROUP + b)
    return np.asarray(lo, np.int32), np.asarray(hi, np.int32)


def _moe_tiles(n_tok):
    return -(-(n_tok + N_PAIR_CLASSES * (MOE_ROWS - 1)) // MOE_ROWS)


def _dispatch(route, n_tok):
    e_lo = route[0].astype(jnp.int32)
    e_hi = route[1].astype(jnp.int32)
    a = e_lo % EXPERTS_PER_GROUP
    b = e_hi % EXPERTS_PER_GROUP
    cls = (e_lo // EXPERTS_PER_GROUP) * 6 + (a * (7 - a)) // 2 + (b - a - 1)
    onehot = (cls[:, None] == jnp.arange(N_PAIR_CLASSES, dtype=jnp.int32)[None, :]).astype(jnp.int32)
    csum = jnp.cumsum(onehot, axis=0)
    rank = jnp.sum(csum * onehot, axis=1) - 1
    cnt = csum[-1]
    ntile = (cnt + MOE_ROWS - 1) // MOE_ROWS
    tile_end = jnp.cumsum(ntile)
    tile_start = tile_end - ntile
    pos = jnp.sum(onehot * tile_start[None, :], axis=1) * MOE_ROWS + rank
    n_tiles = _moe_tiles(n_tok)
    n_used = tile_end[-1]
    tiles = jnp.minimum(jnp.arange(n_tiles, dtype=jnp.int32), n_used - 1)
    tcls = jnp.sum((tiles[:, None] >= tile_end[None, :]).astype(jnp.int32), axis=1)
    lo_tab, hi_tab = _pair_tables()
    t_lo = jnp.asarray(lo_tab)[tcls]
    t_hi = jnp.asarray(hi_tab)[tcls]
    n_slots = n_tiles * MOE_ROWS
    tok = jnp.arange(n_tok, dtype=jnp.int32)
    slot_tok = jnp.zeros((n_slots,), jnp.int32).at[pos].set(tok)
    slot_w = jnp.zeros((n_slots, 2), F32).at[pos].set(jnp.stack([route[2], route[3]], axis=1))
    return pos.astype(jnp.int32), slot_tok, slot_w, t_lo, t_hi, n_used.reshape(1).astype(jnp.int32)


def _gather_rows(idx_ref, first, src_hbm, dst, sem, n_rows):
    def body(r, carry):
        row = idx_ref[first + r]
        pltpu.make_async_copy(src_hbm.at[pl.ds(row, 1), :], dst.at[pl.ds(r, 1), :], sem).start()
        return carry
    lax.fori_loop(0, n_rows, body, 0)


def _wait_rows(src_hbm, dst, sem, n_rows):
    def body(r, carry):
        pltpu.make_async_copy(src_hbm.at[pl.ds(0, 1), :], dst.at[pl.ds(r, 1), :], sem).wait()
        return carry
    lax.fori_loop(0, n_rows, body, 0)


def _ffn_kernel(tlo_ref, thi_ref, nused_ref, stok_ref, h_hbm, sw_ref,
                wg1_ref, wu1_ref, wd1_ref, wg2_ref, wu2_ref, wd2_ref, o_ref, xbuf, sem):
    t = pl.program_id(0)
    n_t = pl.num_programs(0)
    slot = t % 2

    @pl.when(t == 0)
    def _prime():
        _gather_rows(stok_ref, 0, h_hbm, xbuf.at[0], sem.at[0], MOE_ROWS)

    @pl.when(t + 1 < n_t)
    def _prefetch():
        _gather_rows(stok_ref, (t + 1) * MOE_ROWS, h_hbm, xbuf.at[1 - slot], sem.at[1 - slot], MOE_ROWS)

    _wait_rows(h_hbm, xbuf.at[slot], sem.at[slot], MOE_ROWS)

    @pl.when(t < nused_ref[0])
    def _compute():
        xb = xbuf[slot].astype(BF16)
        sw = sw_ref[...]

        def expert(wg_ref, wu_ref, wd_ref, gate):
            hg = jnp.dot(xb, wg_ref[...], preferred_element_type=F32)
            hu = jnp.dot(xb, wu_ref[...], preferred_element_type=F32)
            act = (_silu(hg) * hu * gate).astype(BF16)
            return jnp.dot(act, wd_ref[...], preferred_element_type=F32)

        o_ref[...] = (expert(wg1_ref, wu1_ref, wd1_ref, sw[:, 0:1])
                      + expert(wg2_ref, wu2_ref, wd2_ref, sw[:, 1:2]))

    @pl.when(t >= nused_ref[0])
    def _idle():
        o_ref[...] = jnp.zeros_like(o_ref)


def _moe_ffn(h2, slot_tok, slot_w, t_lo, t_hi, n_used, wg, wu, wd):
    n_tok, d = h2.shape
    f = wg.shape[-1]
    n_tiles = _moe_tiles(n_tok)
    wspec_lo = lambda shape: pl.BlockSpec((None,) + shape, lambda t, lo, hi, nu, st: (lo[t], 0, 0))
    wspec_hi = lambda shape: pl.BlockSpec((None,) + shape, lambda t, lo, hi, nu, st: (hi[t], 0, 0))
    grid_spec = pltpu.PrefetchScalarGridSpec(
        num_scalar_prefetch=4,
        grid=(n_tiles,),
        in_specs=[
            pl.BlockSpec(memory_space=pl.ANY),
            pl.BlockSpec((MOE_ROWS, 2), lambda t, lo, hi, nu, st: (t, 0)),
            wspec_lo((d, f)), wspec_lo((d, f)), wspec_lo((f, d)),
            wspec_hi((d, f)), wspec_hi((d, f)), wspec_hi((f, d)),
        ],
        out_specs=pl.BlockSpec((MOE_ROWS, d), lambda t, lo, hi, nu, st: (t, 0)),
        scratch_shapes=[
            pltpu.VMEM((2, MOE_ROWS, d), F32),
            pltpu.SemaphoreType.DMA((2,)),
        ],
    )
    return pl.pallas_call(
        _ffn_kernel,
        grid_spec=grid_spec,
        out_shape=jax.ShapeDtypeStruct((n_tiles * MOE_ROWS, d), F32),
        compiler_params=_cparams(("arbitrary",)),
        name="moe_ffn",
    )(t_lo, t_hi, n_used, slot_tok, h2, slot_w, wg, wu, wd, wg, wu, wd)


def _combine_kernel(pos_ref, x_ref, gate_ref, fg_ref, y_hbm, o_ref, ybuf, sem, *, final):
    i = pl.program_id(0)
    n_i = pl.num_programs(0)
    slot = i % 2

    @pl.when(i == 0)
    def _prime():
        _gather_rows(pos_ref, 0, y_hbm, ybuf.at[0], sem.at[0], TM)

    @pl.when(i + 1 < n_i)
    def _prefetch():
        _gather_rows(pos_ref, (i + 1) * TM, y_hbm, ybuf.at[1 - slot], sem.at[1 - slot], TM)

    _wait_rows(y_hbm, ybuf.at[slot], sem.at[slot], TM)
    x = x_ref[...] + gate_ref[...] * ybuf[slot]
    if final:
        ms = jnp.mean(x * x, axis=-1, keepdims=True)
        x = x * lax.rsqrt(ms + NORM_EPS) * fg_ref[...]
    o_ref[...] = x


def _moe_combine(tl, xall, y_sorted, pos, modtab, layer, final_g, final):
    d = tl.d
    grid_spec = pltpu.PrefetchScalarGridSpec(
        num_scalar_prefetch=1,
        grid=(tl.n_tiles,),
        in_specs=[
            pl.BlockSpec((TM, d), lambda i, p: (i, 0)),
            tl.mod_spec(layer, 5),
            pl.BlockSpec((1, d), lambda i, p: (0, 0)),
            pl.BlockSpec(memory_space=pl.ANY),
        ],
        out_specs=pl.BlockSpec((TM, d), lambda i, p: (i, 0)),
        scratch_shapes=[
            pltpu.VMEM((2, TM, d), F32),
            pltpu.SemaphoreType.DMA((2,)),
        ],
    )
    return pl.pallas_call(
        functools.partial(_combine_kernel, final=final),
        grid_spec=grid_spec,
        out_shape=jax.ShapeDtypeStruct((tl.n_tok, d), F32),
        compiler_params=_cparams(("arbitrary",)),
        name="moe_combine",
    )(pos, xall, modtab, final_g.reshape(1, d), y_sorted)


def _moe_layer(tl, xall, modtab, layer, norm_g, router_w, router_bias, wg, wu, wd, final_g, final):
    h2, route = _router(tl, xall, modtab, layer, norm_g, router_w, router_bias)
    pos, slot_tok, slot_w, t_lo, t_hi, n_used = _dispatch(route, tl.n_tok)
    y_sorted = _moe_ffn(h2, slot_tok, slot_w, t_lo, t_hi, n_used,
                        wg.astype(BF16), wu.astype(BF16), wd.astype(BF16))
    return _moe_combine(tl, xall, y_sorted, pos, modtab, layer, final_g, final)


INPROJ_TN = 512
OUTPROJ_TN = 512
CONV_HALO = 16


def _inproj_kernel(x_ref, g_ref, sh_ref, sc_ref, w_ref, wdt_ref, p_ref, dt_ref, hbuf):
    @pl.when(pl.program_id(1) == 0)
    def _prologue():
        h = _norm_mod(x_ref[...], g_ref[...], sh_ref[...], sc_ref[...]).astype(BF16)
        hbuf[...] = h
        dt_ref[...] = jnp.dot(h, wdt_ref[...], preferred_element_type=F32)

    p_ref[...] = jnp.dot(hbuf[...], w_ref[...], preferred_element_type=F32).astype(BF16)


def _inproj(tl, xall, modtab, layer, norm_g, w_main, w_dt):
    d = tl.d
    n_main = w_main.shape[1]
    n_dt = w_dt.shape[1]
    tn = INPROJ_TN
    vec = pl.BlockSpec((1, d), lambda i, j: (0, 0))
    return pl.pallas_call(
        _inproj_kernel,
        grid=(tl.n_tiles, n_main // tn),
        in_specs=[
            pl.BlockSpec((TM, d), lambda i, j: (i, 0)),
            vec, tl.mod_spec(layer, 0), tl.mod_spec(layer, 1),
            pl.BlockSpec((d, tn), lambda i, j: (0, j)),
            pl.BlockSpec((d, n_dt), lambda i, j: (0, 0)),
        ],
        out_specs=[
            pl.BlockSpec((TM, tn), lambda i, j: (i, j)),
            pl.BlockSpec((TM, n_dt), lambda i, j: (i, 0)),
        ],
        out_shape=[
            jax.ShapeDtypeStruct((tl.n_tok, n_main), BF16),
            jax.ShapeDtypeStruct((tl.n_tok, n_dt), F32),
        ],
        scratch_shapes=[pltpu.VMEM((TM, d), BF16)],
        compiler_params=_cparams(("arbitrary", "arbitrary")),
        name="ssd_inproj",
    )(xall, norm_g.reshape(1, d), modtab, modtab, w_main, w_dt)


def _split2(v):
    hi = v.astype(BF16)
    lo = (v - hi.astype(F32)).astype(BF16)
    return hi, lo


def _scan_kernel(xs_ref, xsp_ref, xsn_ref, bc_ref, bcp_ref, bcn_ref, dt_ref, cw_ref, cb_ref, hp_ref,
                 e_ref, y_ref, ubuf, xc, xdt, xdd, eacsx, rowx, bm, cm, bmt, acst, lbuf, state,
                 *, ncc, ncl):
    q = SSD_CHUNK
    di = xs_ref.shape[1]
    n_heads = dt_ref.shape[-1]
    n_grp = SSD_GROUPS
    ep = di // n_grp
    blocks_per_group = ep // 128
    d = pl.program_id(0)
    c = pl.program_id(2)
    is_ctx = c < ncc
    fwd = d == 0
    tch = jnp.where(is_ctx, jnp.where(fwd, c, ncc - 1 - c), jnp.where(fwd, c - ncc, ncl - 1 - (c - ncc)))
    first = tch == 0
    last = tch == jnp.where(is_ctx, ncc - 1, ncl - 1)

    @pl.when(c == 0)
    def _reset():
        state[...] = jnp.zeros_like(state)

    hl = CONV_HALO
    ubuf[0:hl, 0:di] = jnp.where(first, 0.0, xsp_ref[...].astype(F32))
    ubuf[hl:hl + q, 0:di] = xs_ref[...].astype(F32)
    ubuf[hl + q:hl + q + hl, 0:di] = jnp.where(last, 0.0, xsn_ref[...].astype(F32))
    ubuf[0:hl, di:] = jnp.where(first, 0.0, bcp_ref[...].astype(F32))
    ubuf[hl:hl + q, di:] = bc_ref[...].astype(F32)
    ubuf[hl + q:hl + q + hl, di:] = jnp.where(last, 0.0, bcn_ref[...].astype(F32))

    def conv_block(col):
        acc = cb_ref[:, pl.ds(col, 128)]
        for k in range(SSD_CONV):
            off = hl + k - SSD_CONV // 2
            acc = acc + cw_ref[k:k + 1, pl.ds(col, 128)] * ubuf[off:off + q, pl.ds(col, 128)]
        return _silu(acc)

    hp = hp_ref[...]
    a_coef = -jnp.exp(hp[0:1])
    raw = dt_ref[...] + hp[1:2]
    dt = jnp.maximum(raw, 0.0) + jnp.log(1.0 + jnp.exp(-jnp.abs(raw)))
    a = dt * a_coef
    li = lax.broadcasted_iota(jnp.int32, (q, q), 0)
    si = lax.broadcasted_iota(jnp.int32, (q, q), 1)
    tri = jnp.where(fwd, li - si, si - li) >= 0
    trib = jnp.where(tri, 1.0, 0.0).astype(BF16)
    a_hi, a_mid = _split2(a)
    a_lo = (a - a_hi.astype(F32) - a_mid.astype(F32)).astype(BF16)
    acs = (jnp.dot(trib, a_hi, preferred_element_type=F32) + jnp.dot(trib, a_mid, preferred_element_type=F32)
           + jnp.dot(trib, a_lo, preferred_element_type=F32))
    tot = jnp.sum(a, axis=0, keepdims=True)
    dec_end = jnp.exp(tot - acs)
    eacs = jnp.exp(acs)
    etot = jnp.exp(tot)
    acst[...] = acs.T
    stack = jnp.concatenate([dt, dt * dec_end, eacs, jnp.broadcast_to(etot, (8, n_heads)),
                             jnp.broadcast_to(hp[2:3], (8, n_heads))], axis=0)
    s_hi, s_lo = _split2(stack)
    lbuf[0] = s_hi
    lbuf[1] = s_lo

    def xs_body(j, carry):
        col = pl.multiple_of(j * 128, 128)
        v = conv_block(col)
        eblk = e_ref[:, pl.ds(col, 128)]
        ex = (jnp.dot(lbuf[0], eblk, preferred_element_type=F32)
              + jnp.dot(lbuf[1], eblk, preferred_element_type=F32))
        xc[:, pl.ds(col, 128)] = v
        xdt[:, pl.ds(col, 128)] = (v * ex[0:q]).astype(BF16)
        xdd[:, pl.ds(col, 128)] = (v * ex[q:2 * q]).astype(BF16)
        eacsx[:, pl.ds(col, 128)] = ex[2 * q:3 * q]
        rowx[:, pl.ds(col, 128)] = ex[3 * q:3 * q + 16]
        return carry
    lax.fori_loop(0, di // 128, xs_body, 0)

    def b_body(j, carry):
        col = pl.multiple_of(j * 128, 128)
        v = conv_block(di + col)
        bm[:, pl.ds(col, 128)] = v.astype(BF16)
        bmt[j] = v.T.astype(BF16)
        return carry
    lax.fori_loop(0, n_grp, b_body, 0)

    def c_body(j, carry):
        col = pl.multiple_of(j * 128, 128)
        cm[:, pl.ds(col, 128)] = conv_block(di + n_grp * SSD_STATE + col).astype(BF16)
        return carry
    lax.fori_loop(0, n_grp, c_body, 0)

    lane = lax.broadcasted_iota(jnp.int32, (q, 128), 1)
    left = lane < SSD_HEAD_DIM
    for pb in range(di // 128):
        grp = pb // blocks_per_group
        gl = slice(grp * SSD_STATE, (grp + 1) * SSD_STATE)
        pc = slice(pb * 128, (pb + 1) * 128)
        sc = slice((pb % blocks_per_group) * 128, (pb % blocks_per_group + 1) * 128)
        cmg = cm[:, gl]
        if pb % blocks_per_group == 0:
            cbm = lax.dot_general(cmg, bm[:, gl], (((1,), (1,)), ((), ())), preferred_element_type=F32)
        ms = []
        for h in (2 * pb, 2 * pb + 1):
            colb = jnp.broadcast_to(acs[:, h:h + 1], (q, q))
            seg = jnp.minimum(colb - acst[h:h + 1, :], 0.0)
            ms.append(jnp.where(tri, cbm * jnp.exp(seg), 0.0).astype(BF16))
        xp = xdt[:, pc]
        zero = jnp.zeros_like(xp)
        rhs = jnp.concatenate([jnp.where(left, xp, zero), jnp.where(left, zero, xp)], axis=0)
        yd = jnp.dot(jnp.concatenate(ms, axis=1), rhs, preferred_element_type=F32)
        s_old = state[grp, :, sc]
        yo = jnp.dot(cmg, s_old.astype(BF16), preferred_element_type=F32) * eacsx[:, pc]
        y_ref[:, pc] = (yd + yo + xc[:, pc] * rowx[8:9, pc]).astype(BF16)
        state[grp, :, sc] = s_old * rowx[0:1, pc] + jnp.dot(bmt[grp], xdd[:, pc], preferred_element_type=F32)


def _ssd_scan(tl, proj, dt2, conv_w, conv_b, head_params, expand):
    n_tok = tl.n_tok
    q = SSD_CHUNK
    n_heads = dt2.shape[-1]
    di = n_heads * SSD_HEAD_DIM
    gn = SSD_GROUPS * SSD_STATE
    conv_ch = di + 2 * gn
    assert (2 * di) % (2 * gn) == 0 and (di // SSD_GROUPS) % 128 == 0
    ncc, ncl = tl.ctx_len // q, tl.seq // q
    bc_block0 = 2 * di // (2 * gn)
    hpc = q // CONV_HALO
    last_halo = n_tok // CONV_HALO - 1

    def blk(d, b, c):
        fwd = d == 0
        is_ctx = c < ncc
        t_ctx = jnp.where(fwd, c, ncc - 1 - c)
        t_lat = jnp.where(fwd, c - ncc, ncl - 1 - (c - ncc))
        return jnp.where(is_ctx, b * ncc + t_ctx, tl.bsz * ncc + b * ncl + t_lat)

    prev = lambda d, b, c: jnp.maximum(blk(d, b, c) * hpc - 1, 0)
    nxt = lambda d, b, c: jnp.minimum((blk(d, b, c) + 1) * hpc, last_halo)
    kern = functools.partial(_scan_kernel, ncc=ncc, ncl=ncl)
    return pl.pallas_call(
        kern,
        grid=(2, tl.bsz, ncc + ncl),
        in_specs=[
            pl.BlockSpec((q, di), lambda d, b, c: (blk(d, b, c), 1)),
            pl.BlockSpec((CONV_HALO, di), lambda d, b, c: (prev(d, b, c), 1)),
            pl.BlockSpec((CONV_HALO, di), lambda d, b, c: (nxt(d, b, c), 1)),
            pl.BlockSpec((q, 2 * gn), lambda d, b, c: (blk(d, b, c), bc_block0 + d)),
            pl.BlockSpec((CONV_HALO, 2 * gn), lambda d, b, c: (prev(d, b, c), bc_block0 + d)),
            pl.BlockSpec((CONV_HALO, 2 * gn), lambda d, b, c: (nxt(d, b, c), bc_block0 + d)),
            pl.BlockSpec((None, q, n_heads), lambda d, b, c: (d, blk(d, b, c), 0)),
            pl.BlockSpec((None, SSD_CONV, conv_ch), lambda d, b, c: (d, 0, 0)),
            pl.BlockSpec((None, 1, conv_ch), lambda d, b, c: (d, 0, 0)),
            pl.BlockSpec((None, 8, n_heads), lambda d, b, c: (d, 0, 0)),
            pl.BlockSpec((n_heads, di), lambda d, b, c: (0, 0)),
        ],
        out_specs=pl.BlockSpec((None, q, di), lambda d, b, c: (d, blk(d, b, c), 0)),
        out_shape=jax.ShapeDtypeStruct((2, n_tok, di), BF16),
        scratch_shapes=[
            pltpu.VMEM((q + 2 * CONV_HALO, conv_ch), F32),
            pltpu.VMEM((q, di), F32),
            pltpu.VMEM((q, di), BF16),
            pltpu.VMEM((q, di), BF16),
            pltpu.VMEM((q, di), F32),
            pltpu.VMEM((16, di), F32),
            pltpu.VMEM((q, gn), BF16),
            pltpu.VMEM((q, gn), BF16),
            pltpu.VMEM((SSD_GROUPS, SSD_STATE, q), BF16),
            pltpu.VMEM((n_heads, q), F32),
            pltpu.VMEM((2, 3 * q + 16, n_heads), BF16),
            pltpu.VMEM((SSD_GROUPS, SSD_STATE, di // SSD_GROUPS), F32),
        ],
        compiler_params=_cparams(("arbitrary", "arbitrary", "arbitrary")),
        name="ssd_scan",
    )(proj, proj, proj, proj, proj, proj, dt2, conv_w, conv_b.reshape(2, 1, conv_ch), head_params, expand)


def _outproj_kernel(y0_ref, y1_ref, z_ref, ng_ref, w_ref, x_ref, gate_ref, o_ref, ybuf):
    @pl.when(pl.program_id(1) == 0)
    def _prologue():
        gw = ybuf.shape[1] // SSD_GROUPS
        for grp in range(SSD_GROUPS):
            cols = slice(grp * gw, (grp + 1) * gw)
            y = (y0_ref[:, cols].astype(F32) + y1_ref[:, cols].astype(F32)) * _silu(z_ref[:, cols].astype(F32))
            ms = jnp.mean(y * y, axis=-1, keepdims=True)
            ybuf[:, cols] = (y * lax.rsqrt(ms + NORM_EPS) * ng_ref[:, cols]).astype(BF16)

    o_ref[...] = x_ref[...] + gate_ref[...] * jnp.dot(ybuf[...], w_ref[...], preferred_element_type=F32)


def _outproj(tl, xall, modtab, layer, y2, proj, norm_g, out_w):
    d = tl.d
    di = out_w.shape[0]
    tn = OUTPROJ_TN
    gate_spec = pl.BlockSpec(
        (None, 1, tn), lambda i, j: ((layer * COND_ROWS + tl.seg(i)) * N_MOD + 2, 0, j))
    return pl.pallas_call(
        _outproj_kernel,
        grid=(tl.n_tiles, d // tn),
        in_specs=[
            pl.BlockSpec((None, TM, di), lambda i, j: (0, i, 0)),
            pl.BlockSpec((None, TM, di), lambda i, j: (1, i, 0)),
            pl.BlockSpec((TM, di), lambda i, j: (i, 0)),
            pl.BlockSpec((1, di), lambda i, j: (0, 0)),
            pl.BlockSpec((di, tn), lambda i, j: (0, j)),
            pl.BlockSpec((TM, tn), lambda i, j: (i, j)),
            gate_spec,
        ],
        out_specs=pl.BlockSpec((TM, tn), lambda i, j: (i, j)),
        out_shape=jax.ShapeDtypeStruct((tl.n_tok, d), F32),
        scratch_shapes=[pltpu.VMEM((TM, di), BF16)],
        compiler_params=_cparams(("arbitrary", "arbitrary")),
        name="ssd_outproj",
    )(y2, y2, proj, norm_g.reshape(1, di), out_w, xall, modtab)


def _ssd_layer(tl, xall, modtab, layer, norm_g, in_w, conv_w, conv_b, a_log, dt_bias, d_skip, ssd_norm_g, out_w):
    n_heads = a_log.shape[-1]
    di = n_heads * SSD_HEAD_DIM
    gn = SSD_GROUPS * SSD_STATE
    dir_cols = 2 * gn + n_heads
    base0 = 2 * di
    base1 = base0 + dir_cols
    w_main = jnp.concatenate(
        [in_w[:, :base0], in_w[:, base0:base0 + 2 * gn], in_w[:, base1:base1 + 2 * gn]], axis=1).astype(BF16)
    w_dt = jnp.concatenate(
        [in_w[:, base0 + 2 * gn:base0 + dir_cols], in_w[:, base1 + 2 * gn:base1 + dir_cols]], axis=1).astype(BF16)
    proj, dt_raw = _inproj(tl, xall, modtab, layer, norm_g, w_main, w_dt)
    dt2 = dt_raw.reshape(tl.n_tok, 2, n_heads).transpose(1, 0, 2)
    head_params = jnp.zeros((2, 8, n_heads), F32).at[:, 0].set(a_log).at[:, 1].set(dt_bias).at[:, 2].set(d_skip)
    expand = (jnp.arange(di, dtype=jnp.int32)[None, :] // SSD_HEAD_DIM
              == jnp.arange(n_heads, dtype=jnp.int32)[:, None]).astype(BF16)
    y2 = _ssd_scan(tl, proj, dt2, conv_w, conv_b, head_params, expand)
    return _outproj(tl, xall, modtab, layer, y2, proj, ssd_norm_g, out_w.astype(BF16))


def kernel(x, c, ctx, c_ctx, mod_w, mod_b, norm_mix_g, norm_ffn_g, pool_w, pool_b, pool_scale, ssd_in_w,
           ssd_conv_w, ssd_conv_b, ssd_a_log, ssd_dt_bias, ssd_d, ssd_norm_g, ssd_out_w, router_w, router_bias,
           moe_w_gate, moe_w_up, moe_w_down, final_norm_g):
    bsz, seq, d = x.shape
    ctx_len = ctx.shape[1]
    depth = mod_w.shape[0]
    assert bsz + 1 <= COND_ROWS
    tl = _Tiles(bsz, ctx_len, seq, d)
    cond = jnp.concatenate([c_ctx[None, :], c, jnp.zeros((COND_ROWS - 1 - bsz, d), F32)], axis=0)
    modtab = _adaln_table(cond, mod_w, mod_b)
    xall = jnp.concatenate([ctx.reshape(bsz * ctx_len, d), x.reshape(bsz * seq, d)], axis=0)
    for i in range(depth):
        j = i // 2
        if i % 2 == 0:
            xall = _pool_layer(tl, xall, modtab, i, norm_mix_g[i], pool_w[j], pool_b[j], pool_scale[j])
        else:
            xall = _ssd_layer(tl, xall, modtab, i, norm_mix_g[i], ssd_in_w[j], ssd_conv_w[j], ssd_conv_b[j],
                              ssd_a_log[j], ssd_dt_bias[j], ssd_d[j], ssd_norm_g[j], ssd_out_w[j])
        xall = _moe_layer(tl, xall, modtab, i, norm_ffn_g[i], router_w, router_bias,
                          moe_w_gate[i], moe_w_up[i], moe_w_down[i], final_norm_g, i == depth - 1)
    return xall[tl.n_ctx:].reshape(bsz, seq, d)
```

```python
import functools

import jax
import jax.numpy as jnp
from jax import lax
from jax.experimental import pallas as pl
from jax.experimental.pallas import tpu as pltpu

F32 = jnp.float32
BF16 = jnp.bfloat16

GRID_W = 64
NORM_EPS = 1e-6
N_MOD = 6
POOL_GROUPS = 4
POOL_WINDOWS_1D = (2, 4, 8, 16)
POOL_WINDOWS_2D = ((1, 2), (2, 2), (2, 4), (4, 4))
SSD_HEAD_DIM = 64
SSD_GROUPS = 8
SSD_STATE = 128
SSD_CONV = 4
SSD_CHUNK = 128
N_EXPERTS = 16
N_EXPERT_GROUPS = 4
EXPERTS_PER_GROUP = 4
TOP_K = 2

TM = 512
MOE_ROWS = 256
COND_ROWS = 8
VMEM_LIMIT = 56 * 1024 * 1024


def _silu(v):
    return v / (1.0 + jnp.exp(-v))


def _norm_mod(x, g, shift, scale):
    ms = jnp.mean(x * x, axis=-1, keepdims=True)
    y = x * lax.rsqrt(ms + NORM_EPS) * g
    return y * (1.0 + scale) + shift


def _cparams(sem):
    return pltpu.CompilerParams(dimension_semantics=sem, vmem_limit_bytes=VMEM_LIMIT)


def _adaln_kernel(cond_ref, w_ref, b_ref, o_ref):
    s = _silu(cond_ref[...]).astype(BF16)
    o_ref[...] = jnp.dot(s, w_ref[...].astype(BF16), preferred_element_type=F32) + b_ref[...]


def _adaln_table(cond, mod_w, mod_b):
    depth, d, nd = mod_w.shape
    tn = 1024 if nd % 1024 == 0 else nd
    out = pl.pallas_call(
        _adaln_kernel,
        grid=(depth, nd // tn),
        in_specs=[
            pl.BlockSpec((COND_ROWS, d), lambda l, j: (0, 0)),
            pl.BlockSpec((None, d, tn), lambda l, j: (l, 0, j)),
            pl.BlockSpec((None, 1, tn), lambda l, j: (l, 0, j)),
        ],
        out_specs=pl.BlockSpec((None, COND_ROWS, tn), lambda l, j: (l, 0, j)),
        out_shape=jax.ShapeDtypeStruct((depth, COND_ROWS, nd), F32),
        compiler_params=_cparams(("arbitrary", "arbitrary")),
        name="adaln",
    )(cond, mod_w, mod_b.reshape(depth, 1, nd))
    return out.reshape(depth * COND_ROWS * N_MOD, 1, d)


class _Tiles:
    def __init__(self, bsz, ctx_len, seq, d):
        assert bsz * ctx_len == TM and seq % TM == 0
        self.bsz, self.ctx_len, self.seq, self.d = bsz, ctx_len, seq, d
        self.n_ctx = bsz * ctx_len
        self.n_tok = self.n_ctx + bsz * seq
        self.n_tiles = self.n_tok // TM
        self.tiles_per_batch = seq // TM

    def seg(self, i):
        return jnp.where(i == 0, 0, 1 + (i - 1) // self.tiles_per_batch)

    def mod_spec(self, layer, k, tile_of=lambda *idx: idx[0]):
        def imap(*idx):
            return ((layer * COND_ROWS + self.seg(tile_of(*idx))) * N_MOD + k, 0, 0)
        return pl.BlockSpec((None, 1, self.d), imap)


POOL_MARGIN = 8
POOL_PREV = 2 * GRID_W
POOL_NEXT = GRID_W


def _pool_kernel(xp_ref, xc_ref, xn_ref, g_ref, sh_ref, sc_ref, gate_ref, w_ref, b_ref, ps_ref,
                 o_ref, hbuf, cpbuf, dbuf, *, ctx_len, tiles_per_batch, rows_per_batch):
    i = pl.program_id(0)
    gd = w_ref.shape[-1]
    g, sh, sc = g_ref[...], sh_ref[...], sc_ref[...]
    base = POOL_MARGIN + POOL_PREV
    zeros_m = jnp.zeros((POOL_MARGIN, hbuf.shape[1]), F32)
    hbuf[0:POOL_MARGIN, :] = zeros_m
    hbuf[base + TM + POOL_NEXT:base + TM + POOL_NEXT + POOL_MARGIN, :] = zeros_m
    hbuf[POOL_MARGIN:base, :] = _norm_mod(xp_ref[...], g, sh, sc)
    hbuf[base:base + TM, :] = _norm_mod(xc_ref[...], g, sh, sc)
    hbuf[base + TM:base + TM + POOL_NEXT, :] = _norm_mod(xn_ref[...], g, sh, sc)

    def finish(grp):
        cols = slice(grp * gd, (grp + 1) * gd)
        y = jnp.dot(dbuf[...], w_ref[grp].astype(BF16), preferred_element_type=F32)
        y = (y + b_ref[:, cols]) * ps_ref[:, cols]
        o_ref[:, cols] = xc_ref[:, cols] + gate_ref[:, cols] * y

    @pl.when(i == 0)
    def _ctx():
        ch = GRID_W
        for grp in range(POOL_GROUPS):
            cols = slice(grp * gd, (grp + 1) * gd)
            w = POOL_WINDOWS_1D[grp]
            lo, hi = w // 2, w - 1 - w // 2
            for k in range(TM // ch):
                pos = (k * ch) % ctx_len + lax.broadcasted_iota(jnp.int32, (ch, 1), 0)
                acc = jnp.zeros((ch, gd), F32)
                for dd in range(-lo, hi + 1):
                    v = hbuf[base + k * ch + dd:base + k * ch + dd + ch, cols]
                    ok = (pos + dd >= 0) & (pos + dd < ctx_len)
                    acc = acc + jnp.where(ok, v, 0.0)
                cnt = jnp.minimum(pos + hi, ctx_len - 1) - jnp.maximum(pos - lo, 0) + 1
                t = acc / cnt.astype(F32)
                hcur = hbuf[base + k * ch:base + (k + 1) * ch, cols]
                dbuf[k * ch:(k + 1) * ch, :] = (t - hcur).astype(BF16)
            finish(grp)

    @pl.when(i > 0)
    def _latent():
        tile_in_batch = (i - 1) % tiles_per_batch
        row0 = tile_in_batch * (TM // GRID_W)
        col = lax.broadcasted_iota(jnp.int32, (GRID_W, 1), 0)
        n_rows = TM // GRID_W
        for grp in range(POOL_GROUPS):
            cols = slice(grp * gd, (grp + 1) * gd)
            wr, wc = POOL_WINDOWS_2D[grp]
            lo_r, hi_r = wr // 2, wr - 1 - wr // 2
            lo_c, hi_c = wc // 2, wc - 1 - wc // 2
            cnt_c = (jnp.minimum(col + hi_c, GRID_W - 1) - jnp.maximum(col - lo_c, 0) + 1).astype(F32)
            for rr in range(2 - lo_r, 2 + n_rows + hi_r):
                start = POOL_MARGIN + rr * GRID_W
                acc = jnp.zeros((GRID_W, gd), F32)
                for dc in range(-lo_c, hi_c + 1):
                    v = hbuf[start + dc:start + dc + GRID_W, cols]
                    ok = (col + dc >= 0) & (col + dc < GRID_W)
                    acc = acc + jnp.where(ok, v, 0.0)
                cpbuf[rr * GRID_W:(rr + 1) * GRID_W, :] = acc / cnt_c
            for r in range(n_rows):
                acc = jnp.zeros((GRID_W, gd), F32)
                cnt_r = jnp.zeros((GRID_W, 1), F32)
                for dr in range(-lo_r, hi_r + 1):
                    grow = row0 + r + dr
                    ok = (grow >= 0) & (grow < rows_per_batch)
                    v = cpbuf[(2 + r + dr) * GRID_W:(3 + r + dr) * GRID_W, :]
                    acc = acc + jnp.where(ok, v, 0.0)
                    cnt_r = cnt_r + jnp.where(ok, 1.0, 0.0)
                hcur = hbuf[base + r * GRID_W:base + (r + 1) * GRID_W, cols]
                dbuf[r * GRID_W:(r + 1) * GRID_W, :] = (acc / cnt_r - hcur).astype(BF16)
            finish(grp)


def _pool_layer(tl, xall, modtab, layer, norm_g, pool_w, pool_b, pool_scale):
    d = tl.d
    gd = d // POOL_GROUPS
    n_prev_blocks = TM // POOL_PREV
    n_next_blocks = TM // POOL_NEXT
    last_next = tl.n_tok // POOL_NEXT - 1
    kern = functools.partial(_pool_kernel, ctx_len=tl.ctx_len, tiles_per_batch=tl.tiles_per_batch,
                             rows_per_batch=tl.seq // GRID_W)
    vec = pl.BlockSpec((1, d), lambda i: (0, 0))
    buf_rows = 2 * POOL_MARGIN + POOL_PREV + TM + POOL_NEXT
    return pl.pallas_call(
        kern,
        grid=(tl.n_tiles,),
        in_specs=[
            pl.BlockSpec((POOL_PREV, d), lambda i: (jnp.maximum(i * n_prev_blocks - 1, 0), 0)),
            pl.BlockSpec((TM, d), lambda i: (i, 0)),
            pl.BlockSpec((POOL_NEXT, d), lambda i: (jnp.minimum((i + 1) * n_next_blocks, last_next), 0)),
            vec,
            tl.mod_spec(layer, 0), tl.mod_spec(layer, 1), tl.mod_spec(layer, 2),
            pl.BlockSpec((POOL_GROUPS, gd, gd), lambda i: (0, 0, 0)),
            vec, vec,
        ],
        out_specs=pl.BlockSpec((TM, d), lambda i: (i, 0)),
        out_shape=jax.ShapeDtypeStruct((tl.n_tok, d), F32),
        scratch_shapes=[
            pltpu.VMEM((buf_rows, d), F32),
            pltpu.VMEM((POOL_PREV + TM + POOL_NEXT, gd), F32),
            pltpu.VMEM((TM, gd), BF16),
        ],
        compiler_params=_cparams(("arbitrary",)),
        name="pool_layer",
    )(xall, xall, xall, norm_g.reshape(1, d), modtab, modtab, modtab, pool_w,
      pool_b.reshape(1, d), pool_scale.reshape(1, d))


def _router_kernel(x_ref, g_ref, sh_ref, sc_ref, rwt_ref, rb_ref, h_ref, r_ref):
    h = _norm_mod(x_ref[...], g_ref[...], sh_ref[...], sc_ref[...])
    h_ref[...] = h
    logits = lax.dot_general(rwt_ref[...], h, (((1,), (1,)), ((), ())),
                             precision=lax.Precision.HIGHEST, preferred_element_type=F32)
    scores = 1.0 / (1.0 + jnp.exp(-logits))
    sel = scores + rb_ref[...]
    tm = sel.shape[1]
    srow = [scores[e:e + 1, :] for e in range(N_EXPERTS)]
    vrow = [sel[e:e + 1, :] for e in range(N_EXPERTS)]
    best_g = jnp.zeros((1, tm), jnp.int32)
    best_s = None
    for grp in range(N_EXPERT_GROUPS):
        v = vrow[grp * EXPERTS_PER_GROUP:(grp + 1) * EXPERTS_PER_GROUP]
        gs = None
        for a in range(EXPERTS_PER_GROUP):
            for b in range(a + 1, EXPERTS_PER_GROUP):
                p = v[a] + v[b]
                gs = p if gs is None else jnp.maximum(gs, p)
        if best_s is None:
            best_s = gs
        else:
            upd = gs > best_s
            best_s = jnp.where(upd, gs, best_s)
            best_g = jnp.where(upd, grp, best_g)
    neg = jnp.full((1, tm), -jnp.inf, F32)
    masked = [jnp.where(best_g == (e // EXPERTS_PER_GROUP), vrow[e], neg) for e in range(N_EXPERTS)]
    m1, i1, s1 = masked[0], jnp.zeros((1, tm), jnp.int32), srow[0]
    for e in range(1, N_EXPERTS):
        upd = masked[e] > m1
        m1 = jnp.where(upd, masked[e], m1)
        i1 = jnp.where(upd, e, i1)
        s1 = jnp.where(upd, srow[e], s1)
    m2, i2, s2 = neg, jnp.full((1, tm), -1, jnp.int32), jnp.zeros((1, tm), F32)
    for e in range(N_EXPERTS):
        cand = i1 != e
        upd = cand & ((masked[e] > m2) | (i2 < 0))
        m2 = jnp.where(upd, masked[e], m2)
        i2 = jnp.where(upd, e, i2)
        s2 = jnp.where(upd, srow[e], s2)
    tot = s1 + s2
    w1, w2 = s1 / tot, s2 / tot
    first_lo = i1 < i2
    e_lo = jnp.where(first_lo, i1, i2).astype(F32)
    e_hi = jnp.where(first_lo, i2, i1).astype(F32)
    w_lo = jnp.where(first_lo, w1, w2)
    w_hi = jnp.where(first_lo, w2, w1)
    zero = jnp.zeros((1, tm), F32)
    r_ref[...] = jnp.concatenate([e_lo, e_hi, w_lo, w_hi, zero, zero, zero, zero], axis=0)


def _router(tl, xall, modtab, layer, norm_g, router_w, router_bias):
    d = tl.d
    vec = pl.BlockSpec((1, d), lambda i: (0, 0))
    return pl.pallas_call(
        _router_kernel,
        grid=(tl.n_tiles,),
        in_specs=[
            pl.BlockSpec((TM, d), lambda i: (i, 0)),
            vec, tl.mod_spec(layer, 3), tl.mod_spec(layer, 4),
            pl.BlockSpec((N_EXPERTS, d), lambda i: (0, 0)),
            pl.BlockSpec((N_EXPERTS, 1), lambda i: (0, 0)),
        ],
        out_specs=[
            pl.BlockSpec((TM, d), lambda i: (i, 0)),
            pl.BlockSpec((8, TM), lambda i: (0, i)),
        ],
        out_shape=[
            jax.ShapeDtypeStruct((tl.n_tok, d), F32),
            jax.ShapeDtypeStruct((8, tl.n_tok), F32),
        ],
        compiler_params=_cparams(("arbitrary",)),
        name="moe_router",
    )(xall, norm_g.reshape(1, d), modtab, modtab, router_w.T, router_bias.reshape(N_EXPERTS, 1))


def _moe_tiles(n_tok):
    return -(-(TOP_K * n_tok + N_EXPERTS * (MOE_ROWS - 1)) // MOE_ROWS) + 1


def _dispatch(route, n_tok):
    e = jnp.stack([route[0], route[1]], axis=1).astype(jnp.int32).reshape(TOP_K * n_tok)
    onehot = (e[:, None] == jnp.arange(N_EXPERTS, dtype=jnp.int32)[None, :]).astype(jnp.int32)
    csum = jnp.cumsum(onehot, axis=0)
    rank = jnp.sum(csum * onehot, axis=1) - 1
    cnt = csum[-1]
    ntile = (cnt + MOE_ROWS - 1) // MOE_ROWS
    tile_end = jnp.cumsum(ntile)
    tile_start = tile_end - ntile
    pos = jnp.sum(onehot * tile_start[None, :], axis=1) * MOE_ROWS + rank
    n_tiles = _moe_tiles(n_tok)
    n_used = tile_end[-1]
    tiles = jnp.minimum(jnp.arange(n_tiles, dtype=jnp.int32), n_used - 1)
    tile_expert = jnp.sum((tiles[:, None] >= tile_end[None, :]).astype(jnp.int32), axis=1)
    tok = jnp.arange(TOP_K * n_tok, dtype=jnp.int32) // TOP_K
    slot_tok = jnp.zeros((n_tiles * MOE_ROWS,), jnp.int32).at[pos].set(tok)
    pos = pos.reshape(n_tok, TOP_K)
    return pos[:, 0], pos[:, 1], slot_tok, tile_expert.astype(jnp.int32), n_used.reshape(1).astype(jnp.int32)


def _row_copy(src_hbm, row, dst, r, sem):
    return pltpu.make_async_copy(src_hbm.at[pl.ds(row, 1), :], dst.at[pl.ds(r, 1), :], sem)


def _wait_rows(src_hbm, dst, sem):
    pltpu.make_async_copy(src_hbm.at[pl.ds(0, dst.shape[0]), :], dst, sem).wait()


def _ffn_kernel(te_ref, nused_ref, stok_ref, h_hbm, wg_ref, wu_ref, wd_ref, o_ref,
                xbuf, wgb, wub, wdb, sem):
    t = pl.program_id(0)
    slot = t % 2
    n_used = nused_ref[0]

    @pl.when(t == 0)
    def _prime():
        for r in range(MOE_ROWS):
            _row_copy(h_hbm, stok_ref[r], xbuf.at[0], r, sem.at[0]).start()

    @pl.when((t < n_used) & ((t == 0) | (te_ref[t] != te_ref[jnp.maximum(t - 1, 0)])))
    def _new_expert():
        wgb[...] = wg_ref[...].astype(BF16)
        wub[...] = wu_ref[...].astype(BF16)
        wdb[...] = wd_ref[...].astype(BF16)

    @pl.when(t <= n_used)
    def _wait():
        _wait_rows(h_hbm, xbuf.at[slot], sem.at[slot])

    @pl.when(t < n_used)
    def _compute():
        for r in range(MOE_ROWS):
            _row_copy(h_hbm, stok_ref[(t + 1) * MOE_ROWS + r], xbuf.at[1 - slot], r, sem.at[1 - slot]).start()
        xb = xbuf[slot].astype(BF16)
        hg = jnp.dot(xb, wgb[...], preferred_element_type=F32)
        hu = jnp.dot(xb, wub[...], preferred_element_type=F32)
        act = (_silu(hg) * hu).astype(BF16)
        o_ref[...] = jnp.dot(act, wdb[...], preferred_element_type=F32)

    @pl.when(t >= n_used)
    def _idle():
        o_ref[...] = jnp.zeros_like(o_ref)


def _moe_ffn(h2, slot_tok, tile_expert, n_used, wg, wu, wd):
    n_tok, d = h2.shape
    f = wg.shape[-1]
    n_tiles = _moe_tiles(n_tok)
    wspec = lambda shape: pl.BlockSpec((None,) + shape, lambda t, te, nu, st: (te[t], 0, 0))
    grid_spec = pltpu.PrefetchScalarGridSpec(
        num_scalar_prefetch=3,
        grid=(n_tiles,),
        in_specs=[pl.BlockSpec(memory_space=pl.ANY), wspec((d, f)), wspec((d, f)), wspec((f, d))],
        out_specs=pl.BlockSpec((MOE_ROWS, d), lambda t, te, nu, st: (t, 0)),
        scratch_shapes=[
            pltpu.VMEM((2, MOE_ROWS, d), F32),
            pltpu.VMEM((d, f), BF16), pltpu.VMEM((d, f), BF16), pltpu.VMEM((f, d), BF16),
            pltpu.SemaphoreType.DMA((2,)),
        ],
    )
    return pl.pallas_call(
        _ffn_kernel,
        grid_spec=grid_spec,
        out_shape=jax.ShapeDtypeStruct((n_tiles * MOE_ROWS, d), F32),
        compiler_params=_cparams(("arbitrary",)),
        name="moe_ffn",
    )(tile_expert, n_used, slot_tok, h2, wg, wu, wd)


COMBINE_TM = 256


def _combine_kernel(p1_ref, p2_ref, x_ref, gate_ref, w_ref, fg_ref, y_hbm, o_ref, ybuf, sem, *, final):
    i = pl.program_id(0)
    n_i = pl.num_programs(0)
    slot = i % 2
    tm = COMBINE_TM

    def gather(tile, dst, dsem):
        def body(r, carry):
            _row_copy(y_hbm, p1_ref[tile * tm + r], dst.at[0], r, dsem).start()
            _row_copy(y_hbm, p2_ref[tile * tm + r], dst.at[1], r, dsem).start()
            return carry
        lax.fori_loop(0, tm, body, 0, unroll=8)

    @pl.when(i == 0)
    def _prime():
        gather(0, ybuf.at[0], sem.at[0])

    @pl.when(i + 1 < n_i)
    def _prefetch():
        gather(i + 1, ybuf.at[1 - slot], sem.at[1 - slot])

    _wait_rows(y_hbm, ybuf.at[slot, 0], sem.at[slot])
    _wait_rows(y_hbm, ybuf.at[slot, 1], sem.at[slot])
    w = w_ref[...]
    y = w[:, 0:1] * ybuf[slot, 0] + w[:, 1:2] * ybuf[slot, 1]
    x = x_ref[...] + gate_ref[...] * y
    if final:
        ms = jnp.mean(x * x, axis=-1, keepdims=True)
        x = x * lax.rsqrt(ms + NORM_EPS) * fg_ref[...]
    o_ref[...] = x


def _moe_combine(tl, xall, y_sorted, pos1, pos2, gates, modtab, layer, final_g, final):
    d = tl.d
    tm = COMBINE_TM
    per_tile = TM // tm
    grid_spec = pltpu.PrefetchScalarGridSpec(
        num_scalar_prefetch=2,
        grid=(tl.n_tok // tm,),
        in_specs=[
            pl.BlockSpec((tm, d), lambda i, p1, p2: (i, 0)),
            tl.mod_spec(layer, 5, tile_of=lambda i, p1, p2: i // per_tile),
            pl.BlockSpec((tm, TOP_K), lambda i, p1, p2: (i, 0)),
            pl.BlockSpec((1, d), lambda i, p1, p2: (0, 0)),
            pl.BlockSpec(memory_space=pl.ANY),
        ],
        out_specs=pl.BlockSpec((tm, d), lambda i, p1, p2: (i, 0)),
        scratch_shapes=[
            pltpu.VMEM((2, TOP_K, tm, d), F32),
            pltpu.SemaphoreType.DMA((2,)),
        ],
    )
    return pl.pallas_call(
        functools.partial(_combine_kernel, final=final),
        grid_spec=grid_spec,
        out_shape=jax.ShapeDtypeStruct((tl.n_tok, d), F32),
        compiler_params=_cparams(("arbitrary",)),
        name="moe_combine",
    )(pos1, pos2, xall, modtab, gates, final_g.reshape(1, d), y_sorted)


def _moe_layer(tl, xall, modtab, layer, norm_g, router_w, router_bias, wg, wu, wd, final_g, final):
    h2, route = _router(tl, xall, modtab, layer, norm_g, router_w, router_bias)
    pos1, pos2, slot_tok, tile_expert, n_used = _dispatch(route, tl.n_tok)
    y_sorted = _moe_ffn(h2, slot_tok, tile_expert, n_used, wg, wu, wd)
    gates = route[2:4].T
    return _moe_combine(tl, xall, y_sorted, pos1, pos2, gates, modtab, layer, final_g, final)


INPROJ_TN = 512
OUTPROJ_TN = 512
CONV_HALO = 16


def _inproj_kernel(x_ref, g_ref, sh_ref, sc_ref, w_ref, wdt_ref, p_ref, dt_ref, hbuf):
    @pl.when(pl.program_id(1) == 0)
    def _prologue():
        h = _norm_mod(x_ref[...], g_ref[...], sh_ref[...], sc_ref[...]).astype(BF16)
        hbuf[...] = h
        dt_ref[...] = jnp.dot(h, wdt_ref[...], preferred_element_type=F32)

    p_ref[...] = jnp.dot(hbuf[...], w_ref[...], preferred_element_type=F32).astype(BF16)


def _inproj(tl, xall, modtab, layer, norm_g, w_main, w_dt):
    d = tl.d
    n_main = w_main.shape[1]
    n_dt = w_dt.shape[1]
    tn = INPROJ_TN
    vec = pl.BlockSpec((1, d), lambda i, j: (0, 0))
    return pl.pallas_call(
        _inproj_kernel,
        grid=(tl.n_tiles, n_main // tn),
        in_specs=[
            pl.BlockSpec((TM, d), lambda i, j: (i, 0)),
            vec, tl.mod_spec(layer, 0), tl.mod_spec(layer, 1),
            pl.BlockSpec((d, tn), lambda i, j: (0, j)),
            pl.BlockSpec((d, n_dt), lambda i, j: (0, 0)),
        ],
        out_specs=[
            pl.BlockSpec((TM, tn), lambda i, j: (i, j)),
            pl.BlockSpec((TM, n_dt), lambda i, j: (i, 0)),
        ],
        out_shape=[
            jax.ShapeDtypeStruct((tl.n_tok, n_main), BF16),
            jax.ShapeDtypeStruct((tl.n_tok, n_dt), F32),
        ],
        scratch_shapes=[pltpu.VMEM((TM, d), BF16)],
        compiler_params=_cparams(("arbitrary", "arbitrary")),
        name="ssd_inproj",
    )(xall, norm_g.reshape(1, d), modtab, modtab, w_main, w_dt)


def _scan_kernel(xs_ref, xsp_ref, xsn_ref, bc_ref, bcp_ref, bcn_ref, dt_ref, cw_ref, cb_ref, hp_ref,
                 dsk_ref, y_ref, ubuf, xc, xcb, bm, cm, cmf, bmt, acst, wt, state, *, ncc, ncl):
    q = SSD_CHUNK
    di = xs_ref.shape[1]
    n_grp = SSD_GROUPS
    blocks_per_group = di // n_grp // 128
    d = pl.program_id(0)
    c = pl.program_id(2)
    is_ctx = c < ncc
    fwd = d == 0
    tch = jnp.where(is_ctx, jnp.where(fwd, c, ncc - 1 - c), jnp.where(fwd, c - ncc, ncl - 1 - (c - ncc)))
    first = tch == 0
    last = tch == jnp.where(is_ctx, ncc - 1, ncl - 1)

    @pl.when(c == 0)
    def _reset():
        state[...] = jnp.zeros_like(state)

    hl = CONV_HALO
    ubuf[0:hl, 0:di] = jnp.where(first, 0.0, xsp_ref[...].astype(F32))
    ubuf[hl:hl + q, 0:di] = xs_ref[...].astype(F32)
    ubuf[hl + q:hl + q + hl, 0:di] = jnp.where(last, 0.0, xsn_ref[...].astype(F32))
    ubuf[0:hl, di:] = jnp.where(first, 0.0, bcp_ref[...].astype(F32))
    ubuf[hl:hl + q, di:] = bc_ref[...].astype(F32)
    ubuf[hl + q:hl + q + hl, di:] = jnp.where(last, 0.0, bcn_ref[...].astype(F32))

    def conv_block(col):
        acc = cb_ref[:, pl.ds(col, 128)]
        for k in range(SSD_CONV):
            off = hl + k - SSD_CONV // 2
            acc = acc + cw_ref[k:k + 1, pl.ds(col, 128)] * ubuf[off:off + q, pl.ds(col, 128)]
        return _silu(acc)

    hp = hp_ref[...]
    a_coef = -jnp.exp(hp[0:1])
    raw = dt_ref[...] + hp[1:2]
    dt = jnp.maximum(raw, 0.0) + jnp.log(1.0 + jnp.exp(-jnp.abs(raw)))
    a = dt * a_coef
    li = lax.broadcasted_iota(jnp.int32, (q, q), 0)
    si = lax.broadcasted_iota(jnp.int32, (q, q), 1)
    tri = jnp.where(fwd, li - si, si - li) >= 0
    trib = jnp.where(tri, 1.0, 0.0).astype(BF16)
    a_hi = a.astype(BF16)
    r1 = a - a_hi.astype(F32)
    a_mid = r1.astype(BF16)
    a_lo = (r1 - a_mid.astype(F32)).astype(BF16)
    acs = (jnp.dot(trib, a_hi, preferred_element_type=F32) + jnp.dot(trib, a_mid, preferred_element_type=F32)
           + jnp.dot(trib, a_lo, preferred_element_type=F32))
    tot = jnp.sum(a, axis=0, keepdims=True)
    eacs = jnp.exp(acs)
    etot = jnp.exp(tot)
    acst[...] = (acs - jnp.log(dt)).T
    wt[...] = (dt * jnp.exp(tot - acs)).T

    def xs_body(j, carry):
        col = pl.multiple_of(j * 128, 128)
        v = conv_block(col)
        xc[:, pl.ds(col, 128)] = v
        xcb[:, pl.ds(col, 128)] = v.astype(BF16)
        return carry
    lax.fori_loop(0, di // 128, xs_body, 0)

    def b_body(j, carry):
        col = pl.multiple_of(j * 128, 128)
        v = conv_block(di + col)
        bm[:, pl.ds(col, 128)] = v.astype(BF16)
        bmt[j] = v.T
        return carry
    lax.fori_loop(0, n_grp, b_body, 0)

    def c_body(j, carry):
        col = pl.multiple_of(j * 128, 128)
        v = conv_block(di + n_grp * SSD_STATE + col)
        cmf[:, pl.ds(col, 128)] = v
        cm[:, pl.ds(col, 128)] = v.astype(BF16)
        return carry
    lax.fori_loop(0, n_grp, c_body, 0)

    lane = lax.broadcasted_iota(jnp.int32, (q, 128), 1)
    left = lane < SSD_HEAD_DIM
    left_row = left[0:1, :]
    for pb in range(di // 128):
        grp = pb // blocks_per_group
        gl = slice(grp * SSD_STATE, (grp + 1) * SSD_STATE)
        pc = slice(pb * 128, (pb + 1) * 128)
        sc = slice((pb % blocks_per_group) * 128, (pb % blocks_per_group + 1) * 128)
        if pb % blocks_per_group == 0:
            cbm = lax.dot_general(cm[:, gl], bm[:, gl], (((1,), (1,)), ((), ())), preferred_element_type=F32)
        h0, h1 = 2 * pb, 2 * pb + 1
        ms, cs, bts = [], [], []
        for h in (h0, h1):
            colb = jnp.broadcast_to(acs[:, h:h + 1], (q, q))
            arg = jnp.where(tri, colb - acst[h:h + 1, :], -jnp.inf)
            ms.append((cbm * jnp.exp(arg)).astype(BF16))
            cs.append((cmf[:, gl] * jnp.broadcast_to(eacs[:, h:h + 1], (q, SSD_STATE))).astype(BF16))
            bts.append((bmt[grp] * wt[h:h + 1, :]).astype(BF16))
        xp = xcb[:, pc]
        zero = jnp.zeros_like(xp)
        xe, xo = jnp.where(left, xp, zero), jnp.where(left, zero, xp)
        s_old = state[grp, :, sc]
        sb = s_old.astype(BF16)
        se, so = jnp.where(left, sb, zero), jnp.where(left, zero, sb)
        lhs = jnp.concatenate(ms + cs, axis=1)
        rhs = jnp.concatenate([xe, xo, se, so], axis=0)
        y = jnp.dot(lhs, rhs, preferred_element_type=F32) + xc[:, pc] * dsk_ref[:, pc]
        y_ref[:, pc] = y.astype(BF16)
        et = jnp.where(left_row, etot[:, h0:h0 + 1], etot[:, h1:h1 + 1])
        upd = jnp.dot(jnp.concatenate(bts, axis=1), jnp.concatenate([xe, xo], axis=0), preferred_element_type=F32)
        state[grp, :, sc] = s_old * et + upd


def _ssd_scan(tl, proj, dt2, conv_w, conv_b, head_params, dskip_lanes):
    n_tok = tl.n_tok
    q = SSD_CHUNK
    n_heads = dt2.shape[-1]
    di = n_heads * SSD_HEAD_DIM
    gn = SSD_GROUPS * SSD_STATE
    conv_ch = di + 2 * gn
    assert (2 * di) % (2 * gn) == 0 and (di // SSD_GROUPS) % 128 == 0
    ncc, ncl = tl.ctx_len // q, tl.seq // q
    bc_block0 = 2 * di // (2 * gn)
    hpc = q // CONV_HALO
    last_halo = n_tok // CONV_HALO - 1

    def blk(d, b, c):
        fwd = d == 0
        is_ctx = c < ncc
        t_ctx = jnp.where(fwd, c, ncc - 1 - c)
        t_lat = jnp.where(fwd, c - ncc, ncl - 1 - (c - ncc))
        return jnp.where(is_ctx, b * ncc + t_ctx, tl.bsz * ncc + b * ncl + t_lat)

    prev = lambda d, b, c: jnp.maximum(blk(d, b, c) * hpc - 1, 0)
    nxt = lambda d, b, c: jnp.minimum((blk(d, b, c) + 1) * hpc, last_halo)
    kern = functools.partial(_scan_kernel, ncc=ncc, ncl=ncl)
    return pl.pallas_call(
        kern,
        grid=(2, tl.bsz, ncc + ncl),
        in_specs=[
            pl.BlockSpec((q, di), lambda d, b, c: (blk(d, b, c), 1)),
            pl.BlockSpec((CONV_HALO, di), lambda d, b, c: (prev(d, b, c), 1)),
            pl.BlockSpec((CONV_HALO, di), lambda d, b, c: (nxt(d, b, c), 1)),
            pl.BlockSpec((q, 2 * gn), lambda d, b, c: (blk(d, b, c), bc_block0 + d)),
            pl.BlockSpec((CONV_HALO, 2 * gn), lambda d, b, c: (prev(d, b, c), bc_block0 + d)),
            pl.BlockSpec((CONV_HALO, 2 * gn), lambda d, b, c: (nxt(d, b, c), bc_block0 + d)),
            pl.BlockSpec((None, q, n_heads), lambda d, b, c: (d, blk(d, b, c), 0)),
            pl.BlockSpec((None, SSD_CONV, conv_ch), lambda d, b, c: (d, 0, 0)),
            pl.BlockSpec((None, 1, conv_ch), lambda d, b, c: (d, 0, 0)),
            pl.BlockSpec((None, 8, n_heads), lambda d, b, c: (d, 0, 0)),
            pl.BlockSpec((None, 1, di), lambda d, b, c: (d, 0, 0)),
        ],
        out_specs=pl.BlockSpec((None, q, di), lambda d, b, c: (d, blk(d, b, c), 0)),
        out_shape=jax.ShapeDtypeStruct((2, n_tok, di), BF16),
        scratch_shapes=[
            pltpu.VMEM((q + 2 * CONV_HALO, conv_ch), F32),
            pltpu.VMEM((q, di), F32),
            pltpu.VMEM((q, di), BF16),
            pltpu.VMEM((q, gn), BF16),
            pltpu.VMEM((q, gn), BF16),
            pltpu.VMEM((q, gn), F32),
            pltpu.VMEM((SSD_GROUPS, SSD_STATE, q), F32),
            pltpu.VMEM((n_heads, q), F32),
            pltpu.VMEM((n_heads, q), F32),
            pltpu.VMEM((SSD_GROUPS, SSD_STATE, di // SSD_GROUPS), F32),
        ],
        compiler_params=_cparams(("arbitrary", "arbitrary", "arbitrary")),
        name="ssd_scan",
    )(proj, proj, proj, proj, proj, proj, dt2, conv_w, conv_b.reshape(2, 1, conv_ch), head_params, dskip_lanes)


def _outproj_kernel(y0_ref, y1_ref, z_ref, ng_ref, w_ref, x_ref, gate_ref, o_ref, ybuf):
    @pl.when(pl.program_id(1) == 0)
    def _prologue():
        gw = ybuf.shape[1] // SSD_GROUPS
        for grp in range(SSD_GROUPS):
            cols = slice(grp * gw, (grp + 1) * gw)
            y = (y0_ref[:, cols].astype(F32) + y1_ref[:, cols].astype(F32)) * _silu(z_ref[:, cols].astype(F32))
            ms = jnp.mean(y * y, axis=-1, keepdims=True)
            ybuf[:, cols] = (y * lax.rsqrt(ms + NORM_EPS) * ng_ref[:, cols]).astype(BF16)

    o_ref[...] = x_ref[...] + gate_ref[...] * jnp.dot(ybuf[...], w_ref[...], preferred_element_type=F32)


def _outproj(tl, xall, modtab, layer, y2, proj, norm_g, out_w):
    d = tl.d
    di = out_w.shape[0]
    tn = OUTPROJ_TN
    gate_spec = pl.BlockSpec(
        (None, 1, tn), lambda i, j: ((layer * COND_ROWS + tl.seg(i)) * N_MOD + 2, 0, j))
    return pl.pallas_call(
        _outproj_kernel,
        grid=(tl.n_tiles, d // tn),
        in_specs=[
            pl.BlockSpec((None, TM, di), lambda i, j: (0, i, 0)),
            pl.BlockSpec((None, TM, di), lambda i, j: (1, i, 0)),
            pl.BlockSpec((TM, di), lambda i, j: (i, 0)),
            pl.BlockSpec((1, di), lambda i, j: (0, 0)),
            pl.BlockSpec((di, tn), lambda i, j: (0, j)),
            pl.BlockSpec((TM, tn), lambda i, j: (i, j)),
            gate_spec,
        ],
        out_specs=pl.BlockSpec((TM, tn), lambda i, j: (i, j)),
        out_shape=jax.ShapeDtypeStruct((tl.n_tok, d), F32),
        scratch_shapes=[pltpu.VMEM((TM, di), BF16)],
        compiler_params=_cparams(("arbitrary", "arbitrary")),
        name="ssd_outproj",
    )(y2, y2, proj, norm_g.reshape(1, di), out_w, xall, modtab)


def _ssd_layer(tl, xall, modtab, layer, norm_g, in_w, conv_w, conv_b, a_log, dt_bias, d_skip, ssd_norm_g, out_w):
    n_heads = a_log.shape[-1]
    di = n_heads * SSD_HEAD_DIM
    gn = SSD_GROUPS * SSD_STATE
    dir_cols = 2 * gn + n_heads
    base0 = 2 * di
    base1 = base0 + dir_cols
    w_main = jnp.concatenate(
        [in_w[:, :base0], in_w[:, base0:base0 + 2 * gn], in_w[:, base1:base1 + 2 * gn]], axis=1).astype(BF16)
    w_dt = jnp.concatenate(
        [in_w[:, base0 + 2 * gn:base0 + dir_cols], in_w[:, base1 + 2 * gn:base1 + dir_cols]], axis=1).astype(BF16)
    proj, dt_raw = _inproj(tl, xall, modtab, layer, norm_g, w_main, w_dt)
    dt2 = dt_raw.reshape(tl.n_tok, 2, n_heads).transpose(1, 0, 2)
    head_params = jnp.zeros((2, 8, n_heads), F32).at[:, 0].set(a_log).at[:, 1].set(dt_bias)
    dskip_lanes = jnp.repeat(d_skip, SSD_HEAD_DIM, axis=-1).reshape(2, 1, di)
    y2 = _ssd_scan(tl, proj, dt2, conv_w, conv_b, head_params, dskip_lanes)
    return _outproj(tl, xall, modtab, layer, y2, proj, ssd_norm_g, out_w.astype(BF16))


def kernel(x, c, ctx, c_ctx, mod_w, mod_b, norm_mix_g, norm_ffn_g, pool_w, pool_b, pool_scale, ssd_in_w,
           ssd_conv_w, ssd_conv_b, ssd_a_log, ssd_dt_bias, ssd_d, ssd_norm_g, ssd_out_w, router_w, router_bias,
           moe_w_gate, moe_w_up, moe_w_down, final_norm_g):
    bsz, seq, d = x.shape
    ctx_len = ctx.shape[1]
    depth = mod_w.shape[0]
    assert bsz + 1 <= COND_ROWS
    tl = _Tiles(bsz, ctx_len, seq, d)
    cond = jnp.concatenate([c_ctx[None, :], c, jnp.zeros((COND_ROWS - 1 - bsz, d), F32)], axis=0)
    modtab = _adaln_table(cond, mod_w, mod_b)
    xall = jnp.concatenate([ctx.reshape(bsz * ctx_len, d), x.reshape(bsz * seq, d)], axis=0)
    for i in range(depth):
        j = i // 2
        if i % 2 == 0:
            xall = _pool_layer(tl, xall, modtab, i, norm_mix_g[i], pool_w[j], pool_b[j], pool_scale[j])
        else:
            xall = _ssd_layer(tl, xall, modtab, i, norm_mix_g[i], ssd_in_w[j], ssd_conv_w[j], ssd_conv_b[j],
                              ssd_a_log[j], ssd_dt_bias[j], ssd_d[j], ssd_norm_g[j], ssd_out_w[j])
        xall = _moe_layer(tl, xall, modtab, i, norm_ffn_g[i], router_w, router_bias,
                          moe_w_gate[i], moe_w_up[i], moe_w_down[i], final_norm_g, i == depth - 1)
    return xall[tl.n_ctx:].reshape(bsz, seq, d)
```

```python
import functools

import jax
import jax.numpy as jnp
from jax import lax
from jax.experimental import pallas as pl
from jax.experimental.pallas import tpu as pltpu

F32 = jnp.float32
BF16 = jnp.bfloat16

GRID_W = 64
NORM_EPS = 1e-6
N_MOD = 6
POOL_GROUPS = 4
POOL_WINDOWS_1D = (2, 4, 8, 16)
POOL_WINDOWS_2D = ((1, 2), (2, 2), (2, 4), (4, 4))
SSD_HEAD_DIM = 64
SSD_GROUPS = 8
SSD_STATE = 128
SSD_CONV = 4
SSD_CHUNK = 128
N_EXPERTS = 16
N_EXPERT_GROUPS = 4
EXPERTS_PER_GROUP = 4
TOP_K = 2

TM = 512
MOE_ROWS = 256
COND_ROWS = 8
VMEM_LIMIT = 56 * 1024 * 1024


def _silu(v):
    return v / (1.0 + jnp.exp(-v))


def _norm_mod(x, g, shift, scale):
    ms = jnp.mean(x * x, axis=-1, keepdims=True)
    y = x * lax.rsqrt(ms + NORM_EPS) * g
    return y * (1.0 + scale) + shift


def _cparams(sem):
    return pltpu.CompilerParams(dimension_semantics=sem, vmem_limit_bytes=VMEM_LIMIT)


def _adaln_kernel(cond_ref, w_ref, b_ref, o_ref):
    s = _silu(cond_ref[...]).astype(BF16)
    o_ref[...] = jnp.dot(s, w_ref[...].astype(BF16), preferred_element_type=F32) + b_ref[...]


def _adaln_table(cond, mod_w, mod_b):
    depth, d, nd = mod_w.shape
    tn = 1024 if nd % 1024 == 0 else nd
    out = pl.pallas_call(
        _adaln_kernel,
        grid=(depth, nd // tn),
        in_specs=[
            pl.BlockSpec((COND_ROWS, d), lambda l, j: (0, 0)),
            pl.BlockSpec((None, d, tn), lambda l, j: (l, 0, j)),
            pl.BlockSpec((None, 1, tn), lambda l, j: (l, 0, j)),
        ],
        out_specs=pl.BlockSpec((None, COND_ROWS, tn), lambda l, j: (l, 0, j)),
        out_shape=jax.ShapeDtypeStruct((depth, COND_ROWS, nd), F32),
        compiler_params=_cparams(("arbitrary", "arbitrary")),
        name="adaln",
    )(cond, mod_w, mod_b.reshape(depth, 1, nd))
    return out.reshape(depth * COND_ROWS * N_MOD, 1, d)


class _Tiles:
    def __init__(self, bsz, ctx_len, seq, d):
        assert bsz * ctx_len == TM and seq % TM == 0
        self.bsz, self.ctx_len, self.seq, self.d = bsz, ctx_len, seq, d
        self.n_ctx = bsz * ctx_len
        self.n_tok = self.n_ctx + bsz * seq
        self.n_tiles = self.n_tok // TM
        self.tiles_per_batch = seq // TM

    def seg(self, i):
        return jnp.where(i == 0, 0, 1 + (i - 1) // self.tiles_per_batch)

    def mod_spec(self, layer, k, tile_of=lambda *idx: idx[0]):
        def imap(*idx):
            return ((layer * COND_ROWS + self.seg(tile_of(*idx))) * N_MOD + k, 0, 0)
        return pl.BlockSpec((None, 1, self.d), imap)


POOL_MARGIN = 8
POOL_PREV = 2 * GRID_W
POOL_NEXT = GRID_W


def _pool_kernel(xp_ref, xc_ref, xn_ref, g_ref, sh_ref, sc_ref, gate_ref, w_ref, b_ref, ps_ref,
                 o_ref, hbuf, cpbuf, dbuf, *, ctx_len, tiles_per_batch, rows_per_batch):
    i = pl.program_id(0)
    gd = w_ref.shape[-1]
    g, sh, sc = g_ref[...], sh_ref[...], sc_ref[...]
    base = POOL_MARGIN + POOL_PREV
    zeros_m = jnp.zeros((POOL_MARGIN, hbuf.shape[1]), F32)
    hbuf[0:POOL_MARGIN, :] = zeros_m
    hbuf[base + TM + POOL_NEXT:base + TM + POOL_NEXT + POOL_MARGIN, :] = zeros_m
    hbuf[POOL_MARGIN:base, :] = _norm_mod(xp_ref[...], g, sh, sc)
    hbuf[base:base + TM, :] = _norm_mod(xc_ref[...], g, sh, sc)
    hbuf[base + TM:base + TM + POOL_NEXT, :] = _norm_mod(xn_ref[...], g, sh, sc)

    def finish(grp):
        cols = slice(grp * gd, (grp + 1) * gd)
        y = jnp.dot(dbuf[...], w_ref[grp].astype(BF16), preferred_element_type=F32)
        y = (y + b_ref[:, cols]) * ps_ref[:, cols]
        o_ref[:, cols] = xc_ref[:, cols] + gate_ref[:, cols] * y

    @pl.when(i == 0)
    def _ctx():
        ch = GRID_W
        for grp in range(POOL_GROUPS):
            cols = slice(grp * gd, (grp + 1) * gd)
            w = POOL_WINDOWS_1D[grp]
            lo, hi = w // 2, w - 1 - w // 2
            for k in range(TM // ch):
                pos = (k * ch) % ctx_len + lax.broadcasted_iota(jnp.int32, (ch, 1), 0)
                acc = jnp.zeros((ch, gd), F32)
                for dd in range(-lo, hi + 1):
                    v = hbuf[base + k * ch + dd:base + k * ch + dd + ch, cols]
                    ok = (pos + dd >= 0) & (pos + dd < ctx_len)
                    acc = acc + jnp.where(ok, v, 0.0)
                cnt = jnp.minimum(pos + hi, ctx_len - 1) - jnp.maximum(pos - lo, 0) + 1
                t = acc / cnt.astype(F32)
                hcur = hbuf[base + k * ch:base + (k + 1) * ch, cols]
                dbuf[k * ch:(k + 1) * ch, :] = (t - hcur).astype(BF16)
            finish(grp)

    @pl.when(i > 0)
    def _latent():
        tile_in_batch = (i - 1) % tiles_per_batch
        row0 = tile_in_batch * (TM // GRID_W)
        col = lax.broadcasted_iota(jnp.int32, (GRID_W, 1), 0)
        n_rows = TM // GRID_W
        for grp in range(POOL_GROUPS):
            cols = slice(grp * gd, (grp + 1) * gd)
            wr, wc = POOL_WINDOWS_2D[grp]
            lo_r, hi_r = wr // 2, wr - 1 - wr // 2
            lo_c, hi_c = wc // 2, wc - 1 - wc // 2
            cnt_c = (jnp.minimum(col + hi_c, GRID_W - 1) - jnp.maximum(col - lo_c, 0) + 1).astype(F32)
            for rr in range(2 - lo_r, 2 + n_rows + hi_r):
                start = POOL_MARGIN + rr * GRID_W
                acc = jnp.zeros((GRID_W, gd), F32)
                for dc in range(-lo_c, hi_c + 1):
                    v = hbuf[start + dc:start + dc + GRID_W, cols]
                    ok = (col + dc >= 0) & (col + dc < GRID_W)
                    acc = acc + jnp.where(ok, v, 0.0)
                cpbuf[rr * GRID_W:(rr + 1) * GRID_W, :] = acc / cnt_c
            for r in range(n_rows):
                acc = jnp.zeros((GRID_W, gd), F32)
                cnt_r = jnp.zeros((GRID_W, 1), F32)
                for dr in range(-lo_r, hi_r + 1):
                    grow = row0 + r + dr
                    ok = (grow >= 0) & (grow < rows_per_batch)
                    v = cpbuf[(2 + r + dr) * GRID_W:(3 + r + dr) * GRID_W, :]
                    acc = acc + jnp.where(ok, v, 0.0)
                    cnt_r = cnt_r + jnp.where(ok, 1.0, 0.0)
                hcur = hbuf[base + r * GRID_W:base + (r + 1) * GRID_W, cols]
                dbuf[r * GRID_W:(r + 1) * GRID_W, :] = (acc / cnt_r - hcur).astype(BF16)
            finish(grp)


def _pool_layer(tl, xall, modtab, layer, norm_g, pool_w, pool_b, pool_scale):
    d = tl.d
    gd = d // POOL_GROUPS
    n_prev_blocks = TM // POOL_PREV
    n_next_blocks = TM // POOL_NEXT
    last_next = tl.n_tok // POOL_NEXT - 1
    kern = functools.partial(_pool_kernel, ctx_len=tl.ctx_len, tiles_per_batch=tl.tiles_per_batch,
                             rows_per_batch=tl.seq // GRID_W)
    vec = pl.BlockSpec((1, d), lambda i: (0, 0))
    buf_rows = 2 * POOL_MARGIN + POOL_PREV + TM + POOL_NEXT
    return pl.pallas_call(
        kern,
        grid=(tl.n_tiles,),
        in_specs=[
            pl.BlockSpec((POOL_PREV, d), lambda i: (jnp.maximum(i * n_prev_blocks - 1, 0), 0)),
            pl.BlockSpec((TM, d), lambda i: (i, 0)),
            pl.BlockSpec((POOL_NEXT, d), lambda i: (jnp.minimum((i + 1) * n_next_blocks, last_next), 0)),
            vec,
            tl.mod_spec(layer, 0), tl.mod_spec(layer, 1), tl.mod_spec(layer, 2),
            pl.BlockSpec((POOL_GROUPS, gd, gd), lambda i: (0, 0, 0)),
            vec, vec,
        ],
        out_specs=pl.BlockSpec((TM, d), lambda i: (i, 0)),
        out_shape=jax.ShapeDtypeStruct((tl.n_tok, d), F32),
        scratch_shapes=[
            pltpu.VMEM((buf_rows, d), F32),
            pltpu.VMEM((POOL_PREV + TM + POOL_NEXT, gd), F32),
            pltpu.VMEM((TM, gd), BF16),
        ],
        compiler_params=_cparams(("arbitrary",)),
        name="pool_layer",
    )(xall, xall, xall, norm_g.reshape(1, d), modtab, modtab, modtab, pool_w,
      pool_b.reshape(1, d), pool_scale.reshape(1, d))


LANES = 128


def _load_rows(ref, n_rows):
    nblk = ref.shape[0] // n_rows
    return jnp.concatenate([ref[pl.ds(c, n_rows, stride=nblk), :] for c in range(nblk)], axis=1)


def _store_rows(ref, value):
    n_rows = value.shape[0]
    nblk = value.shape[1] // LANES
    for c in range(nblk):
        ref[pl.ds(c, n_rows, stride=nblk), :] = value[:, c * LANES:(c + 1) * LANES]


def _router_kernel(x_ref, g_ref, sh_ref, sc_ref, rwt_ref, rb_ref, h_ref, r_ref):
    h = _norm_mod(x_ref[...], g_ref[...], sh_ref[...], sc_ref[...])
    _store_rows(h_ref, h)
    logits = lax.dot_general(rwt_ref[...], h, (((1,), (1,)), ((), ())),
                             precision=lax.Precision.HIGHEST, preferred_element_type=F32)
    scores = 1.0 / (1.0 + jnp.exp(-logits))
    sel = scores + rb_ref[...]
    tm = sel.shape[1]
    srow = [scores[e:e + 1, :] for e in range(N_EXPERTS)]
    vrow = [sel[e:e + 1, :] for e in range(N_EXPERTS)]
    best_g = jnp.zeros((1, tm), jnp.int32)
    best_s = None
    for grp in range(N_EXPERT_GROUPS):
        v = vrow[grp * EXPERTS_PER_GROUP:(grp + 1) * EXPERTS_PER_GROUP]
        gs = None
        for a in range(EXPERTS_PER_GROUP):
            for b in range(a + 1, EXPERTS_PER_GROUP):
                p = v[a] + v[b]
                gs = p if gs is None else jnp.maximum(gs, p)
        if best_s is None:
            best_s = gs
        else:
            upd = gs > best_s
            best_s = jnp.where(upd, gs, best_s)
            best_g = jnp.where(upd, grp, best_g)
    neg = jnp.full((1, tm), -jnp.inf, F32)
    masked = [jnp.where(best_g == (e // EXPERTS_PER_GROUP), vrow[e], neg) for e in range(N_EXPERTS)]
    m1, i1, s1 = masked[0], jnp.zeros((1, tm), jnp.int32), srow[0]
    for e in range(1, N_EXPERTS):
        upd = masked[e] > m1
        m1 = jnp.where(upd, masked[e], m1)
        i1 = jnp.where(upd, e, i1)
        s1 = jnp.where(upd, srow[e], s1)
    m2, i2, s2 = neg, jnp.full((1, tm), -1, jnp.int32), jnp.zeros((1, tm), F32)
    for e in range(N_EXPERTS):
        cand = i1 != e
        upd = cand & ((masked[e] > m2) | (i2 < 0))
        m2 = jnp.where(upd, masked[e], m2)
        i2 = jnp.where(upd, e, i2)
        s2 = jnp.where(upd, srow[e], s2)
    tot = s1 + s2
    w1, w2 = s1 / tot, s2 / tot
    first_lo = i1 < i2
    e_lo = jnp.where(first_lo, i1, i2).astype(F32)
    e_hi = jnp.where(first_lo, i2, i1).astype(F32)
    w_lo = jnp.where(first_lo, w1, w2)
    w_hi = jnp.where(first_lo, w2, w1)
    zero = jnp.zeros((1, tm), F32)
    r_ref[...] = jnp.concatenate([e_lo, e_hi, w_lo, w_hi, zero, zero, zero, zero], axis=0)


def _router(tl, xall, modtab, layer, norm_g, router_w, router_bias):
    d = tl.d
    vec = pl.BlockSpec((1, d), lambda i: (0, 0))
    return pl.pallas_call(
        _router_kernel,
        grid=(tl.n_tiles,),
        in_specs=[
            pl.BlockSpec((TM, d), lambda i: (i, 0)),
            vec, tl.mod_spec(layer, 3), tl.mod_spec(layer, 4),
            pl.BlockSpec((N_EXPERTS, d), lambda i: (0, 0)),
            pl.BlockSpec((N_EXPERTS, 1), lambda i: (0, 0)),
        ],
        out_specs=[
            pl.BlockSpec((TM * (d // LANES), LANES), lambda i: (i, 0)),
            pl.BlockSpec((8, TM), lambda i: (0, i)),
        ],
        out_shape=[
            jax.ShapeDtypeStruct((tl.n_tok * (d // LANES), LANES), F32),
            jax.ShapeDtypeStruct((8, tl.n_tok), F32),
        ],
        compiler_params=_cparams(("arbitrary",)),
        name="moe_router",
    )(xall, norm_g.reshape(1, d), modtab, modtab, router_w.T, router_bias.reshape(N_EXPERTS, 1))


def _moe_tiles(n_tok):
    return -(-(TOP_K * n_tok + N_EXPERTS * (MOE_ROWS - 1)) // MOE_ROWS) + 1


def _dispatch(route, n_tok):
    e = jnp.stack([route[0], route[1]], axis=1).astype(jnp.int32).reshape(TOP_K * n_tok)
    onehot = (e[:, None] == jnp.arange(N_EXPERTS, dtype=jnp.int32)[None, :]).astype(jnp.int32)
    csum = jnp.cumsum(onehot, axis=0)
    rank = jnp.sum(csum * onehot, axis=1) - 1
    cnt = csum[-1]
    ntile = (cnt + MOE_ROWS - 1) // MOE_ROWS
    tile_end = jnp.cumsum(ntile)
    tile_start = tile_end - ntile
    pos = jnp.sum(onehot * tile_start[None, :], axis=1) * MOE_ROWS + rank
    n_tiles = _moe_tiles(n_tok)
    n_used = tile_end[-1]
    tiles = jnp.minimum(jnp.arange(n_tiles, dtype=jnp.int32), n_used - 1)
    tile_expert = jnp.sum((tiles[:, None] >= tile_end[None, :]).astype(jnp.int32), axis=1)
    tile_expert = tile_expert.astype(jnp.int32)
    next_expert = tile_expert[jnp.minimum(tile_end[tile_expert], n_tiles - 1)]
    tok = jnp.arange(TOP_K * n_tok, dtype=jnp.int32) // TOP_K
    slot_tok = jnp.zeros((n_tiles * MOE_ROWS,), jnp.int32).at[pos].set(tok)
    pos = pos.astype(jnp.int32).reshape(n_tok, TOP_K)
    return pos[:, 0], pos[:, 1], slot_tok, tile_expert, next_expert, n_used.reshape(1).astype(jnp.int32)


def _row_copy(src_hbm, row, dst, r, sem, nblk):
    src = src_hbm.at[pl.ds(pl.multiple_of(row * nblk, nblk), nblk), :]
    return pltpu.make_async_copy(src, dst.at[pl.ds(r * nblk, nblk), :], sem)


def _wait_rows(src_hbm, dst, sem):
    pltpu.make_async_copy(src_hbm.at[pl.ds(0, dst.shape[0]), :], dst, sem).wait()


def _ffn_kernel(te_ref, nxt_ref, nused_ref, stok_ref, h_hbm, wg_hbm, wu_hbm, wd_hbm, o_ref,
                xbuf, wgs, wus, wds, wgb, wub, wdb, sem, wsem, *, layer):
    t = pl.program_id(0)
    slot = t % 2
    n_used = nused_ref[0]
    nblk = wgb.shape[0] // LANES

    def weight_copies(e):
        return (pltpu.make_async_copy(wg_hbm.at[layer, e], wgs, wsem.at[0]),
                pltpu.make_async_copy(wu_hbm.at[layer, e], wus, wsem.at[1]),
                pltpu.make_async_copy(wd_hbm.at[layer, e], wds, wsem.at[2]))

    @pl.when(t == 0)
    def _prime():
        for cp in weight_copies(te_ref[0]):
            cp.start()
        for r in range(MOE_ROWS):
            _row_copy(h_hbm, stok_ref[r], xbuf.at[0], r, sem.at[0], nblk).start()

    @pl.when((t < n_used) & ((t == 0) | (te_ref[t] != te_ref[jnp.maximum(t - 1, 0)])))
    def _new_expert():
        for cp in weight_copies(te_ref[t]):
            cp.wait()
        wgb[...] = wgs[...].astype(BF16)
        wub[...] = wus[...].astype(BF16)
        wdb[...] = wds[...].astype(BF16)

        @pl.when(nxt_ref[t] != te_ref[t])
        def _stage_next():
            for cp in weight_copies(nxt_ref[t]):
                cp.start()

    @pl.when(t <= n_used)
    def _wait():
        _wait_rows(h_hbm, xbuf.at[slot], sem.at[slot])

    @pl.when(t < n_used)
    def _compute():
        for r in range(MOE_ROWS):
            _row_copy(h_hbm, stok_ref[(t + 1) * MOE_ROWS + r], xbuf.at[1 - slot], r, sem.at[1 - slot],
                      nblk).start()
        xb = _load_rows(xbuf.at[slot], MOE_ROWS).astype(BF16)
        hg = jnp.dot(xb, wgb[...], preferred_element_type=F32)
        hu = jnp.dot(xb, wub[...], preferred_element_type=F32)
        act = (_silu(hg) * hu).astype(BF16)
        _store_rows(o_ref, jnp.dot(act, wdb[...], preferred_element_type=F32))

    @pl.when(t >= n_used)
    def _idle():
        o_ref[...] = jnp.zeros_like(o_ref)


def _moe_ffn(n_tok, h2, slot_tok, tile_expert, next_expert, n_used, layer, wg, wu, wd):
    _, _, d, f = wg.shape
    nblk = d // LANES
    n_tiles = _moe_tiles(n_tok)
    any_spec = pl.BlockSpec(memory_space=pl.ANY)
    grid_spec = pltpu.PrefetchScalarGridSpec(
        num_scalar_prefetch=4,
        grid=(n_tiles,),
        in_specs=[any_spec, any_spec, any_spec, any_spec],
        out_specs=pl.BlockSpec((MOE_ROWS * nblk, LANES), lambda t, te, nx, nu, st: (t, 0)),
        scratch_shapes=[
            pltpu.VMEM((2, MOE_ROWS * nblk, LANES), F32),
            pltpu.VMEM((d, f), F32), pltpu.VMEM((d, f), F32), pltpu.VMEM((f, d), F32),
            pltpu.VMEM((d, f), BF16), pltpu.VMEM((d, f), BF16), pltpu.VMEM((f, d), BF16),
            pltpu.SemaphoreType.DMA((2,)),
            pltpu.SemaphoreType.DMA((3,)),
        ],
    )
    return pl.pallas_call(
        functools.partial(_ffn_kernel, layer=layer),
        grid_spec=grid_spec,
        out_shape=jax.ShapeDtypeStruct((n_tiles * MOE_ROWS * nblk, LANES), F32),
        compiler_params=_cparams(("arbitrary",)),
        name="moe_ffn",
    )(tile_expert, next_expert, n_used, slot_tok, h2, wg, wu, wd)


COMBINE_TM = 256


def _combine_kernel(p1_ref, p2_ref, x_ref, gate_ref, w_ref, fg_ref, y_hbm, o_ref, ybuf, sem, *, final):
    i = pl.program_id(0)
    n_i = pl.num_programs(0)
    slot = i % 2
    tm = COMBINE_TM
    nblk = x_ref.shape[1] // LANES

    def gather(tile, dst, dsem):
        def body(r, carry):
            _row_copy(y_hbm, p1_ref[tile * tm + r], dst.at[0], r, dsem, nblk).start()
            _row_copy(y_hbm, p2_ref[tile * tm + r], dst.at[1], r, dsem, nblk).start()
            return carry
        lax.fori_loop(0, tm, body, 0, unroll=8)

    @pl.when(i == 0)
    def _prime():
        gather(0, ybuf.at[0], sem.at[0])

    @pl.when(i + 1 < n_i)
    def _prefetch():
        gather(i + 1, ybuf.at[1 - slot], sem.at[1 - slot])

    _wait_rows(y_hbm, ybuf.at[slot, 0], sem.at[slot])
    _wait_rows(y_hbm, ybuf.at[slot, 1], sem.at[slot])
    def rows8(k, r0):
        base = r0 * nblk
        return jnp.concatenate([ybuf[slot, k, pl.ds(base + c, 8, stride=nblk), :] for c in range(nblk)], axis=1)

    def body(g, carry):
        r0 = pl.multiple_of(g * 8, 8)
        w = w_ref[pl.ds(r0, 8), :]
        y = w[:, 0:1] * rows8(0, r0) + w[:, 1:2] * rows8(1, r0)
        x = x_ref[pl.ds(r0, 8), :] + gate_ref[...] * y
        if final:
            ms = jnp.mean(x * x, axis=-1, keepdims=True)
            x = x * lax.rsqrt(ms + NORM_EPS) * fg_ref[...]
        o_ref[pl.ds(r0, 8), :] = x
        return carry
    lax.fori_loop(0, tm // 8, body, 0, unroll=4)


def _moe_combine(tl, xall, y_sorted, pos1, pos2, gates, modtab, layer, final_g, final):
    d = tl.d
    tm = COMBINE_TM
    per_tile = TM // tm
    grid_spec = pltpu.PrefetchScalarGridSpec(
        num_scalar_prefetch=2,
        grid=(tl.n_tok // tm,),
        in_specs=[
            pl.BlockSpec((tm, d), lambda i, p1, p2: (i, 0)),
            tl.mod_spec(layer, 5, tile_of=lambda i, p1, p2: i // per_tile),
            pl.BlockSpec((tm, TOP_K), lambda i, p1, p2: (i, 0)),
            pl.BlockSpec((1, d), lambda i, p1, p2: (0, 0)),
            pl.BlockSpec(memory_space=pl.ANY),
        ],
        out_specs=pl.BlockSpec((tm, d), lambda i, p1, p2: (i, 0)),
        scratch_shapes=[
            pltpu.VMEM((2, TOP_K, tm * (d // LANES), LANES), F32),
            pltpu.SemaphoreType.DMA((2,)),
        ],
    )
    return pl.pallas_call(
        functools.partial(_combine_kernel, final=final),
        grid_spec=grid_spec,
        out_shape=jax.ShapeDtypeStruct((tl.n_tok, d), F32),
        compiler_params=_cparams(("arbitrary",)),
        name="moe_combine",
    )(pos1, pos2, xall, modtab, gates, final_g.reshape(1, d), y_sorted)


def _moe_layer(tl, xall, modtab, layer, norm_g, router_w, router_bias, wg, wu, wd, final_g, final):
    h2, route = _router(tl, xall, modtab, layer, norm_g, router_w, router_bias)
    pos1, pos2, slot_tok, tile_expert, next_expert, n_used = _dispatch(route, tl.n_tok)
    y_sorted = _moe_ffn(tl.n_tok, h2, slot_tok, tile_expert, next_expert, n_used, layer, wg, wu, wd)
    gates = route[2:4].T
    return _moe_combine(tl, xall, y_sorted, pos1, pos2, gates, modtab, layer, final_g, final)


INPROJ_TN = 1024
OUTPROJ_TN = 512
CONV_HALO = 16


def _normmod_kernel(x_ref, g_ref, sh_ref, sc_ref, wdt_ref, h_ref, dt_ref):
    h = _norm_mod(x_ref[...], g_ref[...], sh_ref[...], sc_ref[...]).astype(BF16)
    h_ref[...] = h
    dt_ref[...] = jnp.dot(h, wdt_ref[...], preferred_element_type=F32)


def _normmod(tl, xall, modtab, layer, norm_g, w_dt):
    d = tl.d
    n_dt = w_dt.shape[1]
    return pl.pallas_call(
        _normmod_kernel,
        grid=(tl.n_tiles,),
        in_specs=[pl.BlockSpec((TM, d), lambda i: (i, 0)), pl.BlockSpec((1, d), lambda i: (0, 0)),
                  tl.mod_spec(layer, 0), tl.mod_spec(layer, 1), pl.BlockSpec((d, n_dt), lambda i: (0, 0))],
        out_specs=[pl.BlockSpec((TM, d), lambda i: (i, 0)), pl.BlockSpec((TM, n_dt), lambda i: (i, 0))],
        out_shape=[jax.ShapeDtypeStruct((tl.n_tok, d), BF16), jax.ShapeDtypeStruct((tl.n_tok, n_dt), F32)],
        compiler_params=_cparams(("arbitrary",)),
        name="ssd_normmod",
    )(xall, norm_g.reshape(1, d), modtab, modtab, w_dt)


def _inproj_kernel(h_ref, w_ref, wt_ref, p_ref, wb, *, n_main_blocks):
    n = pl.program_id(0)

    @pl.when((pl.program_id(1) == 0) & (n < n_main_blocks))
    def _main_columns():
        wb[...] = w_ref[...].astype(BF16)

    @pl.when((pl.program_id(1) == 0) & (n >= n_main_blocks))
    def _tail_columns():
        wb[...] = wt_ref[...]

    p_ref[...] = jnp.dot(h_ref[...], wb[...], preferred_element_type=F32).astype(BF16)


def _inproj(tl, h, in_w_all, j, n_main, w_tail):
    d = tl.d
    tn = INPROJ_TN
    n_main_blocks = n_main // tn
    n_tail_blocks = w_tail.shape[1] // tn
    return pl.pallas_call(
        functools.partial(_inproj_kernel, n_main_blocks=n_main_blocks),
        grid=(n_main_blocks + n_tail_blocks, tl.n_tiles),
        in_specs=[
            pl.BlockSpec((TM, d), lambda n, i: (i, 0)),
            pl.BlockSpec((None, d, tn), lambda n, i: (j, 0, jnp.minimum(n, n_main_blocks - 1))),
            pl.BlockSpec((d, tn), lambda n, i: (0, jnp.maximum(n - n_main_blocks, 0))),
        ],
        out_specs=pl.BlockSpec((TM, tn), lambda n, i: (i, n)),
        out_shape=jax.ShapeDtypeStruct((tl.n_tok, n_main + w_tail.shape[1]), BF16),
        scratch_shapes=[pltpu.VMEM((d, tn), BF16)],
        compiler_params=_cparams(("arbitrary", "arbitrary")),
        name="ssd_inproj",
    )(h, in_w_all, w_tail)


def _scan_kernel(xs_ref, xsp_ref, xsn_ref, bc_ref, bcp_ref, bcn_ref, dt_ref, cw_ref, cb_ref, hp_ref,
                 dsk_ref, y_ref, ubuf, xc, xcb, bm, cm, cmf, bmt, acst, wt, state, *, ncc, ncl):
    q = SSD_CHUNK
    di = xs_ref.shape[1]
    n_grp = SSD_GROUPS
    blocks_per_group = di // n_grp // 128
    d = pl.program_id(0)
    c = pl.program_id(2)
    is_ctx = c < ncc
    fwd = d == 0
    tch = jnp.where(is_ctx, jnp.where(fwd, c, ncc - 1 - c), jnp.where(fwd, c - ncc, ncl - 1 - (c - ncc)))
    first = tch == 0
    last = tch == jnp.where(is_ctx, ncc - 1, ncl - 1)

    @pl.when(c == 0)
    def _reset():
        state[...] = jnp.zeros_like(state)

    hl = CONV_HALO
    ubuf[0:hl, 0:di] = jnp.where(first, 0.0, xsp_ref[...].astype(F32))
    ubuf[hl:hl + q, 0:di] = xs_ref[...].astype(F32)
    ubuf[hl + q:hl + q + hl, 0:di] = jnp.where(last, 0.0, xsn_ref[...].astype(F32))
    ubuf[0:hl, di:] = jnp.where(first, 0.0, bcp_ref[...].astype(F32))
    ubuf[hl:hl + q, di:] = bc_ref[...].astype(F32)
    ubuf[hl + q:hl + q + hl, di:] = jnp.where(last, 0.0, bcn_ref[...].astype(F32))

    def conv_block(col):
        acc = cb_ref[:, pl.ds(col, 128)]
        for k in range(SSD_CONV):
            off = hl + k - SSD_CONV // 2
            acc = acc + cw_ref[k:k + 1, pl.ds(col, 128)] * ubuf[off:off + q, pl.ds(col, 128)]
        return _silu(acc)

    hp = hp_ref[...]
    a_coef = -jnp.exp(hp[0:1])
    raw = dt_ref[...] + hp[1:2]
    dt = jnp.maximum(raw, 0.0) + jnp.log(1.0 + jnp.exp(-jnp.abs(raw)))
    a = dt * a_coef
    li = lax.broadcasted_iota(jnp.int32, (q, q), 0)
    si = lax.broadcasted_iota(jnp.int32, (q, q), 1)
    tri = jnp.where(fwd, li - si, si - li) >= 0
    trib = jnp.where(tri, 1.0, 0.0).astype(BF16)
    a_hi = a.astype(BF16)
    r1 = a - a_hi.astype(F32)
    a_mid = r1.astype(BF16)
    a_lo = (r1 - a_mid.astype(F32)).astype(BF16)
    acs = (jnp.dot(trib, a_hi, preferred_element_type=F32) + jnp.dot(trib, a_mid, preferred_element_type=F32)
           + jnp.dot(trib, a_lo, preferred_element_type=F32))
    tot = jnp.sum(a, axis=0, keepdims=True)
    etot = jnp.exp(tot)
    acst[...] = (acs - jnp.log(dt)).T
    wt[...] = (dt * jnp.exp(tot - acs)).T

    def xs_body(j, carry):
        col = pl.multiple_of(j * 128, 128)
        v = conv_block(col)
        xc[:, pl.ds(col, 128)] = v
        xcb[:, pl.ds(col, 128)] = v.astype(BF16)
        return carry
    lax.fori_loop(0, di // 128, xs_body, 0, unroll=2)

    def b_body(j, carry):
        col = pl.multiple_of(j * 128, 128)
        v = conv_block(di + col)
        bm[:, pl.ds(col, 128)] = v.astype(BF16)
        bmt[j] = v.T
        return carry
    lax.fori_loop(0, n_grp, b_body, 0)

    def c_body(j, carry):
        col = pl.multiple_of(j * 128, 128)
        v = conv_block(di + n_grp * SSD_STATE + col)
        cmf[:, pl.ds(col, 128)] = v
        cm[:, pl.ds(col, 128)] = v.astype(BF16)
        return carry
    lax.fori_loop(0, n_grp, c_body, 0, unroll=2)

    lane = lax.broadcasted_iota(jnp.int32, (q, 128), 1)
    left = lane < SSD_HEAD_DIM
    left_row = left[0:1, :]
    for pb in range(di // 128):
        grp = pb // blocks_per_group
        gl = slice(grp * SSD_STATE, (grp + 1) * SSD_STATE)
        pc = slice(pb * 128, (pb + 1) * 128)
        sc = slice((pb % blocks_per_group) * 128, (pb % blocks_per_group + 1) * 128)
        if pb % blocks_per_group == 0:
            cbm = lax.dot_general(cm[:, gl], bm[:, gl], (((1,), (1,)), ((), ())), preferred_element_type=F32)
        h0, h1 = 2 * pb, 2 * pb + 1
        ms, cs, bts = [], [], []
        for h in (h0, h1):
            colb = jnp.broadcast_to(acs[:, h:h + 1], (q, q))
            arg = jnp.where(tri, colb - acst[h:h + 1, :], -jnp.inf)
            ms.append((cbm * jnp.exp(arg)).astype(BF16))
            cs.append((cmf[:, gl] * jnp.exp(colb)).astype(BF16))
            bts.append((bmt[grp] * wt[h:h + 1, :]).astype(BF16))
        xp = xcb[:, pc]
        zero = jnp.zeros_like(xp)
        xe, xo = jnp.where(left, xp, zero), jnp.where(left, zero, xp)
        s_old = state[grp, :, sc]
        sb = s_old.astype(BF16)
        se, so = jnp.where(left, sb, zero), jnp.where(left, zero, sb)
        lhs = jnp.concatenate(ms + cs, axis=1)
        rhs = jnp.concatenate([xe, xo, se, so], axis=0)
        y = jnp.dot(lhs, rhs, preferred_element_type=F32) + xc[:, pc] * dsk_ref[:, pc]
        y_ref[:, pc] = y.astype(BF16)
        et = jnp.where(left_row, etot[:, h0:h0 + 1], etot[:, h1:h1 + 1])
        upd = jnp.dot(jnp.concatenate(bts, axis=1), jnp.concatenate([xe, xo], axis=0), preferred_element_type=F32)
        state[grp, :, sc] = s_old * et + upd


def _ssd_scan(tl, proj, dt2, conv_w, conv_b, head_params, dskip_lanes):
    n_tok = tl.n_tok
    q = SSD_CHUNK
    n_heads = dt2.shape[-1]
    di = n_heads * SSD_HEAD_DIM
    gn = SSD_GROUPS * SSD_STATE
    conv_ch = di + 2 * gn
    assert (2 * di) % (2 * gn) == 0 and (di // SSD_GROUPS) % 128 == 0
    ncc, ncl = tl.ctx_len // q, tl.seq // q
    bc_block0 = 2 * di // (2 * gn)
    hpc = q // CONV_HALO
    last_halo = n_tok // CONV_HALO - 1

    def blk(d, b, c):
        fwd = d == 0
        is_ctx = c < ncc
        t_ctx = jnp.where(fwd, c, ncc - 1 - c)
        t_lat = jnp.where(fwd, c - ncc, ncl - 1 - (c - ncc))
        return jnp.where(is_ctx, b * ncc + t_ctx, tl.bsz * ncc + b * ncl + t_lat)

    prev = lambda d, b, c: jnp.maximum(blk(d, b, c) * hpc - 1, 0)
    nxt = lambda d, b, c: jnp.minimum((blk(d, b, c) + 1) * hpc, last_halo)
    kern = functools.partial(_scan_kernel, ncc=ncc, ncl=ncl)
    return pl.pallas_call(
        kern,
        grid=(2, tl.bsz, ncc + ncl),
        in_specs=[
            pl.BlockSpec((q, di), lambda d, b, c: (blk(d, b, c), 1)),
            pl.BlockSpec((CONV_HALO, di), lambda d, b, c: (prev(d, b, c), 1)),
            pl.BlockSpec((CONV_HALO, di), lambda d, b, c: (nxt(d, b, c), 1)),
            pl.BlockSpec((q, 2 * gn), lambda d, b, c: (blk(d, b, c), bc_block0 + d)),
            pl.BlockSpec((CONV_HALO, 2 * gn), lambda d, b, c: (prev(d, b, c), bc_block0 + d)),
            pl.BlockSpec((CONV_HALO, 2 * gn), lambda d, b, c: (nxt(d, b, c), bc_block0 + d)),
            pl.BlockSpec((None, q, n_heads), lambda d, b, c: (d, blk(d, b, c), 0)),
            pl.BlockSpec((None, SSD_CONV, conv_ch), lambda d, b, c: (d, 0, 0)),
            pl.BlockSpec((None, 1, conv_ch), lambda d, b, c: (d, 0, 0)),
            pl.BlockSpec((None, 8, n_heads), lambda d, b, c: (d, 0, 0)),
            pl.BlockSpec((None, 1, di), lambda d, b, c: (d, 0, 0)),
        ],
        out_specs=pl.BlockSpec((None, q, di), lambda d, b, c: (d, blk(d, b, c), 0)),
        out_shape=jax.ShapeDtypeStruct((2, n_tok, di), BF16),
        scratch_shapes=[
            pltpu.VMEM((q + 2 * CONV_HALO, conv_ch), F32),
            pltpu.VMEM((q, di), F32),
            pltpu.VMEM((q, di), BF16),
            pltpu.VMEM((q, gn), BF16),
            pltpu.VMEM((q, gn), BF16),
            pltpu.VMEM((q, gn), F32),
            pltpu.VMEM((SSD_GROUPS, SSD_STATE, q), F32),
            pltpu.VMEM((n_heads, q), F32),
            pltpu.VMEM((n_heads, q), F32),
            pltpu.VMEM((SSD_GROUPS, SSD_STATE, di // SSD_GROUPS), F32),
        ],
        compiler_params=_cparams(("arbitrary", "arbitrary", "arbitrary")),
        name="ssd_scan",
    )(proj, proj, proj, proj, proj, proj, dt2, conv_w, conv_b.reshape(2, 1, conv_ch), head_params, dskip_lanes)


def _outproj_kernel(y0_ref, y1_ref, z_ref, ng_ref, w_ref, x_ref, gate_ref, o_ref, ybuf):
    @pl.when(pl.program_id(1) == 0)
    def _prologue():
        gw = ybuf.shape[1] // SSD_GROUPS
        for grp in range(SSD_GROUPS):
            cols = slice(grp * gw, (grp + 1) * gw)
            y = (y0_ref[:, cols].astype(F32) + y1_ref[:, cols].astype(F32)) * _silu(z_ref[:, cols].astype(F32))
            ms = jnp.mean(y * y, axis=-1, keepdims=True)
            ybuf[:, cols] = (y * lax.rsqrt(ms + NORM_EPS) * ng_ref[:, cols]).astype(BF16)

    o_ref[...] = x_ref[...] + gate_ref[...] * jnp.dot(ybuf[...], w_ref[...], preferred_element_type=F32)


def _outproj(tl, xall, modtab, layer, y2, proj, norm_g, out_w):
    d = tl.d
    di = out_w.shape[0]
    tn = OUTPROJ_TN
    gate_spec = pl.BlockSpec(
        (None, 1, tn), lambda i, j: ((layer * COND_ROWS + tl.seg(i)) * N_MOD + 2, 0, j))
    return pl.pallas_call(
        _outproj_kernel,
        grid=(tl.n_tiles, d // tn),
        in_specs=[
            pl.BlockSpec((None, TM, di), lambda i, j: (0, i, 0)),
            pl.BlockSpec((None, TM, di), lambda i, j: (1, i, 0)),
            pl.BlockSpec((TM, di), lambda i, j: (i, 0)),
            pl.BlockSpec((1, di), lambda i, j: (0, 0)),
            pl.BlockSpec((di, tn), lambda i, j: (0, j)),
            pl.BlockSpec((TM, tn), lambda i, j: (i, j)),
            gate_spec,
        ],
        out_specs=pl.BlockSpec((TM, tn), lambda i, j: (i, j)),
        out_shape=jax.ShapeDtypeStruct((tl.n_tok, d), F32),
        scratch_shapes=[pltpu.VMEM((TM, di), BF16)],
        compiler_params=_cparams(("arbitrary", "arbitrary")),
        name="ssd_outproj",
    )(y2, y2, proj, norm_g.reshape(1, di), out_w, xall, modtab)


def _ssd_layer(tl, xall, modtab, layer, norm_g, in_w_all, j, conv_w, conv_b, a_log, dt_bias, d_skip, ssd_norm_g,
               out_w):
    n_heads = a_log.shape[-1]
    di = n_heads * SSD_HEAD_DIM
    gn = SSD_GROUPS * SSD_STATE
    dir_cols = 2 * gn + n_heads
    base0 = 2 * di
    base1 = base0 + dir_cols
    n_main = base0 + 2 * gn
    in_w = in_w_all[j]
    w_tail = in_w[:, base1:base1 + 2 * gn].astype(BF16)
    w_dt = jnp.concatenate(
        [in_w[:, base0 + 2 * gn:base0 + dir_cols], in_w[:, base1 + 2 * gn:base1 + dir_cols]], axis=1).astype(BF16)
    h, dt_raw = _normmod(tl, xall, modtab, layer, norm_g, w_dt)
    proj = _inproj(tl, h, in_w_all, j, n_main, w_tail)
    dt2 = dt_raw.reshape(tl.n_tok, 2, n_heads).transpose(1, 0, 2)
    head_params = jnp.zeros((2, 8, n_heads), F32).at[:, 0].set(a_log).at[:, 1].set(dt_bias)
    dskip_lanes = jnp.repeat(d_skip, SSD_HEAD_DIM, axis=-1).reshape(2, 1, di)
    y2 = _ssd_scan(tl, proj, dt2, conv_w, conv_b, head_params, dskip_lanes)
    return _outproj(tl, xall, modtab, layer, y2, proj, ssd_norm_g, out_w.astype(BF16))


def kernel(x, c, ctx, c_ctx, mod_w, mod_b, norm_mix_g, norm_ffn_g, pool_w, pool_b, pool_scale, ssd_in_w,
           ssd_conv_w, ssd_conv_b, ssd_a_log, ssd_dt_bias, ssd_d, ssd_norm_g, ssd_out_w, router_w, router_bias,
           moe_w_gate, moe_w_up, moe_w_down, final_norm_g):
    bsz, seq, d = x.shape
    ctx_len = ctx.shape[1]
    depth = mod_w.shape[0]
    assert bsz + 1 <= COND_ROWS
    tl = _Tiles(bsz, ctx_len, seq, d)
    cond = jnp.concatenate([c_ctx[None, :], c, jnp.zeros((COND_ROWS - 1 - bsz, d), F32)], axis=0)
    modtab = _adaln_table(cond, mod_w, mod_b)
    xall = jnp.concatenate([ctx.reshape(bsz * ctx_len, d), x.reshape(bsz * seq, d)], axis=0)
    for i in range(depth):
        j = i // 2
        if i % 2 == 0:
            xall = _pool_layer(tl, xall, modtab, i, norm_mix_g[i], pool_w[j], pool_b[j], pool_scale[j])
        else:
            xall = _ssd_layer(tl, xall, modtab, i, norm_mix_g[i], ssd_in_w, j, ssd_conv_w[j], ssd_conv_b[j],
                              ssd_a_log[j], ssd_dt_bias[j], ssd_d[j], ssd_norm_g[j], ssd_out_w[j])
        xall = _moe_layer(tl, xall, modtab, i, norm_ffn_g[i], router_w, router_bias,
                          moe_w_gate, moe_w_up, moe_w_down, final_norm_g, i == depth - 1)
    return xall[tl.n_ctx:].reshape(bsz, seq, d)
```

```python
import functools

import jax
import jax.numpy as jnp
from jax import lax
from jax.experimental import pallas as pl
from jax.experimental.pallas import tpu as pltpu

F32 = jnp.float32
BF16 = jnp.bfloat16

GRID_W = 64
NORM_EPS = 1e-6
N_MOD = 6
POOL_GROUPS = 4
POOL_WINDOWS_1D = (2, 4, 8, 16)
POOL_WINDOWS_2D = ((1, 2), (2, 2), (2, 4), (4, 4))
SSD_HEAD_DIM = 64
SSD_GROUPS = 8
SSD_STATE = 128
SSD_CONV = 4
SSD_CHUNK = 128
N_EXPERTS = 16
N_EXPERT_GROUPS = 4
EXPERTS_PER_GROUP = 4
TOP_K = 2

TM = 512
MOE_ROWS = 256
COND_ROWS = 8
VMEM_LIMIT = 56 * 1024 * 1024


def _silu(v):
    return v / (1.0 + jnp.exp(-v))


def _norm_mod(x, g, shift, scale):
    ms = jnp.mean(x * x, axis=-1, keepdims=True)
    y = x * lax.rsqrt(ms + NORM_EPS) * g
    return y * (1.0 + scale) + shift


def _cparams(sem):
    return pltpu.CompilerParams(dimension_semantics=sem, vmem_limit_bytes=VMEM_LIMIT)


def _adaln_kernel(cond_ref, w_ref, b_ref, o_ref):
    s = _silu(cond_ref[...]).astype(BF16)
    o_ref[...] = jnp.dot(s, w_ref[...].astype(BF16), preferred_element_type=F32) + b_ref[...]


def _adaln_table(cond, mod_w, mod_b):
    depth, d, nd = mod_w.shape
    tn = 1024 if nd % 1024 == 0 else nd
    out = pl.pallas_call(
        _adaln_kernel,
        grid=(depth, nd // tn),
        in_specs=[
            pl.BlockSpec((COND_ROWS, d), lambda l, j: (0, 0)),
            pl.BlockSpec((None, d, tn), lambda l, j: (l, 0, j)),
            pl.BlockSpec((None, 1, tn), lambda l, j: (l, 0, j)),
        ],
        out_specs=pl.BlockSpec((None, COND_ROWS, tn), lambda l, j: (l, 0, j)),
        out_shape=jax.ShapeDtypeStruct((depth, COND_ROWS, nd), F32),
        compiler_params=_cparams(("arbitrary", "arbitrary")),
        name="adaln",
    )(cond, mod_w, mod_b.reshape(depth, 1, nd))
    return out.reshape(depth * COND_ROWS * N_MOD, 1, d)


class _Tiles:
    def __init__(self, bsz, ctx_len, seq, d):
        assert bsz * ctx_len == TM and seq % TM == 0
        self.bsz, self.ctx_len, self.seq, self.d = bsz, ctx_len, seq, d
        self.n_ctx = bsz * ctx_len
        self.n_tok = self.n_ctx + bsz * seq
        self.n_tiles = self.n_tok // TM
        self.tiles_per_batch = seq // TM

    def seg(self, i):
        return jnp.where(i == 0, 0, 1 + (i - 1) // self.tiles_per_batch)

    def mod_spec(self, layer, k, tile_of=lambda *idx: idx[0]):
        def imap(*idx):
            return ((layer * COND_ROWS + self.seg(tile_of(*idx))) * N_MOD + k, 0, 0)
        return pl.BlockSpec((None, 1, self.d), imap)


POOL_MARGIN = 8
POOL_PREV = 2 * GRID_W
POOL_NEXT = GRID_W


def _pool_kernel(xp_ref, xc_ref, xn_ref, g_ref, sh_ref, sc_ref, gate_ref, w_ref, b_ref, ps_ref,
                 o_ref, hbuf, cpbuf, dbuf, *, ctx_len, tiles_per_batch, rows_per_batch):
    i = pl.program_id(0)
    gd = w_ref.shape[-1]
    g, sh, sc = g_ref[...], sh_ref[...], sc_ref[...]
    base = POOL_MARGIN + POOL_PREV
    zeros_m = jnp.zeros((POOL_MARGIN, hbuf.shape[1]), F32)
    hbuf[0:POOL_MARGIN, :] = zeros_m
    hbuf[base + TM + POOL_NEXT:base + TM + POOL_NEXT + POOL_MARGIN, :] = zeros_m
    hbuf[POOL_MARGIN:base, :] = _norm_mod(xp_ref[...], g, sh, sc)
    hbuf[base:base + TM, :] = _norm_mod(xc_ref[...], g, sh, sc)
    hbuf[base + TM:base + TM + POOL_NEXT, :] = _norm_mod(xn_ref[...], g, sh, sc)

    def finish(grp):
        cols = slice(grp * gd, (grp + 1) * gd)
        y = jnp.dot(dbuf[...], w_ref[grp].astype(BF16), preferred_element_type=F32)
        y = (y + b_ref[:, cols]) * ps_ref[:, cols]
        o_ref[:, cols] = xc_ref[:, cols] + gate_ref[:, cols] * y

    @pl.when(i == 0)
    def _ctx():
        ch = GRID_W
        for grp in range(POOL_GROUPS):
            cols = slice(grp * gd, (grp + 1) * gd)
            w = POOL_WINDOWS_1D[grp]
            lo, hi = w // 2, w - 1 - w // 2
            for k in range(TM // ch):
                pos = (k * ch) % ctx_len + lax.broadcasted_iota(jnp.int32, (ch, 1), 0)
                acc = jnp.zeros((ch, gd), F32)
                for dd in range(-lo, hi + 1):
                    v = hbuf[base + k * ch + dd:base + k * ch + dd + ch, cols]
                    ok = (pos + dd >= 0) & (pos + dd < ctx_len)
                    acc = acc + jnp.where(ok, v, 0.0)
                cnt = jnp.minimum(pos + hi, ctx_len - 1) - jnp.maximum(pos - lo, 0) + 1
                t = acc / cnt.astype(F32)
                hcur = hbuf[base + k * ch:base + (k + 1) * ch, cols]
                dbuf[k * ch:(k + 1) * ch, :] = (t - hcur).astype(BF16)
            finish(grp)

    @pl.when(i > 0)
    def _latent():
        tile_in_batch = (i - 1) % tiles_per_batch
        row0 = tile_in_batch * (TM // GRID_W)
        col = lax.broadcasted_iota(jnp.int32, (GRID_W, 1), 0)
        n_rows = TM // GRID_W
        for grp in range(POOL_GROUPS):
            cols = slice(grp * gd, (grp + 1) * gd)
            wr, wc = POOL_WINDOWS_2D[grp]
            lo_r, hi_r = wr // 2, wr - 1 - wr // 2
            lo_c, hi_c = wc // 2, wc - 1 - wc // 2
            cnt_c = (jnp.minimum(col + hi_c, GRID_W - 1) - jnp.maximum(col - lo_c, 0) + 1).astype(F32)
            for rr in range(2 - lo_r, 2 + n_rows + hi_r):
                start = POOL_MARGIN + rr * GRID_W
                acc = jnp.zeros((GRID_W, gd), F32)
                for dc in range(-lo_c, hi_c + 1):
                    v = hbuf[start + dc:start + dc + GRID_W, cols]
                    ok = (col + dc >= 0) & (col + dc < GRID_W)
                    acc = acc + jnp.where(ok, v, 0.0)
                cpbuf[rr * GRID_W:(rr + 1) * GRID_W, :] = acc / cnt_c
            for r in range(n_rows):
                acc = jnp.zeros((GRID_W, gd), F32)
                cnt_r = jnp.zeros((GRID_W, 1), F32)
                for dr in range(-lo_r, hi_r + 1):
                    grow = row0 + r + dr
                    ok = (grow >= 0) & (grow < rows_per_batch)
                    v = cpbuf[(2 + r + dr) * GRID_W:(3 + r + dr) * GRID_W, :]
                    acc = acc + jnp.where(ok, v, 0.0)
                    cnt_r = cnt_r + jnp.where(ok, 1.0, 0.0)
                hcur = hbuf[base + r * GRID_W:base + (r + 1) * GRID_W, cols]
                dbuf[r * GRID_W:(r + 1) * GRID_W, :] = (acc / cnt_r - hcur).astype(BF16)
            finish(grp)


def _pool_layer(tl, xall, modtab, layer, norm_g, pool_w, pool_b, pool_scale):
    d = tl.d
    gd = d // POOL_GROUPS
    n_prev_blocks = TM // POOL_PREV
    n_next_blocks = TM // POOL_NEXT
    last_next = tl.n_tok // POOL_NEXT - 1
    kern = functools.partial(_pool_kernel, ctx_len=tl.ctx_len, tiles_per_batch=tl.tiles_per_batch,
                             rows_per_batch=tl.seq // GRID_W)
    vec = pl.BlockSpec((1, d), lambda i: (0, 0))
    buf_rows = 2 * POOL_MARGIN + POOL_PREV + TM + POOL_NEXT
    return pl.pallas_call(
        kern,
        grid=(tl.n_tiles,),
        in_specs=[
            pl.BlockSpec((POOL_PREV, d), lambda i: (jnp.maximum(i * n_prev_blocks - 1, 0), 0)),
            pl.BlockSpec((TM, d), lambda i: (i, 0)),
            pl.BlockSpec((POOL_NEXT, d), lambda i: (jnp.minimum((i + 1) * n_next_blocks, last_next), 0)),
            vec,
            tl.mod_spec(layer, 0), tl.mod_spec(layer, 1), tl.mod_spec(layer, 2),
            pl.BlockSpec((POOL_GROUPS, gd, gd), lambda i: (0, 0, 0)),
            vec, vec,
        ],
        out_specs=pl.BlockSpec((TM, d), lambda i: (i, 0)),
        out_shape=jax.ShapeDtypeStruct((tl.n_tok, d), F32),
        scratch_shapes=[
            pltpu.VMEM((buf_rows, d), F32),
            pltpu.VMEM((POOL_PREV + TM + POOL_NEXT, gd), F32),
            pltpu.VMEM((TM, gd), BF16),
        ],
        compiler_params=_cparams(("arbitrary",)),
        name="pool_layer",
    )(xall, xall, xall, norm_g.reshape(1, d), modtab, modtab, modtab, pool_w,
      pool_b.reshape(1, d), pool_scale.reshape(1, d))


def _router_kernel(x_ref, g_ref, sh_ref, sc_ref, rwt_ref, rb_ref, h_ref, r_ref):
    h = _norm_mod(x_ref[...], g_ref[...], sh_ref[...], sc_ref[...])
    h_ref[...] = h
    logits = lax.dot_general(rwt_ref[...], h, (((1,), (1,)), ((), ())),
                             precision=lax.Precision.HIGHEST, preferred_element_type=F32)
    scores = 1.0 / (1.0 + jnp.exp(-logits))
    sel = scores + rb_ref[...]
    tm = sel.shape[1]
    srow = [scores[e:e + 1, :] for e in range(N_EXPERTS)]
    vrow = [sel[e:e + 1, :] for e in range(N_EXPERTS)]
    best_g = jnp.zeros((1, tm), jnp.int32)
    best_s = None
    for grp in range(N_EXPERT_GROUPS):
        v = vrow[grp * EXPERTS_PER_GROUP:(grp + 1) * EXPERTS_PER_GROUP]
        gs = None
        for a in range(EXPERTS_PER_GROUP):
            for b in range(a + 1, EXPERTS_PER_GROUP):
                p = v[a] + v[b]
                gs = p if gs is None else jnp.maximum(gs, p)
        if best_s is None:
            best_s = gs
        else:
            upd = gs > best_s
            best_s = jnp.where(upd, gs, best_s)
            best_g = jnp.where(upd, grp, best_g)
    neg = jnp.full((1, tm), -jnp.inf, F32)
    masked = [jnp.where(best_g == (e // EXPERTS_PER_GROUP), vrow[e], neg) for e in range(N_EXPERTS)]
    m1, i1, s1 = masked[0], jnp.zeros((1, tm), jnp.int32), srow[0]
    for e in range(1, N_EXPERTS):
        upd = masked[e] > m1
        m1 = jnp.where(upd, masked[e], m1)
        i1 = jnp.where(upd, e, i1)
        s1 = jnp.where(upd, srow[e], s1)
    m2, i2, s2 = neg, jnp.full((1, tm), -1, jnp.int32), jnp.zeros((1, tm), F32)
    for e in range(N_EXPERTS):
        cand = i1 != e
        upd = cand & ((masked[e] > m2) | (i2 < 0))
        m2 = jnp.where(upd, masked[e], m2)
        i2 = jnp.where(upd, e, i2)
        s2 = jnp.where(upd, srow[e], s2)
    tot = s1 + s2
    w1, w2 = s1 / tot, s2 / tot
    first_lo = i1 < i2
    e_lo = jnp.where(first_lo, i1, i2).astype(F32)
    e_hi = jnp.where(first_lo, i2, i1).astype(F32)
    w_lo = jnp.where(first_lo, w1, w2)
    w_hi = jnp.where(first_lo, w2, w1)
    zero = jnp.zeros((1, tm), F32)
    r_ref[...] = jnp.concatenate([e_lo, e_hi, w_lo, w_hi, zero, zero, zero, zero], axis=0)


def _router(tl, xall, modtab, layer, norm_g, router_w, router_bias):
    d = tl.d
    vec = pl.BlockSpec((1, d), lambda i: (0, 0))
    return pl.pallas_call(
        _router_kernel,
        grid=(tl.n_tiles,),
        in_specs=[
            pl.BlockSpec((TM, d), lambda i: (i, 0)),
            vec, tl.mod_spec(layer, 3), tl.mod_spec(layer, 4),
            pl.BlockSpec((N_EXPERTS, d), lambda i: (0, 0)),
            pl.BlockSpec((N_EXPERTS, 1), lambda i: (0, 0)),
        ],
        out_specs=[
            pl.BlockSpec((TM, d), lambda i: (i, 0)),
            pl.BlockSpec((8, TM), lambda i: (0, i)),
        ],
        out_shape=[
            jax.ShapeDtypeStruct((tl.n_tok, d), F32),
            jax.ShapeDtypeStruct((8, tl.n_tok), F32),
        ],
        compiler_params=_cparams(("arbitrary",)),
        name="moe_router",
    )(xall, norm_g.reshape(1, d), modtab, modtab, router_w.T, router_bias.reshape(N_EXPERTS, 1))


def _moe_tiles(n_tok):
    return -(-(TOP_K * n_tok + N_EXPERTS * (MOE_ROWS - 1)) // MOE_ROWS) + 1


def _dispatch(route, n_tok):
    e = jnp.stack([route[0], route[1]], axis=1).astype(jnp.int32).reshape(TOP_K * n_tok)
    onehot = (e[:, None] == jnp.arange(N_EXPERTS, dtype=jnp.int32)[None, :]).astype(jnp.int32)
    csum = jnp.cumsum(onehot, axis=0)
    rank = jnp.sum(csum * onehot, axis=1) - 1
    cnt = csum[-1]
    ntile = (cnt + MOE_ROWS - 1) // MOE_ROWS
    tile_end = jnp.cumsum(ntile)
    tile_start = tile_end - ntile
    pos = jnp.sum(onehot * tile_start[None, :], axis=1) * MOE_ROWS + rank
    n_tiles = _moe_tiles(n_tok)
    n_used = tile_end[-1]
    tiles = jnp.minimum(jnp.arange(n_tiles, dtype=jnp.int32), n_used - 1)
    tile_expert = jnp.sum((tiles[:, None] >= tile_end[None, :]).astype(jnp.int32), axis=1)
    tile_expert = tile_expert.astype(jnp.int32)
    next_expert = tile_expert[jnp.minimum(tile_end[tile_expert], n_tiles - 1)]
    tok = jnp.arange(TOP_K * n_tok, dtype=jnp.int32) // TOP_K
    slot_tok = jnp.zeros((n_tiles * MOE_ROWS,), jnp.int32).at[pos].set(tok)
    pos = pos.astype(jnp.int32).reshape(n_tok, TOP_K)
    return pos[:, 0], pos[:, 1], slot_tok, tile_expert, next_expert, n_used.reshape(1).astype(jnp.int32)


def _row_copy(src_hbm, row, dst, r, sem):
    return pltpu.make_async_copy(src_hbm.at[pl.ds(row, 1), :], dst.at[pl.ds(r, 1), :], sem)


def _wait_rows(src_hbm, dst, sem):
    pltpu.make_async_copy(src_hbm.at[pl.ds(0, dst.shape[0]), :], dst, sem).wait()


def _ffn_kernel(te_ref, nxt_ref, nused_ref, stok_ref, h_hbm, wg_hbm, wu_hbm, wd_hbm, o_ref,
                xbuf, wgs, wus, wds, wgb, wub, wdb, sem, wsem, *, layer):
    t = pl.program_id(0)
    slot = t % 2
    n_used = nused_ref[0]

    def weight_copies(e):
        return (pltpu.make_async_copy(wg_hbm.at[layer, e], wgs, wsem.at[0]),
                pltpu.make_async_copy(wu_hbm.at[layer, e], wus, wsem.at[1]),
                pltpu.make_async_copy(wd_hbm.at[layer, e], wds, wsem.at[2]))

    @pl.when(t == 0)
    def _prime():
        for cp in weight_copies(te_ref[0]):
            cp.start()
        for r in range(MOE_ROWS):
            _row_copy(h_hbm, stok_ref[r], xbuf.at[0], r, sem.at[0]).start()

    @pl.when((t < n_used) & ((t == 0) | (te_ref[t] != te_ref[jnp.maximum(t - 1, 0)])))
    def _new_expert():
        for cp in weight_copies(te_ref[t]):
            cp.wait()
        wgb[...] = wgs[...].astype(BF16)
        wub[...] = wus[...].astype(BF16)
        wdb[...] = wds[...].astype(BF16)

        @pl.when(nxt_ref[t] != te_ref[t])
        def _stage_next():
            for cp in weight_copies(nxt_ref[t]):
                cp.start()

    @pl.when(t <= n_used)
    def _wait():
        _wait_rows(h_hbm, xbuf.at[slot], sem.at[slot])

    @pl.when(t < n_used)
    def _compute():
        xb = xbuf[slot].astype(BF16)
        for r in range(MOE_ROWS):
            _row_copy(h_hbm, stok_ref[(t + 1) * MOE_ROWS + r], xbuf.at[1 - slot], r, sem.at[1 - slot]).start()
        hg = jnp.dot(xb, wgb[...], preferred_element_type=F32)
        hu = jnp.dot(xb, wub[...], preferred_element_type=F32)
        act = (_silu(hg) * hu).astype(BF16)
        o_ref[...] = jnp.dot(act, wdb[...], preferred_element_type=F32)

    @pl.when(t >= n_used)
    def _idle():
        o_ref[...] = jnp.zeros_like(o_ref)


def _moe_ffn(n_tok, h2, slot_tok, tile_expert, next_expert, n_used, layer, wg, wu, wd):
    _, _, d, f = wg.shape
    n_tiles = _moe_tiles(n_tok)
    any_spec = pl.BlockSpec(memory_space=pl.ANY)
    grid_spec = pltpu.PrefetchScalarGridSpec(
        num_scalar_prefetch=4,
        grid=(n_tiles,),
        in_specs=[any_spec, any_spec, any_spec, any_spec],
        out_specs=pl.BlockSpec((MOE_ROWS, d), lambda t, te, nx, nu, st: (t, 0)),
        scratch_shapes=[
            pltpu.VMEM((2, MOE_ROWS, d), F32),
            pltpu.VMEM((d, f), F32), pltpu.VMEM((d, f), F32), pltpu.VMEM((f, d), F32),
            pltpu.VMEM((d, f), BF16), pltpu.VMEM((d, f), BF16), pltpu.VMEM((f, d), BF16),
            pltpu.SemaphoreType.DMA((2,)),
            pltpu.SemaphoreType.DMA((3,)),
        ],
    )
    return pl.pallas_call(
        functools.partial(_ffn_kernel, layer=layer),
        grid_spec=grid_spec,
        out_shape=jax.ShapeDtypeStruct((n_tiles * MOE_ROWS, d), F32),
        compiler_params=_cparams(("arbitrary",)),
        name="moe_ffn",
    )(tile_expert, next_expert, n_used, slot_tok, h2, wg, wu, wd)


COMBINE_TM = 256


def _combine_kernel(p1_ref, p2_ref, x_ref, gate_ref, w_ref, fg_ref, y_hbm, o_ref, ybuf, sem, *, final):
    i = pl.program_id(0)
    n_i = pl.num_programs(0)
    slot = i % 2
    tm = COMBINE_TM
    nxt = jnp.minimum(i + 1, n_i - 1)

    def gather8(tile, r0, dst, dsem):
        for r in range(8):
            _row_copy(y_hbm, p1_ref[tile * tm + r0 + r], dst.at[0], r0 + r, dsem).start()
            _row_copy(y_hbm, p2_ref[tile * tm + r0 + r], dst.at[1], r0 + r, dsem).start()

    @pl.when(i == 0)
    def _prime():
        def body(g, carry):
            gather8(0, g * 8, ybuf.at[0], sem.at[0])
            return carry
        lax.fori_loop(0, tm // 8, body, 0)

    _wait_rows(y_hbm, ybuf.at[slot, 0], sem.at[slot])
    _wait_rows(y_hbm, ybuf.at[slot, 1], sem.at[slot])

    for r0 in range(0, tm, 8):
        gather8(nxt, r0, ybuf.at[1 - slot], sem.at[1 - slot])
        w = w_ref[r0:r0 + 8, :]
        y = w[:, 0:1] * ybuf[slot, 0, r0:r0 + 8, :] + w[:, 1:2] * ybuf[slot, 1, r0:r0 + 8, :]
        x = x_ref[r0:r0 + 8, :] + gate_ref[...] * y
        if final:
            ms = jnp.mean(x * x, axis=-1, keepdims=True)
            x = x * lax.rsqrt(ms + NORM_EPS) * fg_ref[...]
        o_ref[r0:r0 + 8, :] = x

    @pl.when(i == n_i - 1)
    def _drain():
        _wait_rows(y_hbm, ybuf.at[1 - slot, 0], sem.at[1 - slot])
        _wait_rows(y_hbm, ybuf.at[1 - slot, 1], sem.at[1 - slot])


def _moe_combine(tl, xall, y_sorted, pos1, pos2, gates, modtab, layer, final_g, final):
    d = tl.d
    tm = COMBINE_TM
    per_tile = TM // tm
    skip_steps = tl.n_ctx // tm if final else 0
    grid_spec = pltpu.PrefetchScalarGridSpec(
        num_scalar_prefetch=2,
        grid=(tl.n_tok // tm,),
        in_specs=[
            pl.BlockSpec((tm, d), lambda i, p1, p2: (i, 0)),
            tl.mod_spec(layer, 5, tile_of=lambda i, p1, p2: i // per_tile),
            pl.BlockSpec((tm, TOP_K), lambda i, p1, p2: (i, 0)),
            pl.BlockSpec((1, d), lambda i, p1, p2: (0, 0)),
            pl.BlockSpec(memory_space=pl.ANY),
        ],
        out_specs=pl.BlockSpec((tm, d), lambda i, p1, p2: (jnp.maximum(i - skip_steps, 0), 0)),
        scratch_shapes=[
            pltpu.VMEM((2, TOP_K, tm, d), F32),
            pltpu.SemaphoreType.DMA((2,)),
        ],
    )
    return pl.pallas_call(
        functools.partial(_combine_kernel, final=final),
        grid_spec=grid_spec,
        out_shape=jax.ShapeDtypeStruct((tl.n_tok - skip_steps * tm, d), F32),
        compiler_params=_cparams(("arbitrary",)),
        name="moe_combine",
    )(pos1, pos2, xall, modtab, gates, final_g.reshape(1, d), y_sorted)


def _moe_layer(tl, xall, modtab, layer, norm_g, router_w, router_bias, wg, wu, wd, final_g, final):
    h2, route = _router(tl, xall, modtab, layer, norm_g, router_w, router_bias)
    pos1, pos2, slot_tok, tile_expert, next_expert, n_used = _dispatch(route, tl.n_tok)
    y_sorted = _moe_ffn(tl.n_tok, h2, slot_tok, tile_expert, next_expert, n_used, layer, wg, wu, wd)
    gates = route[2:4].T
    return _moe_combine(tl, xall, y_sorted, pos1, pos2, gates, modtab, layer, final_g, final)


INPROJ_TN = 1024
OUTPROJ_TN = 512
CONV_HALO = 16


def _normmod_kernel(x_ref, g_ref, sh_ref, sc_ref, wdt_ref, h_ref, dt_ref):
    h = _norm_mod(x_ref[...], g_ref[...], sh_ref[...], sc_ref[...]).astype(BF16)
    h_ref[...] = h
    dt_ref[...] = jnp.dot(h, wdt_ref[...].astype(BF16), preferred_element_type=F32)


def _normmod(tl, xall, modtab, layer, norm_g, w_dt):
    d = tl.d
    n_dt = w_dt.shape[1]
    return pl.pallas_call(
        _normmod_kernel,
        grid=(tl.n_tiles,),
        in_specs=[pl.BlockSpec((TM, d), lambda i: (i, 0)), pl.BlockSpec((1, d), lambda i: (0, 0)),
                  tl.mod_spec(layer, 0), tl.mod_spec(layer, 1), pl.BlockSpec((d, n_dt), lambda i: (0, 0))],
        out_specs=[pl.BlockSpec((TM, d), lambda i: (i, 0)), pl.BlockSpec((TM, n_dt), lambda i: (i, 0))],
        out_shape=[jax.ShapeDtypeStruct((tl.n_tok, d), BF16), jax.ShapeDtypeStruct((tl.n_tok, n_dt), F32)],
        compiler_params=_cparams(("arbitrary",)),
        name="ssd_normmod",
    )(xall, norm_g.reshape(1, d), modtab, modtab, w_dt)


def _inproj_kernel(h_ref, w_ref, wt_ref, p_ref, wb, *, n_main_blocks):
    n = pl.program_id(0)

    @pl.when((pl.program_id(1) == 0) & (n < n_main_blocks))
    def _main_columns():
        wb[...] = w_ref[...].astype(BF16)

    @pl.when((pl.program_id(1) == 0) & (n >= n_main_blocks))
    def _tail_columns():
        wb[...] = wt_ref[...].astype(BF16)

    p_ref[...] = jnp.dot(h_ref[...], wb[...], preferred_element_type=F32).astype(BF16)


def _inproj(tl, h, in_w_all, j, n_main, w_tail):
    d = tl.d
    tn = INPROJ_TN
    n_main_blocks = n_main // tn
    n_tail_blocks = w_tail.shape[1] // tn
    return pl.pallas_call(
        functools.partial(_inproj_kernel, n_main_blocks=n_main_blocks),
        grid=(n_main_blocks + n_tail_blocks, tl.n_tiles),
        in_specs=[
            pl.BlockSpec((TM, d), lambda n, i: (i, 0)),
            pl.BlockSpec((None, d, tn), lambda n, i: (j, 0, jnp.minimum(n, n_main_blocks - 1))),
            pl.BlockSpec((d, tn), lambda n, i: (0, jnp.maximum(n - n_main_blocks, 0))),
        ],
        out_specs=pl.BlockSpec((TM, tn), lambda n, i: (i, n)),
        out_shape=jax.ShapeDtypeStruct((tl.n_tok, n_main + w_tail.shape[1]), BF16),
        scratch_shapes=[pltpu.VMEM((d, tn), BF16)],
        compiler_params=_cparams(("arbitrary", "arbitrary")),
        name="ssd_inproj",
    )(h, in_w_all, w_tail)


def _scan_kernel(xs_ref, xsp_ref, xsn_ref, bc_ref, bcp_ref, bcn_ref, dt_ref, cw_ref, cb_ref, hp_ref,
                 dsk_ref, y_ref, ubuf, xc, xcb, bm, cm, cmf, bmt, acst, wt, state, *, ncc, ncl):
    q = SSD_CHUNK
    di = xs_ref.shape[1]
    n_grp = SSD_GROUPS
    blocks_per_group = di // n_grp // 128
    d = pl.program_id(0)
    c = pl.program_id(2)
    is_ctx = c < ncc
    fwd = d == 0
    tch = jnp.where(is_ctx, jnp.where(fwd, c, ncc - 1 - c), jnp.where(fwd, c - ncc, ncl - 1 - (c - ncc)))
    first = tch == 0
    last = tch == jnp.where(is_ctx, ncc - 1, ncl - 1)

    @pl.when(c == 0)
    def _reset():
        state[...] = jnp.zeros_like(state)

    hl = CONV_HALO
    ubuf[0:hl, 0:di] = jnp.where(first, 0.0, xsp_ref[...].astype(F32))
    ubuf[hl:hl + q, 0:di] = xs_ref[...].astype(F32)
    ubuf[hl + q:hl + q + hl, 0:di] = jnp.where(last, 0.0, xsn_ref[...].astype(F32))
    ubuf[0:hl, di:] = jnp.where(first, 0.0, bcp_ref[...].astype(F32))
    ubuf[hl:hl + q, di:] = bc_ref[...].astype(F32)
    ubuf[hl + q:hl + q + hl, di:] = jnp.where(last, 0.0, bcn_ref[...].astype(F32))

    def conv_block(col):
        acc = cb_ref[:, pl.ds(col, 128)]
        for k in range(SSD_CONV):
            off = hl + k - SSD_CONV // 2
            acc = acc + cw_ref[k:k + 1, pl.ds(col, 128)] * ubuf[off:off + q, pl.ds(col, 128)]
        return _silu(acc)

    hp = hp_ref[...]
    a_coef = -jnp.exp(hp[0:1])
    raw = dt_ref[...] + hp[1:2]
    dt = jnp.maximum(raw, 0.0) + jnp.log(1.0 + jnp.exp(-jnp.abs(raw)))
    a = dt * a_coef
    li = lax.broadcasted_iota(jnp.int32, (q, q), 0)
    si = lax.broadcasted_iota(jnp.int32, (q, q), 1)
    tri = jnp.where(fwd, li - si, si - li) >= 0
    trib = jnp.where(tri, 1.0, 0.0).astype(BF16)
    a_hi = a.astype(BF16)
    r1 = a - a_hi.astype(F32)
    a_mid = r1.astype(BF16)
    a_lo = (r1 - a_mid.astype(F32)).astype(BF16)
    acs = (jnp.dot(trib, a_hi, preferred_element_type=F32) + jnp.dot(trib, a_mid, preferred_element_type=F32)
           + jnp.dot(trib, a_lo, preferred_element_type=F32))
    tot = jnp.sum(a, axis=0, keepdims=True)
    etot = jnp.exp(tot)
    acst[...] = (acs - jnp.log(dt)).T
    wt[...] = (dt * jnp.exp(tot - acs)).T

    def xs_body(j, carry):
        col = pl.multiple_of(j * 128, 128)
        v = conv_block(col)
        xc[:, pl.ds(col, 128)] = v
        xcb[:, pl.ds(col, 128)] = v.astype(BF16)
        return carry
    lax.fori_loop(0, di // 128, xs_body, 0, unroll=2)

    def b_body(j, carry):
        col = pl.multiple_of(j * 128, 128)
        v = conv_block(di + col)
        bm[:, pl.ds(col, 128)] = v.astype(BF16)
        bmt[j] = v.T
        return carry
    lax.fori_loop(0, n_grp, b_body, 0)

    def c_body(j, carry):
        col = pl.multiple_of(j * 128, 128)
        v = conv_block(di + n_grp * SSD_STATE + col)
        cmf[:, pl.ds(col, 128)] = v
        cm[:, pl.ds(col, 128)] = v.astype(BF16)
        return carry
    lax.fori_loop(0, n_grp, c_body, 0, unroll=2)

    lane = lax.broadcasted_iota(jnp.int32, (q, 128), 1)
    left = lane < SSD_HEAD_DIM
    left_row = left[0:1, :]
    for pb in range(di // 128):
        grp = pb // blocks_per_group
        gl = slice(grp * SSD_STATE, (grp + 1) * SSD_STATE)
        pc = slice(pb * 128, (pb + 1) * 128)
        sc = slice((pb % blocks_per_group) * 128, (pb % blocks_per_group + 1) * 128)
        if pb % blocks_per_group == 0:
            cbm = lax.dot_general(cm[:, gl], bm[:, gl], (((1,), (1,)), ((), ())), preferred_element_type=F32)
        h0, h1 = 2 * pb, 2 * pb + 1
        ms, cs, bts = [], [], []
        for h in (h0, h1):
            colb = jnp.broadcast_to(acs[:, h:h + 1], (q, q))
            arg = jnp.where(tri, colb - acst[h:h + 1, :], -jnp.inf)
            ms.append((cbm * jnp.exp(arg)).astype(BF16))
            cs.append((cmf[:, gl] * jnp.exp(colb)).astype(BF16))
            bts.append((bmt[grp] * wt[h:h + 1, :]).astype(BF16))
        xp = xcb[:, pc]
        zero = jnp.zeros_like(xp)
        xe, xo = jnp.where(left, xp, zero), jnp.where(left, zero, xp)
        s_old = state[grp, :, sc]
        sb = s_old.astype(BF16)
        se, so = jnp.where(left, sb, zero), jnp.where(left, zero, sb)
        lhs = jnp.concatenate(ms + cs, axis=1)
        rhs = jnp.concatenate([xe, xo, se, so], axis=0)
        y = jnp.dot(lhs, rhs, preferred_element_type=F32) + xc[:, pc] * dsk_ref[:, pc]
        y_ref[:, pc] = y.astype(BF16)
        et = jnp.where(left_row, etot[:, h0:h0 + 1], etot[:, h1:h1 + 1])
        upd = jnp.dot(jnp.concatenate(bts, axis=1), jnp.concatenate([xe, xo], axis=0), preferred_element_type=F32)
        state[grp, :, sc] = s_old * et + upd


def _ssd_scan(tl, proj, dt2, conv_w, conv_b, head_params, dskip_lanes):
    n_tok = tl.n_tok
    q = SSD_CHUNK
    n_heads = dt2.shape[-1]
    di = n_heads * SSD_HEAD_DIM
    gn = SSD_GROUPS * SSD_STATE
    conv_ch = di + 2 * gn
    assert (2 * di) % (2 * gn) == 0 and (di // SSD_GROUPS) % 128 == 0
    ncc, ncl = tl.ctx_len // q, tl.seq // q
    bc_block0 = 2 * di // (2 * gn)
    hpc = q // CONV_HALO
    last_halo = n_tok // CONV_HALO - 1

    def blk(d, b, c):
        fwd = d == 0
        is_ctx = c < ncc
        t_ctx = jnp.where(fwd, c, ncc - 1 - c)
        t_lat = jnp.where(fwd, c - ncc, ncl - 1 - (c - ncc))
        return jnp.where(is_ctx, b * ncc + t_ctx, tl.bsz * ncc + b * ncl + t_lat)

    prev = lambda d, b, c: jnp.maximum(blk(d, b, c) * hpc - 1, 0)
    nxt = lambda d, b, c: jnp.minimum((blk(d, b, c) + 1) * hpc, last_halo)
    kern = functools.partial(_scan_kernel, ncc=ncc, ncl=ncl)
    return pl.pallas_call(
        kern,
        grid=(2, tl.bsz, ncc + ncl),
        in_specs=[
            pl.BlockSpec((q, di), lambda d, b, c: (blk(d, b, c), 1)),
            pl.BlockSpec((CONV_HALO, di), lambda d, b, c: (prev(d, b, c), 1)),
            pl.BlockSpec((CONV_HALO, di), lambda d, b, c: (nxt(d, b, c), 1)),
            pl.BlockSpec((q, 2 * gn), lambda d, b, c: (blk(d, b, c), bc_block0 + d)),
            pl.BlockSpec((CONV_HALO, 2 * gn), lambda d, b, c: (prev(d, b, c), bc_block0 + d)),
            pl.BlockSpec((CONV_HALO, 2 * gn), lambda d, b, c: (nxt(d, b, c), bc_block0 + d)),
            pl.BlockSpec((None, q, n_heads), lambda d, b, c: (d, blk(d, b, c), 0)),
            pl.BlockSpec((None, SSD_CONV, conv_ch), lambda d, b, c: (d, 0, 0)),
            pl.BlockSpec((None, 1, conv_ch), lambda d, b, c: (d, 0, 0)),
            pl.BlockSpec((None, 8, n_heads), lambda d, b, c: (d, 0, 0)),
            pl.BlockSpec((None, 1, di), lambda d, b, c: (d, 0, 0)),
        ],
        out_specs=pl.BlockSpec((None, q, di), lambda d, b, c: (d, blk(d, b, c), 0)),
        out_shape=jax.ShapeDtypeStruct((2, n_tok, di), BF16),
        scratch_shapes=[
            pltpu.VMEM((q + 2 * CONV_HALO, conv_ch), F32),
            pltpu.VMEM((q, di), F32),
            pltpu.VMEM((q, di), BF16),
            pltpu.VMEM((q, gn), BF16),
            pltpu.VMEM((q, gn), BF16),
            pltpu.VMEM((q, gn), F32),
            pltpu.VMEM((SSD_GROUPS, SSD_STATE, q), F32),
            pltpu.VMEM((n_heads, q), F32),
            pltpu.VMEM((n_heads, q), F32),
            pltpu.VMEM((SSD_GROUPS, SSD_STATE, di // SSD_GROUPS), F32),
        ],
        compiler_params=_cparams(("arbitrary", "arbitrary", "arbitrary")),
        name="ssd_scan",
    )(proj, proj, proj, proj, proj, proj, dt2, conv_w, conv_b.reshape(2, 1, conv_ch), head_params, dskip_lanes)


def _outproj_kernel(y0_ref, y1_ref, z_ref, ng_ref, w_ref, x_ref, gate_ref, o_ref, ybuf):
    @pl.when(pl.program_id(1) == 0)
    def _prologue():
        gw = ybuf.shape[1] // SSD_GROUPS
        for grp in range(SSD_GROUPS):
            cols = slice(grp * gw, (grp + 1) * gw)
            y = (y0_ref[:, cols].astype(F32) + y1_ref[:, cols].astype(F32)) * _silu(z_ref[:, cols].astype(F32))
            ms = jnp.mean(y * y, axis=-1, keepdims=True)
            ybuf[:, cols] = (y * lax.rsqrt(ms + NORM_EPS) * ng_ref[:, cols]).astype(BF16)

    o_ref[...] = x_ref[...] + gate_ref[...] * jnp.dot(ybuf[...], w_ref[...], preferred_element_type=F32)


def _outproj(tl, xall, modtab, layer, y2, proj, norm_g, out_w):
    d = tl.d
    di = out_w.shape[0]
    tn = OUTPROJ_TN
    gate_spec = pl.BlockSpec(
        (None, 1, tn), lambda i, j: ((layer * COND_ROWS + tl.seg(i)) * N_MOD + 2, 0, j))
    return pl.pallas_call(
        _outproj_kernel,
        grid=(tl.n_tiles, d // tn),
        in_specs=[
            pl.BlockSpec((None, TM, di), lambda i, j: (0, i, 0)),
            pl.BlockSpec((None, TM, di), lambda i, j: (1, i, 0)),
            pl.BlockSpec((TM, di), lambda i, j: (i, 0)),
            pl.BlockSpec((1, di), lambda i, j: (0, 0)),
            pl.BlockSpec((di, tn), lambda i, j: (0, j)),
            pl.BlockSpec((TM, tn), lambda i, j: (i, j)),
            gate_spec,
        ],
        out_specs=pl.BlockSpec((TM, tn), lambda i, j: (i, j)),
        out_shape=jax.ShapeDtypeStruct((tl.n_tok, d), F32),
        scratch_shapes=[pltpu.VMEM((TM, di), BF16)],
        compiler_params=_cparams(("arbitrary", "arbitrary")),
        name="ssd_outproj",
    )(y2, y2, proj, norm_g.reshape(1, di), out_w, xall, modtab)


def _ssd_layer(tl, xall, modtab, layer, norm_g, in_w_all, j, conv_w, conv_b, a_log, dt_bias, d_skip, ssd_norm_g,
               out_w):
    n_heads = a_log.shape[-1]
    di = n_heads * SSD_HEAD_DIM
    gn = SSD_GROUPS * SSD_STATE
    dir_cols = 2 * gn + n_heads
    base0 = 2 * di
    base1 = base0 + dir_cols
    n_main = base0 + 2 * gn
    in_w = in_w_all[j]
    w_tail = in_w[:, base1:base1 + 2 * gn]
    w_dt = jnp.concatenate(
        [in_w[:, base0 + 2 * gn:base0 + dir_cols], in_w[:, base1 + 2 * gn:base1 + dir_cols]], axis=1)
    h, dt_raw = _normmod(tl, xall, modtab, layer, norm_g, w_dt)
    proj = _inproj(tl, h, in_w_all, j, n_main, w_tail)
    dt2 = dt_raw.reshape(tl.n_tok, 2, n_heads).transpose(1, 0, 2)
    head_params = jnp.zeros((2, 8, n_heads), F32).at[:, 0].set(a_log).at[:, 1].set(dt_bias)
    dskip_lanes = jnp.repeat(d_skip, SSD_HEAD_DIM, axis=-1).reshape(2, 1, di)
    y2 = _ssd_scan(tl, proj, dt2, conv_w, conv_b, head_params, dskip_lanes)
    return _outproj(tl, xall, modtab, layer, y2, proj, ssd_norm_g, out_w.astype(BF16))


def kernel(x, c, ctx, c_ctx, mod_w, mod_b, norm_mix_g, norm_ffn_g, pool_w, pool_b, pool_scale, ssd_in_w,
           ssd_conv_w, ssd_conv_b, ssd_a_log, ssd_dt_bias, ssd_d, ssd_norm_g, ssd_out_w, router_w, router_bias,
           moe_w_gate, moe_w_up, moe_w_down, final_norm_g):
    bsz, seq, d = x.shape
    ctx_len = ctx.shape[1]
    depth = mod_w.shape[0]
    assert bsz + 1 <= COND_ROWS
    tl = _Tiles(bsz, ctx_len, seq, d)
    cond = jnp.concatenate([c_ctx[None, :], c, jnp.zeros((COND_ROWS - 1 - bsz, d), F32)], axis=0)
    modtab = _adaln_table(cond, mod_w, mod_b)
    xall = jnp.concatenate([ctx.reshape(bsz * ctx_len, d), x.reshape(bsz * seq, d)], axis=0)
    for i in range(depth):
        j = i // 2
        if i % 2 == 0:
            xall = _pool_layer(tl, xall, modtab, i, norm_mix_g[i], pool_w[j], pool_b[j], pool_scale[j])
        else:
            xall = _ssd_layer(tl, xall, modtab, i, norm_mix_g[i], ssd_in_w, j, ssd_conv_w[j], ssd_conv_b[j],
                              ssd_a_log[j], ssd_dt_bias[j], ssd_d[j], ssd_norm_g[j], ssd_out_w[j])
        xall = _moe_layer(tl, xall, modtab, i, norm_ffn_g[i], router_w, router_bias,
                          moe_w_gate, moe_w_up, moe_w_down, final_norm_g, i == depth - 1)
    return xall.reshape(bsz, seq, d)
```

```python
import functools

import jax
import jax.numpy as jnp
from jax import lax
from jax.experimental import pallas as pl
from jax.experimental.pallas import tpu as pltpu

F32 = jnp.float32
BF16 = jnp.bfloat16

GRID_W = 64
NORM_EPS = 1e-6
N_MOD = 6
POOL_GROUPS = 4
POOL_WINDOWS_1D = (2, 4, 8, 16)
POOL_WINDOWS_2D = ((1, 2), (2, 2), (2, 4), (4, 4))
SSD_HEAD_DIM = 64
SSD_GROUPS = 8
SSD_STATE = 128
SSD_CONV = 4
SSD_CHUNK = 128
N_EXPERTS = 16
N_EXPERT_GROUPS = 4
EXPERTS_PER_GROUP = 4
TOP_K = 2

TM = 512
MOE_ROWS = 128
COND_ROWS = 8
VMEM_LIMIT = 56 * 1024 * 1024


def _silu(v):
    return v / (1.0 + jnp.exp(-v))


def _norm_mod(x, g, shift, scale):
    ms = jnp.mean(x * x, axis=-1, keepdims=True)
    y = x * lax.rsqrt(ms + NORM_EPS) * g
    return y * (1.0 + scale) + shift


def _cparams(sem):
    return pltpu.CompilerParams(dimension_semantics=sem, vmem_limit_bytes=VMEM_LIMIT)


def _adaln_kernel(cond_ref, w_ref, b_ref, o_ref):
    s = _silu(cond_ref[...]).astype(BF16)
    o_ref[...] = jnp.dot(s, w_ref[...].astype(BF16), preferred_element_type=F32) + b_ref[...]


def _adaln_table(cond, mod_w, mod_b):
    depth, d, nd = mod_w.shape
    tn = 1024 if nd % 1024 == 0 else nd
    out = pl.pallas_call(
        _adaln_kernel,
        grid=(depth, nd // tn),
        in_specs=[
            pl.BlockSpec((COND_ROWS, d), lambda l, j: (0, 0)),
            pl.BlockSpec((None, d, tn), lambda l, j: (l, 0, j)),
            pl.BlockSpec((None, 1, tn), lambda l, j: (l, 0, j)),
        ],
        out_specs=pl.BlockSpec((None, COND_ROWS, tn), lambda l, j: (l, 0, j)),
        out_shape=jax.ShapeDtypeStruct((depth, COND_ROWS, nd), F32),
        compiler_params=_cparams(("arbitrary", "arbitrary")),
        name="adaln",
    )(cond, mod_w, mod_b.reshape(depth, 1, nd))
    return out.reshape(depth * COND_ROWS * N_MOD, 1, d)


class _Tiles:
    def __init__(self, bsz, ctx_len, seq, d):
        assert bsz * ctx_len == TM and seq % TM == 0
        self.bsz, self.ctx_len, self.seq, self.d = bsz, ctx_len, seq, d
        self.n_ctx = bsz * ctx_len
        self.n_tok = self.n_ctx + bsz * seq
        self.n_tiles = self.n_tok // TM
        self.tiles_per_batch = seq // TM

    def seg(self, i):
        return jnp.where(i == 0, 0, 1 + (i - 1) // self.tiles_per_batch)

    def mod_spec(self, layer, k, tile_of=lambda *idx: idx[0]):
        def imap(*idx):
            return ((layer * COND_ROWS + self.seg(tile_of(*idx))) * N_MOD + k, 0, 0)
        return pl.BlockSpec((None, 1, self.d), imap)


POOL_MARGIN = 8
POOL_PREV = 2 * GRID_W
POOL_NEXT = GRID_W


def _pool_kernel(xp_ref, xc_ref, xn_ref, g_ref, sh_ref, sc_ref, gate_ref, w_ref, b_ref, ps_ref,
                 o_ref, hbuf, cpbuf, dbuf, *, ctx_len, tiles_per_batch, rows_per_batch):
    i = pl.program_id(0)
    gd = w_ref.shape[-1]
    g, sh, sc = g_ref[...], sh_ref[...], sc_ref[...]
    base = POOL_MARGIN + POOL_PREV
    zeros_m = jnp.zeros((POOL_MARGIN, hbuf.shape[1]), F32)
    hbuf[0:POOL_MARGIN, :] = zeros_m
    hbuf[base + TM + POOL_NEXT:base + TM + POOL_NEXT + POOL_MARGIN, :] = zeros_m
    hbuf[POOL_MARGIN:base, :] = _norm_mod(xp_ref[...], g, sh, sc)
    hbuf[base:base + TM, :] = _norm_mod(xc_ref[...], g, sh, sc)
    hbuf[base + TM:base + TM + POOL_NEXT, :] = _norm_mod(xn_ref[...], g, sh, sc)

    def finish(grp):
        cols = slice(grp * gd, (grp + 1) * gd)
        y = jnp.dot(dbuf[...], w_ref[grp].astype(BF16), preferred_element_type=F32)
        y = (y + b_ref[:, cols]) * ps_ref[:, cols]
        o_ref[:, cols] = xc_ref[:, cols] + gate_ref[:, cols] * y

    @pl.when(i == 0)
    def _ctx():
        ch = GRID_W
        for grp in range(POOL_GROUPS):
            cols = slice(grp * gd, (grp + 1) * gd)
            w = POOL_WINDOWS_1D[grp]
            lo, hi = w // 2, w - 1 - w // 2
            for k in range(TM // ch):
                pos = (k * ch) % ctx_len + lax.broadcasted_iota(jnp.int32, (ch, 1), 0)
                acc = jnp.zeros((ch, gd), F32)
                for dd in range(-lo, hi + 1):
                    v = hbuf[base + k * ch + dd:base + k * ch + dd + ch, cols]
                    ok = (pos + dd >= 0) & (pos + dd < ctx_len)
                    acc = acc + jnp.where(ok, v, 0.0)
                cnt = jnp.minimum(pos + hi, ctx_len - 1) - jnp.maximum(pos - lo, 0) + 1
                t = acc / cnt.astype(F32)
                hcur = hbuf[base + k * ch:base + (k + 1) * ch, cols]
                dbuf[k * ch:(k + 1) * ch, :] = (t - hcur).astype(BF16)
            finish(grp)

    @pl.when(i > 0)
    def _latent():
        tile_in_batch = (i - 1) % tiles_per_batch
        row0 = tile_in_batch * (TM // GRID_W)
        col = lax.broadcasted_iota(jnp.int32, (GRID_W, 1), 0)
        n_rows = TM // GRID_W
        for grp in range(POOL_GROUPS):
            cols = slice(grp * gd, (grp + 1) * gd)
            wr, wc = POOL_WINDOWS_2D[grp]
            lo_r, hi_r = wr // 2, wr - 1 - wr // 2
            lo_c, hi_c = wc // 2, wc - 1 - wc // 2
            cnt_c = (jnp.minimum(col + hi_c, GRID_W - 1) - jnp.maximum(col - lo_c, 0) + 1).astype(F32)
            for rr in range(2 - lo_r, 2 + n_rows + hi_r):
                start = POOL_MARGIN + rr * GRID_W
                acc = jnp.zeros((GRID_W, gd), F32)
                for dc in range(-lo_c, hi_c + 1):
                    v = hbuf[start + dc:start + dc + GRID_W, cols]
                    ok = (col + dc >= 0) & (col + dc < GRID_W)
                    acc = acc + jnp.where(ok, v, 0.0)
                cpbuf[rr * GRID_W:(rr + 1) * GRID_W, :] = acc / cnt_c
            for r in range(n_rows):
                acc = jnp.zeros((GRID_W, gd), F32)
                cnt_r = jnp.zeros((GRID_W, 1), F32)
                for dr in range(-lo_r, hi_r + 1):
                    grow = row0 + r + dr
                    ok = (grow >= 0) & (grow < rows_per_batch)
                    v = cpbuf[(2 + r + dr) * GRID_W:(3 + r + dr) * GRID_W, :]
                    acc = acc + jnp.where(ok, v, 0.0)
                    cnt_r = cnt_r + jnp.where(ok, 1.0, 0.0)
                hcur = hbuf[base + r * GRID_W:base + (r + 1) * GRID_W, cols]
                dbuf[r * GRID_W:(r + 1) * GRID_W, :] = (acc / cnt_r - hcur).astype(BF16)
            finish(grp)


def _pool_layer(tl, xall, modtab, layer, norm_g, pool_w, pool_b, pool_scale):
    d = tl.d
    gd = d // POOL_GROUPS
    n_prev_blocks = TM // POOL_PREV
    n_next_blocks = TM // POOL_NEXT
    last_next = tl.n_tok // POOL_NEXT - 1
    kern = functools.partial(_pool_kernel, ctx_len=tl.ctx_len, tiles_per_batch=tl.tiles_per_batch,
                             rows_per_batch=tl.seq // GRID_W)
    vec = pl.BlockSpec((1, d), lambda i: (0, 0))
    buf_rows = 2 * POOL_MARGIN + POOL_PREV + TM + POOL_NEXT
    return pl.pallas_call(
        kern,
        grid=(tl.n_tiles,),
        in_specs=[
            pl.BlockSpec((POOL_PREV, d), lambda i: (jnp.maximum(i * n_prev_blocks - 1, 0), 0)),
            pl.BlockSpec((TM, d), lambda i: (i, 0)),
            pl.BlockSpec((POOL_NEXT, d), lambda i: (jnp.minimum((i + 1) * n_next_blocks, last_next), 0)),
            vec,
            tl.mod_spec(layer, 0), tl.mod_spec(layer, 1), tl.mod_spec(layer, 2),
            pl.BlockSpec((POOL_GROUPS, gd, gd), lambda i: (0, 0, 0)),
            vec, vec,
        ],
        out_specs=pl.BlockSpec((TM, d), lambda i: (i, 0)),
        out_shape=jax.ShapeDtypeStruct((tl.n_tok, d), F32),
        scratch_shapes=[
            pltpu.VMEM((buf_rows, d), F32),
            pltpu.VMEM((POOL_PREV + TM + POOL_NEXT, gd), F32),
            pltpu.VMEM((TM, gd), BF16),
        ],
        compiler_params=_cparams(("arbitrary",)),
        name="pool_layer",
    )(xall, xall, xall, norm_g.reshape(1, d), modtab, modtab, modtab, pool_w,
      pool_b.reshape(1, d), pool_scale.reshape(1, d))


def _router_kernel(x_ref, g_ref, sh_ref, sc_ref, rwt_ref, rb_ref, h_ref, r_ref):
    h = _norm_mod(x_ref[...], g_ref[...], sh_ref[...], sc_ref[...])
    h_ref[...] = h
    logits = lax.dot_general(rwt_ref[...], h, (((1,), (1,)), ((), ())),
                             precision=lax.Precision.HIGHEST, preferred_element_type=F32)
    scores = 1.0 / (1.0 + jnp.exp(-logits))
    sel = scores + rb_ref[...]
    tm = sel.shape[1]
    srow = [scores[e:e + 1, :] for e in range(N_EXPERTS)]
    vrow = [sel[e:e + 1, :] for e in range(N_EXPERTS)]
    best_g = jnp.zeros((1, tm), jnp.int32)
    best_s = None
    for grp in range(N_EXPERT_GROUPS):
        v = vrow[grp * EXPERTS_PER_GROUP:(grp + 1) * EXPERTS_PER_GROUP]
        gs = None
        for a in range(EXPERTS_PER_GROUP):
            for b in range(a + 1, EXPERTS_PER_GROUP):
                p = v[a] + v[b]
                gs = p if gs is None else jnp.maximum(gs, p)
        if best_s is None:
            best_s = gs
        else:
            upd = gs > best_s
            best_s = jnp.where(upd, gs, best_s)
            best_g = jnp.where(upd, grp, best_g)
    neg = jnp.full((1, tm), -jnp.inf, F32)
    masked = [jnp.where(best_g == (e // EXPERTS_PER_GROUP), vrow[e], neg) for e in range(N_EXPERTS)]
    m1, i1, s1 = masked[0], jnp.zeros((1, tm), jnp.int32), srow[0]
    for e in range(1, N_EXPERTS):
        upd = masked[e] > m1
        m1 = jnp.where(upd, masked[e], m1)
        i1 = jnp.where(upd, e, i1)
        s1 = jnp.where(upd, srow[e], s1)
    m2, i2, s2 = neg, jnp.full((1, tm), -1, jnp.int32), jnp.zeros((1, tm), F32)
    for e in range(N_EXPERTS):
        cand = i1 != e
        upd = cand & ((masked[e] > m2) | (i2 < 0))
        m2 = jnp.where(upd, masked[e], m2)
        i2 = jnp.where(upd, e, i2)
        s2 = jnp.where(upd, srow[e], s2)
    tot = s1 + s2
    w1, w2 = s1 / tot, s2 / tot
    first_lo = i1 < i2
    e_lo = jnp.where(first_lo, i1, i2).astype(F32)
    e_hi = jnp.where(first_lo, i2, i1).astype(F32)
    w_lo = jnp.where(first_lo, w1, w2)
    w_hi = jnp.where(first_lo, w2, w1)
    zero = jnp.zeros((1, tm), F32)
    r_ref[...] = jnp.concatenate([e_lo, e_hi, w_lo, w_hi, zero, zero, zero, zero], axis=0)


def _router(tl, xall, modtab, layer, norm_g, router_w, router_bias):
    d = tl.d
    vec = pl.BlockSpec((1, d), lambda i: (0, 0))
    return pl.pallas_call(
        _router_kernel,
        grid=(tl.n_tiles,),
        in_specs=[
            pl.BlockSpec((TM, d), lambda i: (i, 0)),
            vec, tl.mod_spec(layer, 3), tl.mod_spec(layer, 4),
            pl.BlockSpec((N_EXPERTS, d), lambda i: (0, 0)),
            pl.BlockSpec((N_EXPERTS, 1), lambda i: (0, 0)),
        ],
        out_specs=[
            pl.BlockSpec((TM, d), lambda i: (i, 0)),
            pl.BlockSpec((8, TM), lambda i: (0, i)),
        ],
        out_shape=[
            jax.ShapeDtypeStruct((tl.n_tok, d), F32),
            jax.ShapeDtypeStruct((8, tl.n_tok), F32),
        ],
        compiler_params=_cparams(("arbitrary",)),
        name="moe_router",
    )(xall, norm_g.reshape(1, d), modtab, modtab, router_w.T, router_bias.reshape(N_EXPERTS, 1))


_PAIR_ORDER = ((0, 1), (0, 2), (0, 3), (1, 3), (1, 2), (2, 3))
_PAIR_SLOT_A = (0, 0, 0, 1, 1, 3)
_PAIR_SLOT_B = (1, 2, 3, 3, 2, 2)
N_PAIR_CLASSES = N_EXPERT_GROUPS * len(_PAIR_ORDER)


def _moe_tiles(n_tok):
    return -(-(n_tok + N_PAIR_CLASSES * (MOE_ROWS - 1)) // MOE_ROWS) + 1


def _dispatch(route, n_tok):
    e_lo = route[0].astype(jnp.int32)
    e_hi = route[1].astype(jnp.int32)
    grp = e_lo // EXPERTS_PER_GROUP
    pair_of = [0] * (EXPERTS_PER_GROUP * EXPERTS_PER_GROUP)
    for j, (a, b) in enumerate(_PAIR_ORDER):
        pair_of[a * EXPERTS_PER_GROUP + b] = j
    local = (e_lo % EXPERTS_PER_GROUP) * EXPERTS_PER_GROUP + e_hi % EXPERTS_PER_GROUP
    cls = grp * len(_PAIR_ORDER) + jnp.asarray(pair_of, jnp.int32)[local]
    n_pair = len(_PAIR_ORDER)
    cls_ids = jnp.arange(N_PAIR_CLASSES, dtype=jnp.int32)
    slot_a_tab = (cls_ids // n_pair) * EXPERTS_PER_GROUP + jnp.asarray(_PAIR_SLOT_A, jnp.int32)[cls_ids % n_pair]
    slot_b_tab = (cls_ids // n_pair) * EXPERTS_PER_GROUP + jnp.asarray(_PAIR_SLOT_B, jnp.int32)[cls_ids % n_pair]
    onehot = (cls[:, None] == cls_ids[None, :]).astype(jnp.int32)
    csum = jnp.cumsum(onehot, axis=0)
    rank = jnp.sum(csum * onehot, axis=1) - 1
    cnt = csum[-1]
    ntile = (cnt + MOE_ROWS - 1) // MOE_ROWS
    tile_end = jnp.cumsum(ntile)
    tile_start = tile_end - ntile
    pos = (jnp.sum(onehot * tile_start[None, :], axis=1) * MOE_ROWS + rank).astype(jnp.int32)
    n_tiles = _moe_tiles(n_tok)
    n_used = tile_end[-1]
    tiles = jnp.arange(n_tiles, dtype=jnp.int32)
    tcls = jnp.sum((jnp.minimum(tiles, n_used - 1)[:, None] >= tile_end[None, :]).astype(jnp.int32), axis=1)
    ea = slot_a_tab[tcls]
    eb = slot_b_tab[tcls]
    prev = jnp.maximum(tiles - 1, 0)
    need_a = (tiles < n_used) & ((tiles == 0) | (ea != ea[prev]))
    need_b = (tiles < n_used) & ((tiles == 0) | (eb != eb[prev]))
    first_load = jnp.where(need_a, ea, jnp.where(need_b, eb, -1))
    big = jnp.int32(n_tiles)
    cand = jnp.where(need_a | need_b, tiles, big)
    next_ge = lax.cummin(cand[::-1])[::-1]
    next_gt = jnp.concatenate([next_ge[1:], big[None]])
    next_load = jnp.where(next_gt < big, first_load[jnp.minimum(next_gt, n_tiles - 1)], -1).astype(jnp.int32)
    w_a = jnp.where(slot_a_tab[cls] == e_lo, route[2], route[3])
    w_b = jnp.where(slot_a_tab[cls] == e_lo, route[3], route[2])
    rows = jnp.stack([jnp.arange(n_tok, dtype=jnp.int32).astype(F32), w_a, w_b], axis=1)
    slot_rows = jnp.zeros((n_tiles * MOE_ROWS, 3), F32).at[pos].set(rows)
    slot_tok = slot_rows[:, 0].astype(jnp.int32)
    slot_w = slot_rows[:, 1:3]
    return (pos, slot_tok, slot_w, ea.astype(jnp.int32), eb.astype(jnp.int32), next_load,
            n_used.reshape(1).astype(jnp.int32))


def _row_copy(src_hbm, row, dst, r, sem):
    return pltpu.make_async_copy(src_hbm.at[pl.ds(row, 1), :], dst.at[pl.ds(r, 1), :], sem)


def _wait_rows(src_hbm, dst, sem):
    pltpu.make_async_copy(src_hbm.at[pl.ds(0, dst.shape[0]), :], dst, sem).wait()


def _ffn_kernel(ea_ref, eb_ref, nl_ref, nused_ref, stok_ref, h_hbm, sw_ref, wg_hbm, wu_hbm, wd_hbm, o_ref,
                xbuf, wgs, wus, wds, wga, wua, wda, wgb, wub, wdb, sem, wsem, *, layer):
    t = pl.program_id(0)
    slot = t % 2
    n_used = nused_ref[0]
    prev = jnp.maximum(t - 1, 0)
    need_a = (t < n_used) & ((t == 0) | (ea_ref[t] != ea_ref[prev]))
    need_b = (t < n_used) & ((t == 0) | (eb_ref[t] != eb_ref[prev]))

    def weight_copies(e):
        return (pltpu.make_async_copy(wg_hbm.at[layer, e], wgs, wsem.at[0]),
                pltpu.make_async_copy(wu_hbm.at[layer, e], wus, wsem.at[1]),
                pltpu.make_async_copy(wd_hbm.at[layer, e], wds, wsem.at[2]))

    def start_stage(e):
        for cp in weight_copies(e):
            cp.start()

    def take_staged(e, wg, wu, wd):
        for cp in weight_copies(e):
            cp.wait()
        wg[...] = wgs[...].astype(BF16)
        wu[...] = wus[...].astype(BF16)
        wd[...] = wds[...].astype(BF16)

    @pl.when(t == 0)
    def _prime():
        start_stage(ea_ref[0])
        for r in range(MOE_ROWS):
            _row_copy(h_hbm, stok_ref[r], xbuf.at[0], r, sem.at[0]).start()

    @pl.when(need_a)
    def _load_a():
        take_staged(ea_ref[t], wga, wua, wda)

        @pl.when(need_b)
        def _then_b():
            start_stage(eb_ref[t])

        @pl.when(jnp.logical_not(need_b) & (nl_ref[t] >= 0))
        def _then_next():
            start_stage(nl_ref[t])

    @pl.when(need_b)
    def _load_b():
        take_staged(eb_ref[t], wgb, wub, wdb)

        @pl.when(nl_ref[t] >= 0)
        def _then_next():
            start_stage(nl_ref[t])

    @pl.when(t <= n_used)
    def _wait():
        _wait_rows(h_hbm, xbuf.at[slot], sem.at[slot])

    @pl.when(t < n_used)
    def _compute():
        for r in range(MOE_ROWS):
            _row_copy(h_hbm, stok_ref[(t + 1) * MOE_ROWS + r], xbuf.at[1 - slot], r, sem.at[1 - slot]).start()
        xb = xbuf[slot].astype(BF16)
        sw = sw_ref[...]

        def expert(wg, wu, wd, gate):
            hg = jnp.dot(xb, wg[...], preferred_element_type=F32)
            hu = jnp.dot(xb, wu[...], preferred_element_type=F32)
            act = (_silu(hg) * hu * gate).astype(BF16)
            return jnp.dot(act, wd[...], preferred_element_type=F32)

        o_ref[...] = expert(wga, wua, wda, sw[:, 0:1]) + expert(wgb, wub, wdb, sw[:, 1:2])

    @pl.when(t >= n_used)
    def _idle():
        o_ref[...] = jnp.zeros_like(o_ref)


def _moe_ffn(n_tok, h2, slot_tok, slot_w, ea, eb, next_load, n_used, layer, wg, wu, wd):
    _, _, d, f = wg.shape
    n_tiles = _moe_tiles(n_tok)
    any_spec = pl.BlockSpec(memory_space=pl.ANY)
    wbufs = lambda dt: [pltpu.VMEM((d, f), dt), pltpu.VMEM((d, f), dt), pltpu.VMEM((f, d), dt)]
    grid_spec = pltpu.PrefetchScalarGridSpec(
        num_scalar_prefetch=5,
        grid=(n_tiles,),
        in_specs=[any_spec, pl.BlockSpec((MOE_ROWS, 2), lambda t, *_: (t, 0)), any_spec, any_spec, any_spec],
        out_specs=pl.BlockSpec((MOE_ROWS, d), lambda t, *_: (t, 0)),
        scratch_shapes=[pltpu.VMEM((2, MOE_ROWS, d), F32)] + wbufs(F32) + wbufs(BF16) + wbufs(BF16) + [
            pltpu.SemaphoreType.DMA((2,)),
            pltpu.SemaphoreType.DMA((3,)),
        ],
    )
    return pl.pallas_call(
        functools.partial(_ffn_kernel, layer=layer),
        grid_spec=grid_spec,
        out_shape=jax.ShapeDtypeStruct((n_tiles * MOE_ROWS, d), F32),
        compiler_params=_cparams(("arbitrary",)),
        name="moe_ffn",
    )(ea, eb, next_load, n_used, slot_tok, h2, slot_w, wg, wu, wd)


COMBINE_TM = 256


def _combine_kernel(p_ref, x_ref, gate_ref, fg_ref, y_hbm, o_ref, ybuf, sem, *, final):
    i = pl.program_id(0)
    n_i = pl.num_programs(0)
    slot = i % 2
    tm = COMBINE_TM

    def gather(tile, dst, dsem):
        def body(r, carry):
            _row_copy(y_hbm, p_ref[tile * tm + r], dst, r, dsem).start()
            return carry
        lax.fori_loop(0, tm, body, 0, unroll=8)

    @pl.when(i == 0)
    def _prime():
        gather(0, ybuf.at[0], sem.at[0])

    @pl.when(i + 1 < n_i)
    def _prefetch():
        gather(i + 1, ybuf.at[1 - slot], sem.at[1 - slot])

    _wait_rows(y_hbm, ybuf.at[slot], sem.at[slot])

    def body(g, carry):
        r0 = pl.multiple_of(g * 8, 8)
        x = x_ref[pl.ds(r0, 8), :] + gate_ref[...] * ybuf[slot, pl.ds(r0, 8), :]
        if final:
            ms = jnp.mean(x * x, axis=-1, keepdims=True)
            x = x * lax.rsqrt(ms + NORM_EPS) * fg_ref[...]
        o_ref[pl.ds(r0, 8), :] = x
        return carry
    lax.fori_loop(0, tm // 8, body, 0, unroll=4)


def _moe_combine(tl, xall, y_sorted, pos, modtab, layer, final_g, final):
    d = tl.d
    tm = COMBINE_TM
    per_tile = TM // tm
    skip_steps = tl.n_ctx // tm if final else 0
    grid_spec = pltpu.PrefetchScalarGridSpec(
        num_scalar_prefetch=1,
        grid=(tl.n_tok // tm,),
        in_specs=[
            pl.BlockSpec((tm, d), lambda i, p: (i, 0)),
            tl.mod_spec(layer, 5, tile_of=lambda i, p: i // per_tile),
            pl.BlockSpec((1, d), lambda i, p: (0, 0)),
            pl.BlockSpec(memory_space=pl.ANY),
        ],
        out_specs=pl.BlockSpec((tm, d), lambda i, p: (jnp.maximum(i - skip_steps, 0), 0)),
        scratch_shapes=[
            pltpu.VMEM((2, tm, d), F32),
            pltpu.SemaphoreType.DMA((2,)),
        ],
    )
    return pl.pallas_call(
        functools.partial(_combine_kernel, final=final),
        grid_spec=grid_spec,
        out_shape=jax.ShapeDtypeStruct((tl.n_tok - skip_steps * tm, d), F32),
        compiler_params=_cparams(("arbitrary",)),
        name="moe_combine",
    )(pos, xall, modtab, final_g.reshape(1, d), y_sorted)


def _moe_layer(tl, xall, modtab, layer, norm_g, router_w, router_bias, wg, wu, wd, final_g, final):
    h2, route = _router(tl, xall, modtab, layer, norm_g, router_w, router_bias)
    pos, slot_tok, slot_w, ea, eb, next_load, n_used = _dispatch(route, tl.n_tok)
    y_sorted = _moe_ffn(tl.n_tok, h2, slot_tok, slot_w, ea, eb, next_load, n_used, layer, wg, wu, wd)
    return _moe_combine(tl, xall, y_sorted, pos, modtab, layer, final_g, final)


INPROJ_TN = 1024
OUTPROJ_TN = 512
CONV_HALO = 16


def _normmod_kernel(x_ref, g_ref, sh_ref, sc_ref, wdt_ref, h_ref, dt_ref):
    h = _norm_mod(x_ref[...], g_ref[...], sh_ref[...], sc_ref[...]).astype(BF16)
    h_ref[...] = h
    dt_ref[...] = jnp.dot(h, wdt_ref[...], preferred_element_type=F32)


def _normmod(tl, xall, modtab, layer, norm_g, w_dt):
    d = tl.d
    n_dt = w_dt.shape[1]
    return pl.pallas_call(
        _normmod_kernel,
        grid=(tl.n_tiles,),
        in_specs=[pl.BlockSpec((TM, d), lambda i: (i, 0)), pl.BlockSpec((1, d), lambda i: (0, 0)),
                  tl.mod_spec(layer, 0), tl.mod_spec(layer, 1), pl.BlockSpec((d, n_dt), lambda i: (0, 0))],
        out_specs=[pl.BlockSpec((TM, d), lambda i: (i, 0)), pl.BlockSpec((TM, n_dt), lambda i: (i, 0))],
        out_shape=[jax.ShapeDtypeStruct((tl.n_tok, d), BF16), jax.ShapeDtypeStruct((tl.n_tok, n_dt), F32)],
        compiler_params=_cparams(("arbitrary",)),
        name="ssd_normmod",
    )(xall, norm_g.reshape(1, d), modtab, modtab, w_dt)


def _inproj_kernel(h_ref, w_ref, wt_ref, p_ref, wb, *, n_main_blocks):
    n = pl.program_id(0)

    @pl.when((pl.program_id(1) == 0) & (n < n_main_blocks))
    def _main_columns():
        wb[...] = w_ref[...].astype(BF16)

    @pl.when((pl.program_id(1) == 0) & (n >= n_main_blocks))
    def _tail_columns():
        wb[...] = wt_ref[...]

    p_ref[...] = jnp.dot(h_ref[...], wb[...], preferred_element_type=F32).astype(BF16)


def _inproj(tl, h, in_w_all, j, n_main, w_tail):
    d = tl.d
    tn = INPROJ_TN
    n_main_blocks = n_main // tn
    n_tail_blocks = w_tail.shape[1] // tn
    return pl.pallas_call(
        functools.partial(_inproj_kernel, n_main_blocks=n_main_blocks),
        grid=(n_main_blocks + n_tail_blocks, tl.n_tiles),
        in_specs=[
            pl.BlockSpec((TM, d), lambda n, i: (i, 0)),
            pl.BlockSpec((None, d, tn), lambda n, i: (j, 0, jnp.minimum(n, n_main_blocks - 1))),
            pl.BlockSpec((d, tn), lambda n, i: (0, jnp.maximum(n - n_main_blocks, 0))),
        ],
        out_specs=pl.BlockSpec((TM, tn), lambda n, i: (i, n)),
        out_shape=jax.ShapeDtypeStruct((tl.n_tok, n_main + w_tail.shape[1]), BF16),
        scratch_shapes=[pltpu.VMEM((d, tn), BF16)],
        compiler_params=_cparams(("arbitrary", "arbitrary")),
        name="ssd_inproj",
    )(h, in_w_all, w_tail)


def _scan_kernel(xs_ref, xsp_ref, xsn_ref, bc_ref, bcp_ref, bcn_ref, dt_ref, cw_ref, cb_ref, hp_ref,
                 dsk_ref, y_ref, ubuf, bm, cm, cmf, bmt, acst, wt, state, *, ncc, ncl):
    q = SSD_CHUNK
    di = xs_ref.shape[1]
    n_grp = SSD_GROUPS
    blocks_per_group = di // n_grp // 128
    d = pl.program_id(0)
    c = pl.program_id(2)
    is_ctx = c < ncc
    fwd = d == 0
    tch = jnp.where(is_ctx, jnp.where(fwd, c, ncc - 1 - c), jnp.where(fwd, c - ncc, ncl - 1 - (c - ncc)))
    first = tch == 0
    last = tch == jnp.where(is_ctx, ncc - 1, ncl - 1)

    @pl.when(c == 0)
    def _reset():
        state[...] = jnp.zeros_like(state)

    hl = CONV_HALO
    ubuf[0:hl, 0:di] = jnp.where(first, 0.0, xsp_ref[...].astype(F32))
    ubuf[hl:hl + q, 0:di] = xs_ref[...].astype(F32)
    ubuf[hl + q:hl + q + hl, 0:di] = jnp.where(last, 0.0, xsn_ref[...].astype(F32))
    ubuf[0:hl, di:] = jnp.where(first, 0.0, bcp_ref[...].astype(F32))
    ubuf[hl:hl + q, di:] = bc_ref[...].astype(F32)
    ubuf[hl + q:hl + q + hl, di:] = jnp.where(last, 0.0, bcn_ref[...].astype(F32))

    def conv_block(col):
        acc = cb_ref[:, pl.ds(col, 128)]
        for k in range(SSD_CONV):
            off = hl + k - SSD_CONV // 2
            acc = acc + cw_ref[k:k + 1, pl.ds(col, 128)] * ubuf[off:off + q, pl.ds(col, 128)]
        return _silu(acc)

    hp = hp_ref[...]
    a_coef = -jnp.exp(hp[0:1])
    raw = dt_ref[...] + hp[1:2]
    dt = jnp.maximum(raw, 0.0) + jnp.log(1.0 + jnp.exp(-jnp.abs(raw)))
    a = dt * a_coef
    li = lax.broadcasted_iota(jnp.int32, (q, q), 0)
    si = lax.broadcasted_iota(jnp.int32, (q, q), 1)
    tri = jnp.where(fwd, li - si, si - li) >= 0
    trib = jnp.where(tri, 1.0, 0.0).astype(BF16)
    a_hi = a.astype(BF16)
    r1 = a - a_hi.astype(F32)
    a_mid = r1.astype(BF16)
    a_lo = (r1 - a_mid.astype(F32)).astype(BF16)
    acs = (jnp.dot(trib, a_hi, preferred_element_type=F32) + jnp.dot(trib, a_mid, preferred_element_type=F32)
           + jnp.dot(trib, a_lo, preferred_element_type=F32))
    tot = jnp.sum(a, axis=0, keepdims=True)
    etot = jnp.exp(tot)
    acst[...] = (acs - jnp.log(dt)).T
    wt[...] = (dt * jnp.exp(tot - acs)).T

    eye = jnp.where(li == si, 1.0, 0.0).astype(BF16)

    def b_body(j, carry):
        col = pl.multiple_of(j * 128, 128)
        vb = conv_block(di + col).astype(BF16)
        bm[:, pl.ds(col, 128)] = vb
        bmt[j] = lax.dot_general(eye, vb, (((1,), (1,)), ((), ())), preferred_element_type=F32)
        return carry
    lax.fori_loop(0, n_grp, b_body, 0, unroll=2)

    def c_body(j, carry):
        col = pl.multiple_of(j * 128, 128)
        v = conv_block(di + n_grp * SSD_STATE + col)
        cmf[:, pl.ds(col, 128)] = v
        cm[:, pl.ds(col, 128)] = v.astype(BF16)
        return carry
    lax.fori_loop(0, n_grp, c_body, 0, unroll=2)

    lane = lax.broadcasted_iota(jnp.int32, (q, 128), 1)
    left = lane < SSD_HEAD_DIM
    left_row = left[0:1, :]
    for pb in range(di // 128):
        grp = pb // blocks_per_group
        gl = slice(grp * SSD_STATE, (grp + 1) * SSD_STATE)
        pc = slice(pb * 128, (pb + 1) * 128)
        sc = slice((pb % blocks_per_group) * 128, (pb % blocks_per_group + 1) * 128)
        if pb % blocks_per_group == 0:
            cbm = lax.dot_general(cm[:, gl], bm[:, gl], (((1,), (1,)), ((), ())), preferred_element_type=F32)
        h0, h1 = 2 * pb, 2 * pb + 1
        ms, cs, bts = [], [], []
        for h in (h0, h1):
            colb = jnp.broadcast_to(acs[:, h:h + 1], (q, q))
            arg = jnp.where(tri, colb - acst[h:h + 1, :], -jnp.inf)
            ms.append((cbm * jnp.exp(arg)).astype(BF16))
            cs.append((cmf[:, gl] * jnp.exp(colb)).astype(BF16))
            bts.append((bmt[grp] * wt[h:h + 1, :]).astype(BF16))
        xf = conv_block(pb * 128)
        xp = xf.astype(BF16)
        zero = jnp.zeros_like(xp)
        xe, xo = jnp.where(left, xp, zero), jnp.where(left, zero, xp)
        s_old = state[grp, :, sc]
        sb = s_old.astype(BF16)
        se, so = jnp.where(left, sb, zero), jnp.where(left, zero, sb)
        lhs = jnp.concatenate(ms + cs, axis=1)
        rhs = jnp.concatenate([xe, xo, se, so], axis=0)
        y = jnp.dot(lhs, rhs, preferred_element_type=F32) + xf * dsk_ref[:, pc]
        y_ref[:, pc] = y.astype(BF16)
        et = jnp.where(left_row, etot[:, h0:h0 + 1], etot[:, h1:h1 + 1])
        upd = jnp.dot(jnp.concatenate(bts, axis=1), jnp.concatenate([xe, xo], axis=0), preferred_element_type=F32)
        state[grp, :, sc] = s_old * et + upd


def _ssd_scan(tl, proj, dt2, conv_w, conv_b, head_params, dskip_lanes):
    n_tok = tl.n_tok
    q = SSD_CHUNK
    n_heads = dt2.shape[-1]
    di = n_heads * SSD_HEAD_DIM
    gn = SSD_GROUPS * SSD_STATE
    conv_ch = di + 2 * gn
    assert (2 * di) % (2 * gn) == 0 and (di // SSD_GROUPS) % 128 == 0
    ncc, ncl = tl.ctx_len // q, tl.seq // q
    bc_block0 = 2 * di // (2 * gn)
    hpc = q // CONV_HALO
    last_halo = n_tok // CONV_HALO - 1

    def blk(d, b, c):
        fwd = d == 0
        is_ctx = c < ncc
        t_ctx = jnp.where(fwd, c, ncc - 1 - c)
        t_lat = jnp.where(fwd, c - ncc, ncl - 1 - (c - ncc))
        return jnp.where(is_ctx, b * ncc + t_ctx, tl.bsz * ncc + b * ncl + t_lat)

    prev = lambda d, b, c: jnp.maximum(blk(d, b, c) * hpc - 1, 0)
    nxt = lambda d, b, c: jnp.minimum((blk(d, b, c) + 1) * hpc, last_halo)
    kern = functools.partial(_scan_kernel, ncc=ncc, ncl=ncl)
    return pl.pallas_call(
        kern,
        grid=(2, tl.bsz, ncc + ncl),
        in_specs=[
            pl.BlockSpec((q, di), lambda d, b, c: (blk(d, b, c), 1)),
            pl.BlockSpec((CONV_HALO, di), lambda d, b, c: (prev(d, b, c), 1)),
            pl.BlockSpec((CONV_HALO, di), lambda d, b, c: (nxt(d, b, c), 1)),
            pl.BlockSpec((q, 2 * gn), lambda d, b, c: (blk(d, b, c), bc_block0 + d)),
            pl.BlockSpec((CONV_HALO, 2 * gn), lambda d, b, c: (prev(d, b, c), bc_block0 + d)),
            pl.BlockSpec((CONV_HALO, 2 * gn), lambda d, b, c: (nxt(d, b, c), bc_block0 + d)),
            pl.BlockSpec((None, q, n_heads), lambda d, b, c: (d, blk(d, b, c), 0)),
            pl.BlockSpec((None, SSD_CONV, conv_ch), lambda d, b, c: (d, 0, 0)),
            pl.BlockSpec((None, 1, conv_ch), lambda d, b, c: (d, 0, 0)),
            pl.BlockSpec((None, 8, n_heads), lambda d, b, c: (d, 0, 0)),
            pl.BlockSpec((None, 1, di), lambda d, b, c: (d, 0, 0)),
        ],
        out_specs=pl.BlockSpec((None, q, di), lambda d, b, c: (d, blk(d, b, c), 0)),
        out_shape=jax.ShapeDtypeStruct((2, n_tok, di), BF16),
        scratch_shapes=[
            pltpu.VMEM((q + 2 * CONV_HALO, conv_ch), F32),
            pltpu.VMEM((q, gn), BF16),
            pltpu.VMEM((q, gn), BF16),
            pltpu.VMEM((q, gn), F32),
            pltpu.VMEM((SSD_GROUPS, SSD_STATE, q), F32),
            pltpu.VMEM((n_heads, q), F32),
            pltpu.VMEM((n_heads, q), F32),
            pltpu.VMEM((SSD_GROUPS, SSD_STATE, di // SSD_GROUPS), F32),
        ],
        compiler_params=_cparams(("arbitrary", "arbitrary", "arbitrary")),
        name="ssd_scan",
    )(proj, proj, proj, proj, proj, proj, dt2, conv_w, conv_b.reshape(2, 1, conv_ch), head_params, dskip_lanes)


def _outproj_kernel(y0_ref, y1_ref, z_ref, ng_ref, w_ref, x_ref, gate_ref, o_ref, ybuf):
    @pl.when(pl.program_id(1) == 0)
    def _prologue():
        gw = ybuf.shape[1] // SSD_GROUPS
        for grp in range(SSD_GROUPS):
            cols = slice(grp * gw, (grp + 1) * gw)
            y = (y0_ref[:, cols].astype(F32) + y1_ref[:, cols].astype(F32)) * _silu(z_ref[:, cols].astype(F32))
            ms = jnp.mean(y * y, axis=-1, keepdims=True)
            ybuf[:, cols] = (y * lax.rsqrt(ms + NORM_EPS) * ng_ref[:, cols]).astype(BF16)

    o_ref[...] = x_ref[...] + gate_ref[...] * jnp.dot(ybuf[...], w_ref[...], preferred_element_type=F32)


def _outproj(tl, xall, modtab, layer, y2, proj, norm_g, out_w):
    d = tl.d
    di = out_w.shape[0]
    tn = OUTPROJ_TN
    gate_spec = pl.BlockSpec(
        (None, 1, tn), lambda i, j: ((layer * COND_ROWS + tl.seg(i)) * N_MOD + 2, 0, j))
    return pl.pallas_call(
        _outproj_kernel,
        grid=(tl.n_tiles, d // tn),
        in_specs=[
            pl.BlockSpec((None, TM, di), lambda i, j: (0, i, 0)),
            pl.BlockSpec((None, TM, di), lambda i, j: (1, i, 0)),
            pl.BlockSpec((TM, di), lambda i, j: (i, 0)),
            pl.BlockSpec((1, di), lambda i, j: (0, 0)),
            pl.BlockSpec((di, tn), lambda i, j: (0, j)),
            pl.BlockSpec((TM, tn), lambda i, j: (i, j)),
            gate_spec,
        ],
        out_specs=pl.BlockSpec((TM, tn), lambda i, j: (i, j)),
        out_shape=jax.ShapeDtypeStruct((tl.n_tok, d), F32),
        scratch_shapes=[pltpu.VMEM((TM, di), BF16)],
        compiler_params=_cparams(("arbitrary", "arbitrary")),
        name="ssd_outproj",
    )(y2, y2, proj, norm_g.reshape(1, di), out_w, xall, modtab)


def _ssd_layer(tl, xall, modtab, layer, norm_g, in_w_all, j, conv_w, conv_b, a_log, dt_bias, d_skip, ssd_norm_g,
               out_w):
    n_heads = a_log.shape[-1]
    di = n_heads * SSD_HEAD_DIM
    gn = SSD_GROUPS * SSD_STATE
    dir_cols = 2 * gn + n_heads
    base0 = 2 * di
    base1 = base0 + dir_cols
    n_main = base0 + 2 * gn
    in_w = in_w_all[j]
    w_tail = in_w[:, base1:base1 + 2 * gn].astype(BF16)
    w_dt = jnp.concatenate(
        [in_w[:, base0 + 2 * gn:base0 + dir_cols], in_w[:, base1 + 2 * gn:base1 + dir_cols]], axis=1).astype(BF16)
    h, dt_raw = _normmod(tl, xall, modtab, layer, norm_g, w_dt)
    proj = _inproj(tl, h, in_w_all, j, n_main, w_tail)
    dt2 = dt_raw.reshape(tl.n_tok, 2, n_heads).transpose(1, 0, 2)
    head_params = jnp.zeros((2, 8, n_heads), F32).at[:, 0].set(a_log).at[:, 1].set(dt_bias)
    dskip_lanes = jnp.repeat(d_skip, SSD_HEAD_DIM, axis=-1).reshape(2, 1, di)
    y2 = _ssd_scan(tl, proj, dt2, conv_w, conv_b, head_params, dskip_lanes)
    return _outproj(tl, xall, modtab, layer, y2, proj, ssd_norm_g, out_w.astype(BF16))


def kernel(x, c, ctx, c_ctx, mod_w, mod_b, norm_mix_g, norm_ffn_g, pool_w, pool_b, pool_scale, ssd_in_w,
           ssd_conv_w, ssd_conv_b, ssd_a_log, ssd_dt_bias, ssd_d, ssd_norm_g, ssd_out_w, router_w, router_bias,
           moe_w_gate, moe_w_up, moe_w_down, final_norm_g):
    bsz, seq, d = x.shape
    ctx_len = ctx.shape[1]
    depth = mod_w.shape[0]
    assert bsz + 1 <= COND_ROWS
    tl = _Tiles(bsz, ctx_len, seq, d)
    cond = jnp.concatenate([c_ctx[None, :], c, jnp.zeros((COND_ROWS - 1 - bsz, d), F32)], axis=0)
    modtab = _adaln_table(cond, mod_w, mod_b)
    xall = jnp.concatenate([ctx.reshape(bsz * ctx_len, d), x.reshape(bsz * seq, d)], axis=0)
    for i in range(depth):
        j = i // 2
        if i % 2 == 0:
            xall = _pool_layer(tl, xall, modtab, i, norm_mix_g[i], pool_w[j], pool_b[j], pool_scale[j])
        else:
            xall = _ssd_layer(tl, xall, modtab, i, norm_mix_g[i], ssd_in_w, j, ssd_conv_w[j], ssd_conv_b[j],
                              ssd_a_log[j], ssd_dt_bias[j], ssd_d[j], ssd_norm_g[j], ssd_out_w[j])
        xall = _moe_layer(tl, xall, modtab, i, norm_ffn_g[i], router_w, router_bias,
                          moe_w_gate, moe_w_up, moe_w_down, final_norm_g, i == depth - 1)
    return xall.reshape(bsz, seq, d)
```

```python
import functools

import jax
import jax.numpy as jnp
from jax import lax
from jax.experimental import pallas as pl
from jax.experimental.pallas import tpu as pltpu

F32 = jnp.float32
BF16 = jnp.bfloat16

GRID_W = 64
NORM_EPS = 1e-6
N_MOD = 6
POOL_GROUPS = 4
POOL_WINDOWS_1D = (2, 4, 8, 16)
POOL_WINDOWS_2D = ((1, 2), (2, 2), (2, 4), (4, 4))
SSD_HEAD_DIM = 64
SSD_GROUPS = 8
SSD_STATE = 128
SSD_CONV = 4
SSD_CHUNK = 128
N_EXPERTS = 16
N_EXPERT_GROUPS = 4
EXPERTS_PER_GROUP = 4
TOP_K = 2

TM = 512
MOE_ROWS = 128
COND_ROWS = 8
VMEM_LIMIT = 56 * 1024 * 1024


def _silu(v):
    return v / (1.0 + jnp.exp(-v))


def _norm_mod(x, g, shift, scale):
    ms = jnp.mean(x * x, axis=-1, keepdims=True)
    y = x * lax.rsqrt(ms + NORM_EPS) * g
    return y * (1.0 + scale) + shift


def _cparams(sem):
    return pltpu.CompilerParams(dimension_semantics=sem, vmem_limit_bytes=VMEM_LIMIT)


def _adaln_kernel(cond_ref, w_ref, b_ref, o_ref):
    s = _silu(cond_ref[...]).astype(BF16)
    o_ref[...] = jnp.dot(s, w_ref[...].astype(BF16), preferred_element_type=F32) + b_ref[...]


def _adaln_table(cond, mod_w, mod_b):
    depth, d, nd = mod_w.shape
    tn = 1024 if nd % 1024 == 0 else nd
    out = pl.pallas_call(
        _adaln_kernel,
        grid=(depth, nd // tn),
        in_specs=[
            pl.BlockSpec((COND_ROWS, d), lambda l, j: (0, 0)),
            pl.BlockSpec((None, d, tn), lambda l, j: (l, 0, j)),
            pl.BlockSpec((None, 1, tn), lambda l, j: (l, 0, j)),
        ],
        out_specs=pl.BlockSpec((None, COND_ROWS, tn), lambda l, j: (l, 0, j)),
        out_shape=jax.ShapeDtypeStruct((depth, COND_ROWS, nd), F32),
        compiler_params=_cparams(("arbitrary", "arbitrary")),
        name="adaln",
    )(cond, mod_w, mod_b.reshape(depth, 1, nd))
    return out.reshape(depth * COND_ROWS * N_MOD, 1, d)


class _Tiles:
    def __init__(self, bsz, ctx_len, seq, d):
        assert bsz * ctx_len == TM and seq % TM == 0
        self.bsz, self.ctx_len, self.seq, self.d = bsz, ctx_len, seq, d
        self.n_ctx = bsz * ctx_len
        self.n_tok = self.n_ctx + bsz * seq
        self.n_tiles = self.n_tok // TM
        self.tiles_per_batch = seq // TM

    def seg(self, i):
        return jnp.where(i == 0, 0, 1 + (i - 1) // self.tiles_per_batch)

    def mod_spec(self, layer, k, tile_of=lambda *idx: idx[0]):
        def imap(*idx):
            return ((layer * COND_ROWS + self.seg(tile_of(*idx))) * N_MOD + k, 0, 0)
        return pl.BlockSpec((None, 1, self.d), imap)


POOL_MARGIN = 8
POOL_PREV = 2 * GRID_W
POOL_NEXT = GRID_W


def _pool_kernel(xp_ref, xc_ref, xn_ref, g_ref, sh_ref, sc_ref, gate_ref, w_ref, b_ref, ps_ref,
                 o_ref, hbuf, cpbuf, dbuf, *, ctx_len, tiles_per_batch, rows_per_batch):
    i = pl.program_id(0)
    gd = w_ref.shape[-1]
    g, sh, sc = g_ref[...], sh_ref[...], sc_ref[...]
    base = POOL_MARGIN + POOL_PREV
    zeros_m = jnp.zeros((POOL_MARGIN, hbuf.shape[1]), F32)
    hbuf[0:POOL_MARGIN, :] = zeros_m
    hbuf[base + TM + POOL_NEXT:base + TM + POOL_NEXT + POOL_MARGIN, :] = zeros_m
    hbuf[POOL_MARGIN:base, :] = _norm_mod(xp_ref[...], g, sh, sc)
    hbuf[base:base + TM, :] = _norm_mod(xc_ref[...], g, sh, sc)
    hbuf[base + TM:base + TM + POOL_NEXT, :] = _norm_mod(xn_ref[...], g, sh, sc)

    def finish(grp):
        cols = slice(grp * gd, (grp + 1) * gd)
        y = jnp.dot(dbuf[...], w_ref[grp].astype(BF16), preferred_element_type=F32)
        y = (y + b_ref[:, cols]) * ps_ref[:, cols]
        o_ref[:, cols] = xc_ref[:, cols] + gate_ref[:, cols] * y

    @pl.when(i == 0)
    def _ctx():
        ch = GRID_W
        for grp in range(POOL_GROUPS):
            cols = slice(grp * gd, (grp + 1) * gd)
            w = POOL_WINDOWS_1D[grp]
            lo, hi = w // 2, w - 1 - w // 2
            for k in range(TM // ch):
                pos = (k * ch) % ctx_len + lax.broadcasted_iota(jnp.int32, (ch, 1), 0)
                acc = jnp.zeros((ch, gd), F32)
                for dd in range(-lo, hi + 1):
                    v = hbuf[base + k * ch + dd:base + k * ch + dd + ch, cols]
                    ok = (pos + dd >= 0) & (pos + dd < ctx_len)
                    acc = acc + jnp.where(ok, v, 0.0)
                cnt = jnp.minimum(pos + hi, ctx_len - 1) - jnp.maximum(pos - lo, 0) + 1
                t = acc / cnt.astype(F32)
                hcur = hbuf[base + k * ch:base + (k + 1) * ch, cols]
                dbuf[k * ch:(k + 1) * ch, :] = (t - hcur).astype(BF16)
            finish(grp)

    @pl.when(i > 0)
    def _latent():
        tile_in_batch = (i - 1) % tiles_per_batch
        row0 = tile_in_batch * (TM // GRID_W)
        col = lax.broadcasted_iota(jnp.int32, (GRID_W, 1), 0)
        n_rows = TM // GRID_W
        for grp in range(POOL_GROUPS):
            cols = slice(grp * gd, (grp + 1) * gd)
            wr, wc = POOL_WINDOWS_2D[grp]
            lo_r, hi_r = wr // 2, wr - 1 - wr // 2
            lo_c, hi_c = wc // 2, wc - 1 - wc // 2
            cnt_c = (jnp.minimum(col + hi_c, GRID_W - 1) - jnp.maximum(col - lo_c, 0) + 1).astype(F32)
            for rr in range(2 - lo_r, 2 + n_rows + hi_r):
                start = POOL_MARGIN + rr * GRID_W
                acc = jnp.zeros((GRID_W, gd), F32)
                for dc in range(-lo_c, hi_c + 1):
                    v = hbuf[start + dc:start + dc + GRID_W, cols]
                    ok = (col + dc >= 0) & (col + dc < GRID_W)
                    acc = acc + jnp.where(ok, v, 0.0)
                cpbuf[rr * GRID_W:(rr + 1) * GRID_W, :] = acc / cnt_c
            for r in range(n_rows):
                acc = jnp.zeros((GRID_W, gd), F32)
                cnt_r = jnp.zeros((GRID_W, 1), F32)
                for dr in range(-lo_r, hi_r + 1):
                    grow = row0 + r + dr
                    ok = (grow >= 0) & (grow < rows_per_batch)
                    v = cpbuf[(2 + r + dr) * GRID_W:(3 + r + dr) * GRID_W, :]
                    acc = acc + jnp.where(ok, v, 0.0)
                    cnt_r = cnt_r + jnp.where(ok, 1.0, 0.0)
                hcur = hbuf[base + r * GRID_W:base + (r + 1) * GRID_W, cols]
                dbuf[r * GRID_W:(r + 1) * GRID_W, :] = (acc / cnt_r - hcur).astype(BF16)
            finish(grp)


def _pool_layer(tl, xall, modtab, layer, norm_g, pool_w, pool_b, pool_scale):
    d = tl.d
    gd = d // POOL_GROUPS
    n_prev_blocks = TM // POOL_PREV
    n_next_blocks = TM // POOL_NEXT
    last_next = tl.n_tok // POOL_NEXT - 1
    kern = functools.partial(_pool_kernel, ctx_len=tl.ctx_len, tiles_per_batch=tl.tiles_per_batch,
                             rows_per_batch=tl.seq // GRID_W)
    vec = pl.BlockSpec((1, d), lambda i: (0, 0))
    buf_rows = 2 * POOL_MARGIN + POOL_PREV + TM + POOL_NEXT
    return pl.pallas_call(
        kern,
        grid=(tl.n_tiles,),
        in_specs=[
            pl.BlockSpec((POOL_PREV, d), lambda i: (jnp.maximum(i * n_prev_blocks - 1, 0), 0)),
            pl.BlockSpec((TM, d), lambda i: (i, 0)),
            pl.BlockSpec((POOL_NEXT, d), lambda i: (jnp.minimum((i + 1) * n_next_blocks, last_next), 0)),
            vec,
            tl.mod_spec(layer, 0), tl.mod_spec(layer, 1), tl.mod_spec(layer, 2),
            pl.BlockSpec((POOL_GROUPS, gd, gd), lambda i: (0, 0, 0)),
            vec, vec,
        ],
        out_specs=pl.BlockSpec((TM, d), lambda i: (i, 0)),
        out_shape=jax.ShapeDtypeStruct((tl.n_tok, d), F32),
        scratch_shapes=[
            pltpu.VMEM((buf_rows, d), F32),
            pltpu.VMEM((POOL_PREV + TM + POOL_NEXT, gd), F32),
            pltpu.VMEM((TM, gd), BF16),
        ],
        compiler_params=_cparams(("arbitrary",)),
        name="pool_layer",
    )(xall, xall, xall, norm_g.reshape(1, d), modtab, modtab, modtab, pool_w,
      pool_b.reshape(1, d), pool_scale.reshape(1, d))


def _router_kernel(x_ref, g_ref, sh_ref, sc_ref, rwt_ref, rb_ref, h_ref, r_ref):
    h = _norm_mod(x_ref[...], g_ref[...], sh_ref[...], sc_ref[...])
    h_ref[...] = h
    logits = lax.dot_general(rwt_ref[...], h, (((1,), (1,)), ((), ())),
                             precision=lax.Precision.HIGHEST, preferred_element_type=F32)
    scores = 1.0 / (1.0 + jnp.exp(-logits))
    sel = scores + rb_ref[...]
    tm = sel.shape[1]
    srow = [scores[e:e + 1, :] for e in range(N_EXPERTS)]
    vrow = [sel[e:e + 1, :] for e in range(N_EXPERTS)]
    best_g = jnp.zeros((1, tm), jnp.int32)
    best_s = None
    for grp in range(N_EXPERT_GROUPS):
        v = vrow[grp * EXPERTS_PER_GROUP:(grp + 1) * EXPERTS_PER_GROUP]
        gs = None
        for a in range(EXPERTS_PER_GROUP):
            for b in range(a + 1, EXPERTS_PER_GROUP):
                p = v[a] + v[b]
                gs = p if gs is None else jnp.maximum(gs, p)
        if best_s is None:
            best_s = gs
        else:
            upd = gs > best_s
            best_s = jnp.where(upd, gs, best_s)
            best_g = jnp.where(upd, grp, best_g)
    neg = jnp.full((1, tm), -jnp.inf, F32)
    masked = [jnp.where(best_g == (e // EXPERTS_PER_GROUP), vrow[e], neg) for e in range(N_EXPERTS)]
    m1, i1, s1 = masked[0], jnp.zeros((1, tm), jnp.int32), srow[0]
    for e in range(1, N_EXPERTS):
        upd = masked[e] > m1
        m1 = jnp.where(upd, masked[e], m1)
        i1 = jnp.where(upd, e, i1)
        s1 = jnp.where(upd, srow[e], s1)
    m2, i2, s2 = neg, jnp.full((1, tm), -1, jnp.int32), jnp.zeros((1, tm), F32)
    for e in range(N_EXPERTS):
        cand = i1 != e
        upd = cand & ((masked[e] > m2) | (i2 < 0))
        m2 = jnp.where(upd, masked[e], m2)
        i2 = jnp.where(upd, e, i2)
        s2 = jnp.where(upd, srow[e], s2)
    tot = s1 + s2
    w1, w2 = s1 / tot, s2 / tot
    first_lo = i1 < i2
    e_lo = jnp.where(first_lo, i1, i2).astype(F32)
    e_hi = jnp.where(first_lo, i2, i1).astype(F32)
    w_lo = jnp.where(first_lo, w1, w2)
    w_hi = jnp.where(first_lo, w2, w1)
    zero = jnp.zeros((1, tm), F32)
    r_ref[...] = jnp.concatenate([e_lo, e_hi, w_lo, w_hi, zero, zero, zero, zero], axis=0)


def _router(tl, xall, modtab, layer, norm_g, router_w, router_bias):
    d = tl.d
    vec = pl.BlockSpec((1, d), lambda i: (0, 0))
    return pl.pallas_call(
        _router_kernel,
        grid=(tl.n_tiles,),
        in_specs=[
            pl.BlockSpec((TM, d), lambda i: (i, 0)),
            vec, tl.mod_spec(layer, 3), tl.mod_spec(layer, 4),
            pl.BlockSpec((N_EXPERTS, d), lambda i: (0, 0)),
            pl.BlockSpec((N_EXPERTS, 1), lambda i: (0, 0)),
        ],
        out_specs=[
            pl.BlockSpec((TM, d), lambda i: (i, 0)),
            pl.BlockSpec((8, TM), lambda i: (0, i)),
        ],
        out_shape=[
            jax.ShapeDtypeStruct((tl.n_tok, d), F32),
            jax.ShapeDtypeStruct((8, tl.n_tok), F32),
        ],
        compiler_params=_cparams(("arbitrary",)),
        name="moe_router",
    )(xall, norm_g.reshape(1, d), modtab, modtab, router_w.T, router_bias.reshape(N_EXPERTS, 1))


_PAIR_ORDER = ((0, 1), (0, 2), (0, 3), (1, 3), (1, 2), (2, 3))
_PAIR_SLOT_A = (0, 0, 0, 1, 1, 3)
_PAIR_SLOT_B = (1, 2, 3, 3, 2, 2)
N_PAIR_CLASSES = N_EXPERT_GROUPS * len(_PAIR_ORDER)


def _moe_tiles(n_tok):
    return -(-(n_tok + N_PAIR_CLASSES * (MOE_ROWS - 1)) // MOE_ROWS) + 1


def _dispatch(route, n_tok):
    e_lo = route[0].astype(jnp.int32)
    e_hi = route[1].astype(jnp.int32)
    grp = e_lo // EXPERTS_PER_GROUP
    pair_of = [0] * (EXPERTS_PER_GROUP * EXPERTS_PER_GROUP)
    for j, (a, b) in enumerate(_PAIR_ORDER):
        pair_of[a * EXPERTS_PER_GROUP + b] = j
    local = (e_lo % EXPERTS_PER_GROUP) * EXPERTS_PER_GROUP + e_hi % EXPERTS_PER_GROUP
    cls = grp * len(_PAIR_ORDER) + jnp.asarray(pair_of, jnp.int32)[local]
    n_pair = len(_PAIR_ORDER)
    cls_ids = jnp.arange(N_PAIR_CLASSES, dtype=jnp.int32)
    slot_a_tab = (cls_ids // n_pair) * EXPERTS_PER_GROUP + jnp.asarray(_PAIR_SLOT_A, jnp.int32)[cls_ids % n_pair]
    slot_b_tab = (cls_ids // n_pair) * EXPERTS_PER_GROUP + jnp.asarray(_PAIR_SLOT_B, jnp.int32)[cls_ids % n_pair]
    onehot = (cls[:, None] == cls_ids[None, :]).astype(jnp.int32)
    csum = jnp.cumsum(onehot, axis=0)
    rank = jnp.take_along_axis(csum, cls[:, None], axis=1)[:, 0] - 1
    cnt = csum[-1]
    ntile = (cnt + MOE_ROWS - 1) // MOE_ROWS
    tile_end = jnp.cumsum(ntile)
    tile_start = tile_end - ntile
    pos = (tile_start[cls] * MOE_ROWS + rank).astype(jnp.int32)
    n_tiles = _moe_tiles(n_tok)
    n_used = tile_end[-1]
    tiles = jnp.arange(n_tiles, dtype=jnp.int32)
    tcls = jnp.sum((jnp.minimum(tiles, n_used - 1)[:, None] >= tile_end[None, :]).astype(jnp.int32), axis=1)
    ea = slot_a_tab[tcls]
    eb = slot_b_tab[tcls]
    prev = jnp.maximum(tiles - 1, 0)
    need_a = (tiles < n_used) & ((tiles == 0) | (ea != ea[prev]))
    need_b = (tiles < n_used) & ((tiles == 0) | (eb != eb[prev]))
    first_load = jnp.where(need_a, ea, jnp.where(need_b, eb, -1))
    big = jnp.int32(n_tiles)
    cand = jnp.where(need_a | need_b, tiles, big)
    next_ge = lax.cummin(cand[::-1])[::-1]
    next_gt = jnp.concatenate([next_ge[1:], big[None]])
    next_load = jnp.where(next_gt < big, first_load[jnp.minimum(next_gt, n_tiles - 1)], -1).astype(jnp.int32)
    w_a = jnp.where(slot_a_tab[cls] == e_lo, route[2], route[3])
    w_b = jnp.where(slot_a_tab[cls] == e_lo, route[3], route[2])
    rows = jnp.stack([jnp.arange(n_tok, dtype=jnp.int32).astype(F32), w_a, w_b], axis=1)
    slot_rows = jnp.zeros((n_tiles * MOE_ROWS, 3), F32).at[pos].set(rows)
    slot_tok = slot_rows[:, 0].astype(jnp.int32)
    slot_w = slot_rows[:, 1:3]
    return (pos, slot_tok, slot_w, ea.astype(jnp.int32), eb.astype(jnp.int32), next_load,
            n_used.reshape(1).astype(jnp.int32))


def _row_copy(src_hbm, row, dst, r, sem):
    return pltpu.make_async_copy(src_hbm.at[pl.ds(row, 1), :], dst.at[pl.ds(r, 1), :], sem)


def _wait_rows(src_hbm, dst, sem):
    pltpu.make_async_copy(src_hbm.at[pl.ds(0, dst.shape[0]), :], dst, sem).wait()


def _ffn_kernel(ea_ref, eb_ref, nl_ref, nused_ref, stok_ref, h_hbm, sw_ref, wg_hbm, wu_hbm, wd_hbm, o_ref,
                xbuf, wgs, wus, wds, wga, wua, wda, wgb, wub, wdb, sem, wsem, *, layer):
    t = pl.program_id(0)
    slot = t % 2
    n_used = nused_ref[0]
    prev = jnp.maximum(t - 1, 0)
    need_a = (t < n_used) & ((t == 0) | (ea_ref[t] != ea_ref[prev]))
    need_b = (t < n_used) & ((t == 0) | (eb_ref[t] != eb_ref[prev]))

    def weight_copies(e):
        return (pltpu.make_async_copy(wg_hbm.at[layer, e], wgs, wsem.at[0]),
                pltpu.make_async_copy(wu_hbm.at[layer, e], wus, wsem.at[1]),
                pltpu.make_async_copy(wd_hbm.at[layer, e], wds, wsem.at[2]))

    def start_stage(e):
        for cp in weight_copies(e):
            cp.start()

    def take_staged(e, wg, wu, wd):
        for cp in weight_copies(e):
            cp.wait()
        wg[...] = wgs[...].astype(BF16)
        wu[...] = wus[...].astype(BF16)
        wd[...] = wds[...].astype(BF16)

    @pl.when(t == 0)
    def _prime():
        start_stage(ea_ref[0])
        for r in range(MOE_ROWS):
            _row_copy(h_hbm, stok_ref[r], xbuf.at[0], r, sem.at[0]).start()

    @pl.when(need_a)
    def _load_a():
        take_staged(ea_ref[t], wga, wua, wda)

        @pl.when(need_b)
        def _then_b():
            start_stage(eb_ref[t])

        @pl.when(jnp.logical_not(need_b) & (nl_ref[t] >= 0))
        def _then_next():
            start_stage(nl_ref[t])

    @pl.when(need_b)
    def _load_b():
        take_staged(eb_ref[t], wgb, wub, wdb)

        @pl.when(nl_ref[t] >= 0)
        def _then_next():
            start_stage(nl_ref[t])

    @pl.when(t <= n_used)
    def _wait():
        _wait_rows(h_hbm, xbuf.at[slot], sem.at[slot])

    @pl.when(t < n_used)
    def _compute():
        for r in range(MOE_ROWS):
            _row_copy(h_hbm, stok_ref[(t + 1) * MOE_ROWS + r], xbuf.at[1 - slot], r, sem.at[1 - slot]).start()
        xb = xbuf[slot].astype(BF16)
        sw = sw_ref[...]

        def expert(wg, wu, wd, gate):
            hg = jnp.dot(xb, wg[...], preferred_element_type=F32)
            hu = jnp.dot(xb, wu[...], preferred_element_type=F32)
            act = (_silu(hg) * hu * gate).astype(BF16)
            return jnp.dot(act, wd[...], preferred_element_type=F32)

        o_ref[...] = expert(wga, wua, wda, sw[:, 0:1]) + expert(wgb, wub, wdb, sw[:, 1:2])

    @pl.when(t >= n_used)
    def _idle():
        o_ref[...] = jnp.zeros_like(o_ref)


def _moe_ffn(n_tok, h2, slot_tok, slot_w, ea, eb, next_load, n_used, layer, wg, wu, wd):
    _, _, d, f = wg.shape
    n_tiles = _moe_tiles(n_tok)
    any_spec = pl.BlockSpec(memory_space=pl.ANY)
    wbufs = lambda dt: [pltpu.VMEM((d, f), dt), pltpu.VMEM((d, f), dt), pltpu.VMEM((f, d), dt)]
    grid_spec = pltpu.PrefetchScalarGridSpec(
        num_scalar_prefetch=5,
        grid=(n_tiles,),
        in_specs=[any_spec, pl.BlockSpec((MOE_ROWS, 2), lambda t, *_: (t, 0)), any_spec, any_spec, any_spec],
        out_specs=pl.BlockSpec((MOE_ROWS, d), lambda t, *_: (t, 0)),
        scratch_shapes=[pltpu.VMEM((2, MOE_ROWS, d), F32)] + wbufs(F32) + wbufs(BF16) + wbufs(BF16) + [
            pltpu.SemaphoreType.DMA((2,)),
            pltpu.SemaphoreType.DMA((3,)),
        ],
    )
    return pl.pallas_call(
        functools.partial(_ffn_kernel, layer=layer),
        grid_spec=grid_spec,
        out_shape=jax.ShapeDtypeStruct((n_tiles * MOE_ROWS, d), F32),
        compiler_params=_cparams(("arbitrary",)),
        name="moe_ffn",
    )(ea, eb, next_load, n_used, slot_tok, h2, slot_w, wg, wu, wd)


COMBINE_TM = 256


def _combine_kernel(p_ref, x_ref, gate_ref, fg_ref, y_hbm, o_ref, ybuf, sem, *, final):
    i = pl.program_id(0)
    n_i = pl.num_programs(0)
    slot = i % 2
    tm = COMBINE_TM

    def gather(tile, dst, dsem):
        def body(r, carry):
            _row_copy(y_hbm, p_ref[tile * tm + r], dst, r, dsem).start()
            return carry
        lax.fori_loop(0, tm, body, 0, unroll=8)

    @pl.when(i == 0)
    def _prime():
        gather(0, ybuf.at[0], sem.at[0])

    @pl.when(i + 1 < n_i)
    def _prefetch():
        gather(i + 1, ybuf.at[1 - slot], sem.at[1 - slot])

    _wait_rows(y_hbm, ybuf.at[slot], sem.at[slot])

    def body(g, carry):
        r0 = pl.multiple_of(g * 8, 8)
        x = x_ref[pl.ds(r0, 8), :] + gate_ref[...] * ybuf[slot, pl.ds(r0, 8), :]
        if final:
            ms = jnp.mean(x * x, axis=-1, keepdims=True)
            x = x * lax.rsqrt(ms + NORM_EPS) * fg_ref[...]
        o_ref[pl.ds(r0, 8), :] = x
        return carry
    lax.fori_loop(0, tm // 8, body, 0, unroll=4)


def _moe_combine(tl, xall, y_sorted, pos, modtab, layer, final_g, final):
    d = tl.d
    tm = COMBINE_TM
    per_tile = TM // tm
    skip_steps = tl.n_ctx // tm if final else 0
    grid_spec = pltpu.PrefetchScalarGridSpec(
        num_scalar_prefetch=1,
        grid=(tl.n_tok // tm,),
        in_specs=[
            pl.BlockSpec((tm, d), lambda i, p: (i, 0)),
            tl.mod_spec(layer, 5, tile_of=lambda i, p: i // per_tile),
            pl.BlockSpec((1, d), lambda i, p: (0, 0)),
            pl.BlockSpec(memory_space=pl.ANY),
        ],
        out_specs=pl.BlockSpec((tm, d), lambda i, p: (jnp.maximum(i - skip_steps, 0), 0)),
        scratch_shapes=[
            pltpu.VMEM((2, tm, d), F32),
            pltpu.SemaphoreType.DMA((2,)),
        ],
    )
    return pl.pallas_call(
        functools.partial(_combine_kernel, final=final),
        grid_spec=grid_spec,
        out_shape=jax.ShapeDtypeStruct((tl.n_tok - skip_steps * tm, d), F32),
        compiler_params=_cparams(("arbitrary",)),
        name="moe_combine",
    )(pos, xall, modtab, final_g.reshape(1, d), y_sorted)


def _moe_layer(tl, xall, modtab, layer, norm_g, router_w, router_bias, wg, wu, wd, final_g, final):
    h2, route = _router(tl, xall, modtab, layer, norm_g, router_w, router_bias)
    pos, slot_tok, slot_w, ea, eb, next_load, n_used = _dispatch(route, tl.n_tok)
    y_sorted = _moe_ffn(tl.n_tok, h2, slot_tok, slot_w, ea, eb, next_load, n_used, layer, wg, wu, wd)
    return _moe_combine(tl, xall, y_sorted, pos, modtab, layer, final_g, final)


INPROJ_TN = 1024
INPROJ_M_TILES = 8
OUTPROJ_TN = 512
CONV_HALO = 16


def _normmod_kernel(x_ref, g_ref, sh_ref, sc_ref, wdt_ref, h_ref, dt_ref):
    h = _norm_mod(x_ref[...], g_ref[...], sh_ref[...], sc_ref[...]).astype(BF16)
    h_ref[...] = h
    dt_ref[...] = jnp.dot(h, wdt_ref[...], preferred_element_type=F32)


def _normmod(tl, xall, modtab, layer, norm_g, w_dt):
    d = tl.d
    n_dt = w_dt.shape[1]
    return pl.pallas_call(
        _normmod_kernel,
        grid=(tl.n_tiles,),
        in_specs=[pl.BlockSpec((TM, d), lambda i: (i, 0)), pl.BlockSpec((1, d), lambda i: (0, 0)),
                  tl.mod_spec(layer, 0), tl.mod_spec(layer, 1), pl.BlockSpec((d, n_dt), lambda i: (0, 0))],
        out_specs=[pl.BlockSpec((TM, d), lambda i: (i, 0)), pl.BlockSpec((TM, n_dt), lambda i: (i, 0))],
        out_shape=[jax.ShapeDtypeStruct((tl.n_tok, d), BF16), jax.ShapeDtypeStruct((tl.n_tok, n_dt), F32)],
        compiler_params=_cparams(("arbitrary",)),
        name="ssd_normmod",
    )(xall, norm_g.reshape(1, d), modtab, modtab, w_dt)


def _inproj_kernel(h_ref, w_ref, wt_ref, p_ref, wb, *, n_main_blocks):
    n = pl.program_id(0)

    @pl.when((pl.program_id(1) == 0) & (n < n_main_blocks))
    def _main_columns():
        wb[...] = w_ref[...].astype(BF16)

    @pl.when((pl.program_id(1) == 0) & (n >= n_main_blocks))
    def _tail_columns():
        wb[...] = wt_ref[...]

    p_ref[...] = jnp.dot(h_ref[...], wb[...], preferred_element_type=F32).astype(BF16)


def _inproj(tl, h, in_w_all, j, n_main, w_tail):
    d = tl.d
    tn = INPROJ_TN
    n_main_blocks = n_main // tn
    n_tail_blocks = w_tail.shape[1] // tn
    tm = tl.n_tok // INPROJ_M_TILES if tl.n_tok % (16 * INPROJ_M_TILES) == 0 else TM
    return pl.pallas_call(
        functools.partial(_inproj_kernel, n_main_blocks=n_main_blocks),
        grid=(n_main_blocks + n_tail_blocks, tl.n_tok // tm),
        in_specs=[
            pl.BlockSpec((tm, d), lambda n, i: (i, 0)),
            pl.BlockSpec((None, d, tn), lambda n, i: (j, 0, jnp.minimum(n, n_main_blocks - 1))),
            pl.BlockSpec((d, tn), lambda n, i: (0, jnp.maximum(n - n_main_blocks, 0))),
        ],
        out_specs=pl.BlockSpec((tm, tn), lambda n, i: (i, n)),
        out_shape=jax.ShapeDtypeStruct((tl.n_tok, n_main + w_tail.shape[1]), BF16),
        scratch_shapes=[pltpu.VMEM((d, tn), BF16)],
        compiler_params=_cparams(("arbitrary", "arbitrary")),
        name="ssd_inproj",
    )(h, in_w_all, w_tail)


def _scan_kernel(xs_ref, xsp_ref, xsn_ref, bc_ref, bcp_ref, bcn_ref, dt_ref, cw_ref, cb_ref, hp_ref,
                 dsk_ref, y_ref, ubuf, bm, cm, cmf, bmt, acst, wt, state, *, ncc, ncl):
    q = SSD_CHUNK
    di = xs_ref.shape[1]
    n_grp = SSD_GROUPS
    blocks_per_group = di // n_grp // 128
    d = pl.program_id(0)
    c = pl.program_id(2)
    is_ctx = c < ncc
    fwd = d == 0
    tch = jnp.where(is_ctx, jnp.where(fwd, c, ncc - 1 - c), jnp.where(fwd, c - ncc, ncl - 1 - (c - ncc)))
    first = tch == 0
    last = tch == jnp.where(is_ctx, ncc - 1, ncl - 1)

    @pl.when(c == 0)
    def _reset():
        state[...] = jnp.zeros_like(state)

    hl = CONV_HALO
    ubuf[0:hl, 0:di] = jnp.where(first, 0.0, xsp_ref[...].astype(F32))
    ubuf[hl:hl + q, 0:di] = xs_ref[...].astype(F32)
    ubuf[hl + q:hl + q + hl, 0:di] = jnp.where(last, 0.0, xsn_ref[...].astype(F32))
    ubuf[0:hl, di:] = jnp.where(first, 0.0, bcp_ref[...].astype(F32))
    ubuf[hl:hl + q, di:] = bc_ref[...].astype(F32)
    ubuf[hl + q:hl + q + hl, di:] = jnp.where(last, 0.0, bcn_ref[...].astype(F32))

    def conv_block(col):
        acc = cb_ref[:, pl.ds(col, 128)]
        for k in range(SSD_CONV):
            off = hl + k - SSD_CONV // 2
            acc = acc + cw_ref[k:k + 1, pl.ds(col, 128)] * ubuf[off:off + q, pl.ds(col, 128)]
        return _silu(acc)

    hp = hp_ref[...]
    a_coef = -jnp.exp(hp[0:1])
    raw = dt_ref[...] + hp[1:2]
    dt = jnp.maximum(raw, 0.0) + jnp.log(1.0 + jnp.exp(-jnp.abs(raw)))
    a = dt * a_coef
    li = lax.broadcasted_iota(jnp.int32, (q, q), 0)
    si = lax.broadcasted_iota(jnp.int32, (q, q), 1)
    tri = jnp.where(fwd, li - si, si - li) >= 0
    trib = jnp.where(tri, 1.0, 0.0).astype(BF16)
    a_hi = a.astype(BF16)
    r1 = a - a_hi.astype(F32)
    a_mid = r1.astype(BF16)
    a_lo = (r1 - a_mid.astype(F32)).astype(BF16)
    acs = (jnp.dot(trib, a_hi, preferred_element_type=F32) + jnp.dot(trib, a_mid, preferred_element_type=F32)
           + jnp.dot(trib, a_lo, preferred_element_type=F32))
    tot = jnp.sum(a, axis=0, keepdims=True)
    eacs = jnp.exp(acs)
    etot = jnp.exp(tot)
    acst[...] = (acs - jnp.log(dt)).T
    wt[...] = (dt * jnp.exp(tot - acs)).T

    eye = jnp.where(li == si, 1.0, 0.0).astype(BF16)

    def b_body(j, carry):
        col = pl.multiple_of(j * 128, 128)
        vb = conv_block(di + col).astype(BF16)
        bm[:, pl.ds(col, 128)] = vb
        bmt[j] = lax.dot_general(eye, vb, (((1,), (1,)), ((), ())), preferred_element_type=F32)
        return carry
    lax.fori_loop(0, n_grp, b_body, 0, unroll=2)

    def c_body(j, carry):
        col = pl.multiple_of(j * 128, 128)
        v = conv_block(di + n_grp * SSD_STATE + col)
        cmf[:, pl.ds(col, 128)] = v
        cm[:, pl.ds(col, 128)] = v.astype(BF16)
        return carry
    lax.fori_loop(0, n_grp, c_body, 0, unroll=2)

    lane = lax.broadcasted_iota(jnp.int32, (q, 128), 1)
    left = lane < SSD_HEAD_DIM
    left_row = left[0:1, :]
    for pb in range(di // 128):
        grp = pb // blocks_per_group
        gl = slice(grp * SSD_STATE, (grp + 1) * SSD_STATE)
        pc = slice(pb * 128, (pb + 1) * 128)
        sc = slice((pb % blocks_per_group) * 128, (pb % blocks_per_group + 1) * 128)
        if pb % blocks_per_group == 0:
            cbm = lax.dot_general(cm[:, gl], bm[:, gl], (((1,), (1,)), ((), ())), preferred_element_type=F32)
        h0, h1 = 2 * pb, 2 * pb + 1
        ms, cs, bts = [], [], []
        for h in (h0, h1):
            colb = jnp.broadcast_to(acs[:, h:h + 1], (q, q))
            arg = jnp.where(tri, colb - acst[h:h + 1, :], -jnp.inf)
            ms.append((cbm * jnp.exp(arg)).astype(BF16))
            cs.append((cmf[:, gl] * jnp.broadcast_to(eacs[:, h:h + 1], (q, SSD_STATE))).astype(BF16))
            bts.append((bmt[grp] * wt[h:h + 1, :]).astype(BF16))
        xf = conv_block(pb * 128)
        xp = xf.astype(BF16)
        zero = jnp.zeros_like(xp)
        xe, xo = jnp.where(left, xp, zero), jnp.where(left, zero, xp)
        s_old = state[grp, :, sc]
        sb = s_old.astype(BF16)
        se, so = jnp.where(left, sb, zero), jnp.where(left, zero, sb)
        lhs = jnp.concatenate(ms + cs, axis=1)
        rhs = jnp.concatenate([xe, xo, se, so], axis=0)
        y = jnp.dot(lhs, rhs, preferred_element_type=F32) + xf * dsk_ref[:, pc]
        y_ref[:, pc] = y.astype(BF16)
        et = jnp.where(left_row, etot[:, h0:h0 + 1], etot[:, h1:h1 + 1])
        upd = jnp.dot(jnp.concatenate(bts, axis=1), jnp.concatenate([xe, xo], axis=0), preferred_element_type=F32)
        state[grp, :, sc] = s_old * et + upd


def _ssd_scan(tl, proj, dt2, conv_w, conv_b, head_params, dskip_lanes):
    n_tok = tl.n_tok
    q = SSD_CHUNK
    n_heads = dt2.shape[-1]
    di = n_heads * SSD_HEAD_DIM
    gn = SSD_GROUPS * SSD_STATE
    conv_ch = di + 2 * gn
    assert (2 * di) % (2 * gn) == 0 and (di // SSD_GROUPS) % 128 == 0
    ncc, ncl = tl.ctx_len // q, tl.seq // q
    bc_block0 = 2 * di // (2 * gn)
    hpc = q // CONV_HALO
    last_halo = n_tok // CONV_HALO - 1

    def blk(d, b, c):
        fwd = d == 0
        is_ctx = c < ncc
        t_ctx = jnp.where(fwd, c, ncc - 1 - c)
        t_lat = jnp.where(fwd, c - ncc, ncl - 1 - (c - ncc))
        return jnp.where(is_ctx, b * ncc + t_ctx, tl.bsz * ncc + b * ncl + t_lat)

    prev = lambda d, b, c: jnp.maximum(blk(d, b, c) * hpc - 1, 0)
    nxt = lambda d, b, c: jnp.minimum((blk(d, b, c) + 1) * hpc, last_halo)
    kern = functools.partial(_scan_kernel, ncc=ncc, ncl=ncl)
    return pl.pallas_call(
        kern,
        grid=(2, tl.bsz, ncc + ncl),
        in_specs=[
            pl.BlockSpec((q, di), lambda d, b, c: (blk(d, b, c), 1)),
            pl.BlockSpec((CONV_HALO, di), lambda d, b, c: (prev(d, b, c), 1)),
            pl.BlockSpec((CONV_HALO, di), lambda d, b, c: (nxt(d, b, c), 1)),
            pl.BlockSpec((q, 2 * gn), lambda d, b, c: (blk(d, b, c), bc_block0 + d)),
            pl.BlockSpec((CONV_HALO, 2 * gn), lambda d, b, c: (prev(d, b, c), bc_block0 + d)),
            pl.BlockSpec((CONV_HALO, 2 * gn), lambda d, b, c: (nxt(d, b, c), bc_block0 + d)),
            pl.BlockSpec((None, q, n_heads), lambda d, b, c: (d, blk(d, b, c), 0)),
            pl.BlockSpec((None, SSD_CONV, conv_ch), lambda d, b, c: (d, 0, 0)),
            pl.BlockSpec((None, 1, conv_ch), lambda d, b, c: (d, 0, 0)),
            pl.BlockSpec((None, 8, n_heads), lambda d, b, c: (d, 0, 0)),
            pl.BlockSpec((None, 1, di), lambda d, b, c: (d, 0, 0)),
        ],
        out_specs=pl.BlockSpec((None, q, di), lambda d, b, c: (d, blk(d, b, c), 0)),
        out_shape=jax.ShapeDtypeStruct((2, n_tok, di), BF16),
        scratch_shapes=[
            pltpu.VMEM((q + 2 * CONV_HALO, conv_ch), F32),
            pltpu.VMEM((q, gn), BF16),
            pltpu.VMEM((q, gn), BF16),
            pltpu.VMEM((q, gn), F32),
            pltpu.VMEM((SSD_GROUPS, SSD_STATE, q), F32),
            pltpu.VMEM((n_heads, q), F32),
            pltpu.VMEM((n_heads, q), F32),
            pltpu.VMEM((SSD_GROUPS, SSD_STATE, di // SSD_GROUPS), F32),
        ],
        compiler_params=_cparams(("arbitrary", "arbitrary", "arbitrary")),
        name="ssd_scan",
    )(proj, proj, proj, proj, proj, proj, dt2, conv_w, conv_b.reshape(2, 1, conv_ch), head_params, dskip_lanes)


def _outproj_kernel(y0_ref, y1_ref, z_ref, ng_ref, w_ref, x_ref, gate_ref, o_ref, ybuf):
    @pl.when(pl.program_id(1) == 0)
    def _prologue():
        gw = ybuf.shape[1] // SSD_GROUPS
        for grp in range(SSD_GROUPS):
            cols = slice(grp * gw, (grp + 1) * gw)
            y = (y0_ref[:, cols].astype(F32) + y1_ref[:, cols].astype(F32)) * _silu(z_ref[:, cols].astype(F32))
            ms = jnp.mean(y * y, axis=-1, keepdims=True)
            ybuf[:, cols] = (y * lax.rsqrt(ms + NORM_EPS) * ng_ref[:, cols]).astype(BF16)

    o_ref[...] = x_ref[...] + gate_ref[...] * jnp.dot(ybuf[...], w_ref[...], preferred_element_type=F32)


def _outproj(tl, xall, modtab, layer, y2, proj, norm_g, out_w):
    d = tl.d
    di = out_w.shape[0]
    tn = OUTPROJ_TN
    gate_spec = pl.BlockSpec(
        (None, 1, tn), lambda i, j: ((layer * COND_ROWS + tl.seg(i)) * N_MOD + 2, 0, j))
    return pl.pallas_call(
        _outproj_kernel,
        grid=(tl.n_tiles, d // tn),
        in_specs=[
            pl.BlockSpec((None, TM, di), lambda i, j: (0, i, 0)),
            pl.BlockSpec((None, TM, di), lambda i, j: (1, i, 0)),
            pl.BlockSpec((TM, di), lambda i, j: (i, 0)),
            pl.BlockSpec((1, di), lambda i, j: (0, 0)),
            pl.BlockSpec((di, tn), lambda i, j: (0, j)),
            pl.BlockSpec((TM, tn), lambda i, j: (i, j)),
            gate_spec,
        ],
        out_specs=pl.BlockSpec((TM, tn), lambda i, j: (i, j)),
        out_shape=jax.ShapeDtypeStruct((tl.n_tok, d), F32),
        scratch_shapes=[pltpu.VMEM((TM, di), BF16)],
        compiler_params=_cparams(("arbitrary", "arbitrary")),
        name="ssd_outproj",
    )(y2, y2, proj, norm_g.reshape(1, di), out_w, xall, modtab)


def _ssd_layer(tl, xall, modtab, layer, norm_g, in_w_all, j, conv_w, conv_b, a_log, dt_bias, d_skip, ssd_norm_g,
               out_w):
    n_heads = a_log.shape[-1]
    di = n_heads * SSD_HEAD_DIM
    gn = SSD_GROUPS * SSD_STATE
    dir_cols = 2 * gn + n_heads
    base0 = 2 * di
    base1 = base0 + dir_cols
    n_main = base0 + 2 * gn
    in_w = in_w_all[j]
    w_tail = in_w[:, base1:base1 + 2 * gn].astype(BF16)
    w_dt = jnp.concatenate(
        [in_w[:, base0 + 2 * gn:base0 + dir_cols], in_w[:, base1 + 2 * gn:base1 + dir_cols]], axis=1).astype(BF16)
    h, dt_raw = _normmod(tl, xall, modtab, layer, norm_g, w_dt)
    proj = _inproj(tl, h, in_w_all, j, n_main, w_tail)
    dt2 = dt_raw.reshape(tl.n_tok, 2, n_heads).transpose(1, 0, 2)
    head_params = jnp.zeros((2, 8, n_heads), F32).at[:, 0].set(a_log).at[:, 1].set(dt_bias)
    dskip_lanes = jnp.repeat(d_skip, SSD_HEAD_DIM, axis=-1).reshape(2, 1, di)
    y2 = _ssd_scan(tl, proj, dt2, conv_w, conv_b, head_params, dskip_lanes)
    return _outproj(tl, xall, modtab, layer, y2, proj, ssd_norm_g, out_w.astype(BF16))


def kernel(x, c, ctx, c_ctx, mod_w, mod_b, norm_mix_g, norm_ffn_g, pool_w, pool_b, pool_scale, ssd_in_w,
           ssd_conv_w, ssd_conv_b, ssd_a_log, ssd_dt_bias, ssd_d, ssd_norm_g, ssd_out_w, router_w, router_bias,
           moe_w_gate, moe_w_up, moe_w_down, final_norm_g):
    bsz, seq, d = x.shape
    ctx_len = ctx.shape[1]
    depth = mod_w.shape[0]
    assert bsz + 1 <= COND_ROWS
    tl = _Tiles(bsz, ctx_len, seq, d)
    cond = jnp.concatenate([c_ctx[None, :], c, jnp.zeros((COND_ROWS - 1 - bsz, d), F32)], axis=0)
    modtab = _adaln_table(cond, mod_w, mod_b)
    xall = jnp.concatenate([ctx.reshape(bsz * ctx_len, d), x.reshape(bsz * seq, d)], axis=0)
    for i in range(depth):
        j = i // 2
        if i % 2 == 0:
            xall = _pool_layer(tl, xall, modtab, i, norm_mix_g[i], pool_w[j], pool_b[j], pool_scale[j])
        else:
            xall = _ssd_layer(tl, xall, modtab, i, norm_mix_g[i], ssd_in_w, j, ssd_conv_w[j], ssd_conv_b[j],
                              ssd_a_log[j], ssd_dt_bias[j], ssd_d[j], ssd_norm_g[j], ssd_out_w[j])
        xall = _moe_layer(tl, xall, modtab, i, norm_ffn_g[i], router_w, router_bias,
                          moe_w_gate, moe_w_up, moe_w_down, final_norm_g, i == depth - 1)
    return xall.reshape(bsz, seq, d)
```

```python
import functools

import jax
import jax.numpy as jnp
from jax import lax
from jax.experimental import pallas as pl
from jax.experimental.pallas import tpu as pltpu

F32 = jnp.float32
BF16 = jnp.bfloat16

GRID_W = 64
NORM_EPS = 1e-6
N_MOD = 6
POOL_GROUPS = 4
POOL_WINDOWS_1D = (2, 4, 8, 16)
POOL_WINDOWS_2D = ((1, 2), (2, 2), (2, 4), (4, 4))
SSD_HEAD_DIM = 64
SSD_GROUPS = 8
SSD_STATE = 128
SSD_CONV = 4
SSD_CHUNK = 128
N_EXPERTS = 16
N_EXPERT_GROUPS = 4
EXPERTS_PER_GROUP = 4
TOP_K = 2

TM = 512
MOE_ROWS = 128
COND_ROWS = 8
VMEM_LIMIT = 56 * 1024 * 1024


def _silu(v):
    return v / (1.0 + jnp.exp(-v))


def _norm_mod(x, g, shift, scale):
    ms = jnp.mean(x * x, axis=-1, keepdims=True)
    y = x * lax.rsqrt(ms + NORM_EPS) * g
    return y * (1.0 + scale) + shift


def _cparams(sem):
    return pltpu.CompilerParams(dimension_semantics=sem, vmem_limit_bytes=VMEM_LIMIT)


def _adaln_kernel(cond_ref, w_ref, b_ref, o_ref):
    s = _silu(cond_ref[...]).astype(BF16)
    o_ref[...] = jnp.dot(s, w_ref[...].astype(BF16), preferred_element_type=F32) + b_ref[...]


def _adaln_table(cond, mod_w, mod_b):
    depth, d, nd = mod_w.shape
    tn = 1024 if nd % 1024 == 0 else nd
    out = pl.pallas_call(
        _adaln_kernel,
        grid=(depth, nd // tn),
        in_specs=[
            pl.BlockSpec((COND_ROWS, d), lambda l, j: (0, 0)),
            pl.BlockSpec((None, d, tn), lambda l, j: (l, 0, j)),
            pl.BlockSpec((None, 1, tn), lambda l, j: (l, 0, j)),
        ],
        out_specs=pl.BlockSpec((None, COND_ROWS, tn), lambda l, j: (l, 0, j)),
        out_shape=jax.ShapeDtypeStruct((depth, COND_ROWS, nd), F32),
        compiler_params=_cparams(("arbitrary", "arbitrary")),
        name="adaln",
    )(cond, mod_w, mod_b.reshape(depth, 1, nd))
    return out.reshape(depth * COND_ROWS * N_MOD, 1, d)


class _Tiles:
    def __init__(self, bsz, ctx_len, seq, d):
        assert bsz * ctx_len == TM and seq % TM == 0
        self.bsz, self.ctx_len, self.seq, self.d = bsz, ctx_len, seq, d
        self.n_ctx = bsz * ctx_len
        self.n_tok = self.n_ctx + bsz * seq
        self.n_tiles = self.n_tok // TM
        self.tiles_per_batch = seq // TM

    def seg(self, i):
        return jnp.where(i == 0, 0, 1 + (i - 1) // self.tiles_per_batch)

    def mod_spec(self, layer, k, tile_of=lambda *idx: idx[0]):
        def imap(*idx):
            return ((layer * COND_ROWS + self.seg(tile_of(*idx))) * N_MOD + k, 0, 0)
        return pl.BlockSpec((None, 1, self.d), imap)


POOL_MARGIN = 8
POOL_PREV = 2 * GRID_W
POOL_NEXT = GRID_W


def _pool_kernel(xp_ref, xc_ref, xn_ref, g_ref, sh_ref, sc_ref, gate_ref, w_ref, b_ref, ps_ref,
                 o_ref, hbuf, cpbuf, dbuf, *, ctx_len, tiles_per_batch, rows_per_batch):
    i = pl.program_id(0)
    gd = w_ref.shape[-1]
    g, sh, sc = g_ref[...], sh_ref[...], sc_ref[...]
    base = POOL_MARGIN + POOL_PREV
    zeros_m = jnp.zeros((POOL_MARGIN, hbuf.shape[1]), F32)
    hbuf[0:POOL_MARGIN, :] = zeros_m
    hbuf[base + TM + POOL_NEXT:base + TM + POOL_NEXT + POOL_MARGIN, :] = zeros_m
    hbuf[POOL_MARGIN:base, :] = _norm_mod(xp_ref[...], g, sh, sc)
    hbuf[base:base + TM, :] = _norm_mod(xc_ref[...], g, sh, sc)
    hbuf[base + TM:base + TM + POOL_NEXT, :] = _norm_mod(xn_ref[...], g, sh, sc)

    def finish(grp):
        cols = slice(grp * gd, (grp + 1) * gd)
        y = jnp.dot(dbuf[...], w_ref[grp].astype(BF16), preferred_element_type=F32)
        y = (y + b_ref[:, cols]) * ps_ref[:, cols]
        o_ref[:, cols] = xc_ref[:, cols] + gate_ref[:, cols] * y

    @pl.when(i == 0)
    def _ctx():
        ch = GRID_W
        for grp in range(POOL_GROUPS):
            cols = slice(grp * gd, (grp + 1) * gd)
            w = POOL_WINDOWS_1D[grp]
            lo, hi = w // 2, w - 1 - w // 2
            for k in range(TM // ch):
                pos = (k * ch) % ctx_len + lax.broadcasted_iota(jnp.int32, (ch, 1), 0)
                acc = jnp.zeros((ch, gd), F32)
                for dd in range(-lo, hi + 1):
                    v = hbuf[base + k * ch + dd:base + k * ch + dd + ch, cols]
                    ok = (pos + dd >= 0) & (pos + dd < ctx_len)
                    acc = acc + jnp.where(ok, v, 0.0)
                cnt = jnp.minimum(pos + hi, ctx_len - 1) - jnp.maximum(pos - lo, 0) + 1
                t = acc / cnt.astype(F32)
                hcur = hbuf[base + k * ch:base + (k + 1) * ch, cols]
                dbuf[k * ch:(k + 1) * ch, :] = (t - hcur).astype(BF16)
            finish(grp)

    @pl.when(i > 0)
    def _latent():
        tile_in_batch = (i - 1) % tiles_per_batch
        row0 = tile_in_batch * (TM // GRID_W)
        col = lax.broadcasted_iota(jnp.int32, (GRID_W, 1), 0)
        n_rows = TM // GRID_W
        for grp in range(POOL_GROUPS):
            cols = slice(grp * gd, (grp + 1) * gd)
            wr, wc = POOL_WINDOWS_2D[grp]
            lo_r, hi_r = wr // 2, wr - 1 - wr // 2
            lo_c, hi_c = wc // 2, wc - 1 - wc // 2
            cnt_c = (jnp.minimum(col + hi_c, GRID_W - 1) - jnp.maximum(col - lo_c, 0) + 1).astype(F32)
            for rr in range(2 - lo_r, 2 + n_rows + hi_r):
                start = POOL_MARGIN + rr * GRID_W
                acc = jnp.zeros((GRID_W, gd), F32)
                for dc in range(-lo_c, hi_c + 1):
                    v = hbuf[start + dc:start + dc + GRID_W, cols]
                    ok = (col + dc >= 0) & (col + dc < GRID_W)
                    acc = acc + jnp.where(ok, v, 0.0)
                cpbuf[rr * GRID_W:(rr + 1) * GRID_W, :] = acc / cnt_c
            for r in range(n_rows):
                acc = jnp.zeros((GRID_W, gd), F32)
                cnt_r = jnp.zeros((GRID_W, 1), F32)
                for dr in range(-lo_r, hi_r + 1):
                    grow = row0 + r + dr
                    ok = (grow >= 0) & (grow < rows_per_batch)
                    v = cpbuf[(2 + r + dr) * GRID_W:(3 + r + dr) * GRID_W, :]
                    acc = acc + jnp.where(ok, v, 0.0)
                    cnt_r = cnt_r + jnp.where(ok, 1.0, 0.0)
                hcur = hbuf[base + r * GRID_W:base + (r + 1) * GRID_W, cols]
                dbuf[r * GRID_W:(r + 1) * GRID_W, :] = (acc / cnt_r - hcur).astype(BF16)
            finish(grp)


def _pool_layer(tl, xall, modtab, layer, norm_g, pool_w, pool_b, pool_scale):
    d = tl.d
    gd = d // POOL_GROUPS
    n_prev_blocks = TM // POOL_PREV
    n_next_blocks = TM // POOL_NEXT
    last_next = tl.n_tok // POOL_NEXT - 1
    kern = functools.partial(_pool_kernel, ctx_len=tl.ctx_len, tiles_per_batch=tl.tiles_per_batch,
                             rows_per_batch=tl.seq // GRID_W)
    vec = pl.BlockSpec((1, d), lambda i: (0, 0))
    buf_rows = 2 * POOL_MARGIN + POOL_PREV + TM + POOL_NEXT
    return pl.pallas_call(
        kern,
        grid=(tl.n_tiles,),
        in_specs=[
            pl.BlockSpec((POOL_PREV, d), lambda i: (jnp.maximum(i * n_prev_blocks - 1, 0), 0)),
            pl.BlockSpec((TM, d), lambda i: (i, 0)),
            pl.BlockSpec((POOL_NEXT, d), lambda i: (jnp.minimum((i + 1) * n_next_blocks, last_next), 0)),
            vec,
            tl.mod_spec(layer, 0), tl.mod_spec(layer, 1), tl.mod_spec(layer, 2),
            pl.BlockSpec((POOL_GROUPS, gd, gd), lambda i: (0, 0, 0)),
            vec, vec,
        ],
        out_specs=pl.BlockSpec((TM, d), lambda i: (i, 0)),
        out_shape=jax.ShapeDtypeStruct((tl.n_tok, d), F32),
        scratch_shapes=[
            pltpu.VMEM((buf_rows, d), F32),
            pltpu.VMEM((POOL_PREV + TM + POOL_NEXT, gd), F32),
            pltpu.VMEM((TM, gd), BF16),
        ],
        compiler_params=_cparams(("arbitrary",)),
        name="pool_layer",
    )(xall, xall, xall, norm_g.reshape(1, d), modtab, modtab, modtab, pool_w,
      pool_b.reshape(1, d), pool_scale.reshape(1, d))


def _router_kernel(x_ref, g_ref, sh_ref, sc_ref, rwt_ref, rb_ref, h_ref, r_ref):
    h = _norm_mod(x_ref[...], g_ref[...], sh_ref[...], sc_ref[...])
    h_ref[...] = h
    logits = lax.dot_general(rwt_ref[...], h, (((1,), (1,)), ((), ())),
                             precision=lax.Precision.HIGHEST, preferred_element_type=F32)
    scores = 1.0 / (1.0 + jnp.exp(-logits))
    sel = scores + rb_ref[...]
    tm = sel.shape[1]
    srow = [scores[e:e + 1, :] for e in range(N_EXPERTS)]
    vrow = [sel[e:e + 1, :] for e in range(N_EXPERTS)]
    best_g = jnp.zeros((1, tm), jnp.int32)
    best_s = None
    for grp in range(N_EXPERT_GROUPS):
        v = vrow[grp * EXPERTS_PER_GROUP:(grp + 1) * EXPERTS_PER_GROUP]
        gs = None
        for a in range(EXPERTS_PER_GROUP):
            for b in range(a + 1, EXPERTS_PER_GROUP):
                p = v[a] + v[b]
                gs = p if gs is None else jnp.maximum(gs, p)
        if best_s is None:
            best_s = gs
        else:
            upd = gs > best_s
            best_s = jnp.where(upd, gs, best_s)
            best_g = jnp.where(upd, grp, best_g)
    neg = jnp.full((1, tm), -jnp.inf, F32)
    masked = [jnp.where(best_g == (e // EXPERTS_PER_GROUP), vrow[e], neg) for e in range(N_EXPERTS)]
    m1, i1, s1 = masked[0], jnp.zeros((1, tm), jnp.int32), srow[0]
    for e in range(1, N_EXPERTS):
        upd = masked[e] > m1
        m1 = jnp.where(upd, masked[e], m1)
        i1 = jnp.where(upd, e, i1)
        s1 = jnp.where(upd, srow[e], s1)
    m2, i2, s2 = neg, jnp.full((1, tm), -1, jnp.int32), jnp.zeros((1, tm), F32)
    for e in range(N_EXPERTS):
        cand = i1 != e
        upd = cand & ((masked[e] > m2) | (i2 < 0))
        m2 = jnp.where(upd, masked[e], m2)
        i2 = jnp.where(upd, e, i2)
        s2 = jnp.where(upd, srow[e], s2)
    tot = s1 + s2
    w1, w2 = s1 / tot, s2 / tot
    first_lo = i1 < i2
    e_lo = jnp.where(first_lo, i1, i2).astype(F32)
    e_hi = jnp.where(first_lo, i2, i1).astype(F32)
    w_lo = jnp.where(first_lo, w1, w2)
    w_hi = jnp.where(first_lo, w2, w1)
    zero = jnp.zeros((1, tm), F32)
    r_ref[...] = jnp.concatenate([e_lo, e_hi, w_lo, w_hi, zero, zero, zero, zero], axis=0)


def _router(tl, xall, modtab, layer, norm_g, router_w, router_bias):
    d = tl.d
    vec = pl.BlockSpec((1, d), lambda i: (0, 0))
    return pl.pallas_call(
        _router_kernel,
        grid=(tl.n_tiles,),
        in_specs=[
            pl.BlockSpec((TM, d), lambda i: (i, 0)),
            vec, tl.mod_spec(layer, 3), tl.mod_spec(layer, 4),
            pl.BlockSpec((N_EXPERTS, d), lambda i: (0, 0)),
            pl.BlockSpec((N_EXPERTS, 1), lambda i: (0, 0)),
        ],
        out_specs=[
            pl.BlockSpec((TM, d), lambda i: (i, 0)),
            pl.BlockSpec((8, TM), lambda i: (0, i)),
        ],
        out_shape=[
            jax.ShapeDtypeStruct((tl.n_tok, d), F32),
            jax.ShapeDtypeStruct((8, tl.n_tok), F32),
        ],
        compiler_params=_cparams(("arbitrary",)),
        name="moe_router",
    )(xall, norm_g.reshape(1, d), modtab, modtab, router_w.T, router_bias.reshape(N_EXPERTS, 1))


_PAIR_ORDER = ((0, 1), (0, 2), (0, 3), (1, 3), (1, 2), (2, 3))
_PAIR_SLOT_A = (0, 0, 0, 1, 1, 3)
_PAIR_SLOT_B = (1, 2, 3, 3, 2, 2)
N_PAIR_CLASSES = N_EXPERT_GROUPS * len(_PAIR_ORDER)


def _moe_tiles(n_tok):
    return -(-(n_tok + N_PAIR_CLASSES * (MOE_ROWS - 1)) // MOE_ROWS) + 1


def _dispatch(route, n_tok):
    e_lo = route[0].astype(jnp.int32)
    e_hi = route[1].astype(jnp.int32)
    grp = e_lo // EXPERTS_PER_GROUP
    pair_of = [0] * (EXPERTS_PER_GROUP * EXPERTS_PER_GROUP)
    for j, (a, b) in enumerate(_PAIR_ORDER):
        pair_of[a * EXPERTS_PER_GROUP + b] = j
    local = (e_lo % EXPERTS_PER_GROUP) * EXPERTS_PER_GROUP + e_hi % EXPERTS_PER_GROUP
    cls = grp * len(_PAIR_ORDER) + jnp.asarray(pair_of, jnp.int32)[local]
    n_pair = len(_PAIR_ORDER)
    cls_ids = jnp.arange(N_PAIR_CLASSES, dtype=jnp.int32)
    slot_a_tab = (cls_ids // n_pair) * EXPERTS_PER_GROUP + jnp.asarray(_PAIR_SLOT_A, jnp.int32)[cls_ids % n_pair]
    slot_b_tab = (cls_ids // n_pair) * EXPERTS_PER_GROUP + jnp.asarray(_PAIR_SLOT_B, jnp.int32)[cls_ids % n_pair]
    onehot = (cls[:, None] == cls_ids[None, :]).astype(jnp.int32)
    csum = jnp.cumsum(onehot, axis=0)
    rank = jnp.sum(csum * onehot, axis=1) - 1
    cnt = csum[-1]
    ntile = (cnt + MOE_ROWS - 1) // MOE_ROWS
    tile_end = jnp.cumsum(ntile)
    tile_start = tile_end - ntile
    pos = (jnp.sum(onehot * tile_start[None, :], axis=1) * MOE_ROWS + rank).astype(jnp.int32)
    n_tiles = _moe_tiles(n_tok)
    n_used = tile_end[-1]
    tiles = jnp.arange(n_tiles, dtype=jnp.int32)
    tcls = jnp.sum((jnp.minimum(tiles, n_used - 1)[:, None] >= tile_end[None, :]).astype(jnp.int32), axis=1)
    ea = slot_a_tab[tcls]
    eb = slot_b_tab[tcls]
    prev = jnp.maximum(tiles - 1, 0)
    need_a = (tiles < n_used) & ((tiles == 0) | (ea != ea[prev]))
    need_b = (tiles < n_used) & ((tiles == 0) | (eb != eb[prev]))
    first_load = jnp.where(need_a, ea, jnp.where(need_b, eb, -1))
    big = jnp.int32(n_tiles)
    cand = jnp.where(need_a | need_b, tiles, big)
    next_ge = lax.cummin(cand[::-1])[::-1]
    next_gt = jnp.concatenate([next_ge[1:], big[None]])
    next_load = jnp.where(next_gt < big, first_load[jnp.minimum(next_gt, n_tiles - 1)], -1).astype(jnp.int32)
    pair = cls % n_pair
    a_is_lo = functools.reduce(
        jnp.logical_or, [pair == j for j in range(n_pair) if _PAIR_SLOT_A[j] == _PAIR_ORDER[j][0]])
    w_a = jnp.where(a_is_lo, route[2], route[3])
    w_b = jnp.where(a_is_lo, route[3], route[2])
    rows = jnp.stack([jnp.arange(n_tok, dtype=jnp.int32).astype(F32), w_a, w_b], axis=1)
    slot_rows = jnp.zeros((n_tiles * MOE_ROWS, 3), F32).at[pos].set(rows)
    slot_tok = slot_rows[:, 0].astype(jnp.int32)
    slot_w = slot_rows[:, 1:3]
    return (pos, slot_tok, slot_w, ea.astype(jnp.int32), eb.astype(jnp.int32), next_load,
            n_used.reshape(1).astype(jnp.int32))


def _row_copy(src_hbm, row, dst, r, sem):
    return pltpu.make_async_copy(src_hbm.at[pl.ds(row, 1), :], dst.at[pl.ds(r, 1), :], sem)


def _wait_rows(src_hbm, dst, sem):
    pltpu.make_async_copy(src_hbm.at[pl.ds(0, dst.shape[0]), :], dst, sem).wait()


def _ffn_kernel(ea_ref, eb_ref, nl_ref, nused_ref, stok_ref, h_hbm, sw_ref, wg_hbm, wu_hbm, wd_hbm, o_ref,
                xbuf, wgs, wus, wds, wga, wua, wda, wgb, wub, wdb, sem, wsem, *, layer):
    t = pl.program_id(0)
    slot = t % 2
    n_used = nused_ref[0]
    prev = jnp.maximum(t - 1, 0)
    need_a = (t < n_used) & ((t == 0) | (ea_ref[t] != ea_ref[prev]))
    need_b = (t < n_used) & ((t == 0) | (eb_ref[t] != eb_ref[prev]))

    def weight_copies(e):
        return (pltpu.make_async_copy(wg_hbm.at[layer, e], wgs, wsem.at[0]),
                pltpu.make_async_copy(wu_hbm.at[layer, e], wus, wsem.at[1]),
                pltpu.make_async_copy(wd_hbm.at[layer, e], wds, wsem.at[2]))

    def start_stage(e):
        for cp in weight_copies(e):
            cp.start()

    def take_staged(e, wg, wu, wd):
        for cp in weight_copies(e):
            cp.wait()
        wg[...] = wgs[...].astype(BF16)
        wu[...] = wus[...].astype(BF16)
        wd[...] = wds[...].astype(BF16)

    @pl.when(t == 0)
    def _prime():
        start_stage(ea_ref[0])
        for r in range(MOE_ROWS):
            _row_copy(h_hbm, stok_ref[r], xbuf.at[0], r, sem.at[0]).start()

    @pl.when(need_a)
    def _load_a():
        take_staged(ea_ref[t], wga, wua, wda)

        @pl.when(need_b)
        def _then_b():
            start_stage(eb_ref[t])

        @pl.when(jnp.logical_not(need_b) & (nl_ref[t] >= 0))
        def _then_next():
            start_stage(nl_ref[t])

    @pl.when(need_b)
    def _load_b():
        take_staged(eb_ref[t], wgb, wub, wdb)

        @pl.when(nl_ref[t] >= 0)
        def _then_next():
            start_stage(nl_ref[t])

    @pl.when(t <= n_used)
    def _wait():
        _wait_rows(h_hbm, xbuf.at[slot], sem.at[slot])

    @pl.when(t < n_used)
    def _compute():
        for r in range(MOE_ROWS):
            _row_copy(h_hbm, stok_ref[(t + 1) * MOE_ROWS + r], xbuf.at[1 - slot], r, sem.at[1 - slot]).start()
        xb = xbuf[slot].astype(BF16)
        sw = sw_ref[...]

        def expert(wg, wu, wd, gate):
            hg = jnp.dot(xb, wg[...], preferred_element_type=F32)
            hu = jnp.dot(xb, wu[...], preferred_element_type=F32)
            act = (_silu(hg) * hu * gate).astype(BF16)
            return jnp.dot(act, wd[...], preferred_element_type=F32)

        o_ref[...] = expert(wga, wua, wda, sw[:, 0:1]) + expert(wgb, wub, wdb, sw[:, 1:2])

    @pl.when(t >= n_used)
    def _idle():
        o_ref[...] = jnp.zeros_like(o_ref)


def _moe_ffn(n_tok, h2, slot_tok, slot_w, ea, eb, next_load, n_used, layer, wg, wu, wd):
    _, _, d, f = wg.shape
    n_tiles = _moe_tiles(n_tok)
    any_spec = pl.BlockSpec(memory_space=pl.ANY)
    wbufs = lambda dt: [pltpu.VMEM((d, f), dt), pltpu.VMEM((d, f), dt), pltpu.VMEM((f, d), dt)]
    grid_spec = pltpu.PrefetchScalarGridSpec(
        num_scalar_prefetch=5,
        grid=(n_tiles,),
        in_specs=[any_spec, pl.BlockSpec((MOE_ROWS, 2), lambda t, *_: (t, 0)), any_spec, any_spec, any_spec],
        out_specs=pl.BlockSpec((MOE_ROWS, d), lambda t, *_: (t, 0)),
        scratch_shapes=[pltpu.VMEM((2, MOE_ROWS, d), F32)] + wbufs(F32) + wbufs(BF16) + wbufs(BF16) + [
            pltpu.SemaphoreType.DMA((2,)),
            pltpu.SemaphoreType.DMA((3,)),
        ],
    )
    return pl.pallas_call(
        functools.partial(_ffn_kernel, layer=layer),
        grid_spec=grid_spec,
        out_shape=jax.ShapeDtypeStruct((n_tiles * MOE_ROWS, d), F32),
        compiler_params=_cparams(("arbitrary",)),
        name="moe_ffn",
    )(ea, eb, next_load, n_used, slot_tok, h2, slot_w, wg, wu, wd)


COMBINE_TM = 256


def _combine_kernel(p_ref, x_ref, gate_ref, fg_ref, y_hbm, o_ref, ybuf, sem, *, final):
    i = pl.program_id(0)
    n_i = pl.num_programs(0)
    slot = i % 2
    tm = COMBINE_TM

    def gather(tile, dst, dsem):
        def body(r, carry):
            _row_copy(y_hbm, p_ref[tile * tm + r], dst, r, dsem).start()
            return carry
        lax.fori_loop(0, tm, body, 0, unroll=8)

    @pl.when(i == 0)
    def _prime():
        gather(0, ybuf.at[0], sem.at[0])

    @pl.when(i + 1 < n_i)
    def _prefetch():
        gather(i + 1, ybuf.at[1 - slot], sem.at[1 - slot])

    _wait_rows(y_hbm, ybuf.at[slot], sem.at[slot])

    def body(g, carry):
        r0 = pl.multiple_of(g * 8, 8)
        x = x_ref[pl.ds(r0, 8), :] + gate_ref[...] * ybuf[slot, pl.ds(r0, 8), :]
        if final:
            ms = jnp.mean(x * x, axis=-1, keepdims=True)
            x = x * lax.rsqrt(ms + NORM_EPS) * fg_ref[...]
        o_ref[pl.ds(r0, 8), :] = x
        return carry
    lax.fori_loop(0, tm // 8, body, 0, unroll=4)


def _moe_combine(tl, xall, y_sorted, pos, modtab, layer, final_g, final):
    d = tl.d
    tm = COMBINE_TM
    per_tile = TM // tm
    skip_steps = tl.n_ctx // tm if final else 0
    grid_spec = pltpu.PrefetchScalarGridSpec(
        num_scalar_prefetch=1,
        grid=(tl.n_tok // tm,),
        in_specs=[
            pl.BlockSpec((tm, d), lambda i, p: (i, 0)),
            tl.mod_spec(layer, 5, tile_of=lambda i, p: i // per_tile),
            pl.BlockSpec((1, d), lambda i, p: (0, 0)),
            pl.BlockSpec(memory_space=pl.ANY),
        ],
        out_specs=pl.BlockSpec((tm, d), lambda i, p: (jnp.maximum(i - skip_steps, 0), 0)),
        scratch_shapes=[
            pltpu.VMEM((2, tm, d), F32),
            pltpu.SemaphoreType.DMA((2,)),
        ],
    )
    return pl.pallas_call(
        functools.partial(_combine_kernel, final=final),
        grid_spec=grid_spec,
        out_shape=jax.ShapeDtypeStruct((tl.n_tok - skip_steps * tm, d), F32),
        compiler_params=_cparams(("arbitrary",)),
        name="moe_combine",
    )(pos, xall, modtab, final_g.reshape(1, d), y_sorted)


def _moe_layer(tl, xall, modtab, layer, norm_g, router_w, router_bias, wg, wu, wd, final_g, final):
    h2, route = _router(tl, xall, modtab, layer, norm_g, router_w, router_bias)
    pos, slot_tok, slot_w, ea, eb, next_load, n_used = _dispatch(route, tl.n_tok)
    y_sorted = _moe_ffn(tl.n_tok, h2, slot_tok, slot_w, ea, eb, next_load, n_used, layer, wg, wu, wd)
    return _moe_combine(tl, xall, y_sorted, pos, modtab, layer, final_g, final)


INPROJ_TN = 1024
INPROJ_M_TILES = 8
OUTPROJ_TN = 512
CONV_HALO = 16


def _normmod_kernel(x_ref, g_ref, sh_ref, sc_ref, wdt_ref, h_ref, dt_ref):
    h = _norm_mod(x_ref[...], g_ref[...], sh_ref[...], sc_ref[...]).astype(BF16)
    h_ref[...] = h
    dt_ref[...] = jnp.dot(h, wdt_ref[...], preferred_element_type=F32)


def _normmod(tl, xall, modtab, layer, norm_g, w_dt):
    d = tl.d
    n_dt = w_dt.shape[1]
    return pl.pallas_call(
        _normmod_kernel,
        grid=(tl.n_tiles,),
        in_specs=[pl.BlockSpec((TM, d), lambda i: (i, 0)), pl.BlockSpec((1, d), lambda i: (0, 0)),
                  tl.mod_spec(layer, 0), tl.mod_spec(layer, 1), pl.BlockSpec((d, n_dt), lambda i: (0, 0))],
        out_specs=[pl.BlockSpec((TM, d), lambda i: (i, 0)), pl.BlockSpec((TM, n_dt), lambda i: (i, 0))],
        out_shape=[jax.ShapeDtypeStruct((tl.n_tok, d), BF16), jax.ShapeDtypeStruct((tl.n_tok, n_dt), F32)],
        compiler_params=_cparams(("arbitrary",)),
        name="ssd_normmod",
    )(xall, norm_g.reshape(1, d), modtab, modtab, w_dt)


def _inproj_kernel(h_ref, w_ref, wt_ref, p_ref, wb, *, n_main_blocks):
    n = pl.program_id(0)

    @pl.when((pl.program_id(1) == 0) & (n < n_main_blocks))
    def _main_columns():
        wb[...] = w_ref[...].astype(BF16)

    @pl.when((pl.program_id(1) == 0) & (n >= n_main_blocks))
    def _tail_columns():
        wb[...] = wt_ref[...]

    p_ref[...] = jnp.dot(h_ref[...], wb[...], preferred_element_type=F32).astype(BF16)


def _inproj(tl, h, in_w_all, j, n_main, w_tail):
    d = tl.d
    tn = INPROJ_TN
    n_main_blocks = n_main // tn
    n_tail_blocks = w_tail.shape[1] // tn
    tm = tl.n_tok // INPROJ_M_TILES if tl.n_tok % (16 * INPROJ_M_TILES) == 0 else TM
    return pl.pallas_call(
        functools.partial(_inproj_kernel, n_main_blocks=n_main_blocks),
        grid=(n_main_blocks + n_tail_blocks, tl.n_tok // tm),
        in_specs=[
            pl.BlockSpec((tm, d), lambda n, i: (i, 0)),
            pl.BlockSpec((None, d, tn), lambda n, i: (j, 0, jnp.minimum(n, n_main_blocks - 1))),
            pl.BlockSpec((d, tn), lambda n, i: (0, jnp.maximum(n - n_main_blocks, 0))),
        ],
        out_specs=pl.BlockSpec((tm, tn), lambda n, i: (i, n)),
        out_shape=jax.ShapeDtypeStruct((tl.n_tok, n_main + w_tail.shape[1]), BF16),
        scratch_shapes=[pltpu.VMEM((d, tn), BF16)],
        compiler_params=_cparams(("arbitrary", "arbitrary")),
        name="ssd_inproj",
    )(h, in_w_all, w_tail)


def _scan_kernel(xs_ref, xsp_ref, xsn_ref, bc_ref, bcp_ref, bcn_ref, dt_ref, cw_ref, cb_ref, hp_ref,
                 dsk_ref, y_ref, ubuf, bm, cm, cmf, bmt, acst, wt, state, *, ncc, ncl):
    q = SSD_CHUNK
    di = xs_ref.shape[1]
    n_grp = SSD_GROUPS
    blocks_per_group = di // n_grp // 128
    d = pl.program_id(0)
    c = pl.program_id(2)
    is_ctx = c < ncc
    fwd = d == 0
    tch = jnp.where(is_ctx, jnp.where(fwd, c, ncc - 1 - c), jnp.where(fwd, c - ncc, ncl - 1 - (c - ncc)))
    first = tch == 0
    last = tch == jnp.where(is_ctx, ncc - 1, ncl - 1)

    @pl.when(c == 0)
    def _reset():
        state[...] = jnp.zeros_like(state)

    hl = CONV_HALO
    ubuf[0:hl, 0:di] = jnp.where(first, 0.0, xsp_ref[...].astype(F32))
    ubuf[hl:hl + q, 0:di] = xs_ref[...].astype(F32)
    ubuf[hl + q:hl + q + hl, 0:di] = jnp.where(last, 0.0, xsn_ref[...].astype(F32))
    ubuf[0:hl, di:] = jnp.where(first, 0.0, bcp_ref[...].astype(F32))
    ubuf[hl:hl + q, di:] = bc_ref[...].astype(F32)
    ubuf[hl + q:hl + q + hl, di:] = jnp.where(last, 0.0, bcn_ref[...].astype(F32))

    def conv_block(col):
        acc = cb_ref[:, pl.ds(col, 128)]
        for k in range(SSD_CONV):
            off = hl + k - SSD_CONV // 2
            acc = acc + cw_ref[k:k + 1, pl.ds(col, 128)] * ubuf[off:off + q, pl.ds(col, 128)]
        return _silu(acc)

    hp = hp_ref[...]
    a_coef = -jnp.exp(hp[0:1])
    raw = dt_ref[...] + hp[1:2]
    dt = jnp.maximum(raw, 0.0) + jnp.log(1.0 + jnp.exp(-jnp.abs(raw)))
    a = dt * a_coef
    li = lax.broadcasted_iota(jnp.int32, (q, q), 0)
    si = lax.broadcasted_iota(jnp.int32, (q, q), 1)
    tri = jnp.where(fwd, li - si, si - li) >= 0
    trib = jnp.where(tri, 1.0, 0.0).astype(BF16)
    a_hi = a.astype(BF16)
    r1 = a - a_hi.astype(F32)
    a_mid = r1.astype(BF16)
    a_lo = (r1 - a_mid.astype(F32)).astype(BF16)
    acs = (jnp.dot(trib, a_hi, preferred_element_type=F32) + jnp.dot(trib, a_mid, preferred_element_type=F32)
           + jnp.dot(trib, a_lo, preferred_element_type=F32))
    tot = jnp.sum(a, axis=0, keepdims=True)
    eacs = jnp.exp(acs)
    etot = jnp.exp(tot)
    acst[...] = (acs - jnp.log(dt)).T
    wt[...] = (dt * jnp.exp(tot - acs)).T

    eye = jnp.where(li == si, 1.0, 0.0).astype(BF16)

    def b_body(j, carry):
        col = pl.multiple_of(j * 128, 128)
        vb = conv_block(di + col).astype(BF16)
        bm[:, pl.ds(col, 128)] = vb
        bmt[j] = lax.dot_general(eye, vb, (((1,), (1,)), ((), ())), preferred_element_type=F32)
        return carry
    lax.fori_loop(0, n_grp, b_body, 0, unroll=2)

    def c_body(j, carry):
        col = pl.multiple_of(j * 128, 128)
        v = conv_block(di + n_grp * SSD_STATE + col)
        cmf[:, pl.ds(col, 128)] = v
        cm[:, pl.ds(col, 128)] = v.astype(BF16)
        return carry
    lax.fori_loop(0, n_grp, c_body, 0, unroll=2)

    lane = lax.broadcasted_iota(jnp.int32, (q, 128), 1)
    left = lane < SSD_HEAD_DIM
    left_row = left[0:1, :]
    for pb in range(di // 128):
        grp = pb // blocks_per_group
        gl = slice(grp * SSD_STATE, (grp + 1) * SSD_STATE)
        pc = slice(pb * 128, (pb + 1) * 128)
        sc = slice((pb % blocks_per_group) * 128, (pb % blocks_per_group + 1) * 128)
        if pb % blocks_per_group == 0:
            cbm = lax.dot_general(cm[:, gl], bm[:, gl], (((1,), (1,)), ((), ())), preferred_element_type=F32)
        h0, h1 = 2 * pb, 2 * pb + 1
        ms, cs, bts = [], [], []
        for h in (h0, h1):
            colb = jnp.broadcast_to(acs[:, h:h + 1], (q, q))
            arg = jnp.where(tri, colb - acst[h:h + 1, :], -jnp.inf)
            ms.append((cbm * jnp.exp(arg)).astype(BF16))
            cs.append((cmf[:, gl] * jnp.broadcast_to(eacs[:, h:h + 1], (q, SSD_STATE))).astype(BF16))
            bts.append((bmt[grp] * wt[h:h + 1, :]).astype(BF16))
        xf = conv_block(pb * 128)
        xp = xf.astype(BF16)
        zero = jnp.zeros_like(xp)
        xe, xo = jnp.where(left, xp, zero), jnp.where(left, zero, xp)
        s_old = state[grp, :, sc]
        sb = s_old.astype(BF16)
        se, so = jnp.where(left, sb, zero), jnp.where(left, zero, sb)
        lhs = jnp.concatenate(ms + cs, axis=1)
        rhs = jnp.concatenate([xe, xo, se, so], axis=0)
        y = jnp.dot(lhs, rhs, preferred_element_type=F32) + xf * dsk_ref[:, pc]
        y_ref[:, pc] = y.astype(BF16)
        et = jnp.where(left_row, etot[:, h0:h0 + 1], etot[:, h1:h1 + 1])
        upd = jnp.dot(jnp.concatenate(bts, axis=1), jnp.concatenate([xe, xo], axis=0), preferred_element_type=F32)
        state[grp, :, sc] = s_old * et + upd


def _ssd_scan(tl, proj, dt2, conv_w, conv_b, head_params, dskip_lanes):
    n_tok = tl.n_tok
    q = SSD_CHUNK
    n_heads = dt2.shape[-1]
    di = n_heads * SSD_HEAD_DIM
    gn = SSD_GROUPS * SSD_STATE
    conv_ch = di + 2 * gn
    assert (2 * di) % (2 * gn) == 0 and (di // SSD_GROUPS) % 128 == 0
    ncc, ncl = tl.ctx_len // q, tl.seq // q
    bc_block0 = 2 * di // (2 * gn)
    hpc = q // CONV_HALO
    last_halo = n_tok // CONV_HALO - 1

    def blk(d, b, c):
        fwd = d == 0
        is_ctx = c < ncc
        t_ctx = jnp.where(fwd, c, ncc - 1 - c)
        t_lat = jnp.where(fwd, c - ncc, ncl - 1 - (c - ncc))
        return jnp.where(is_ctx, b * ncc + t_ctx, tl.bsz * ncc + b * ncl + t_lat)

    prev = lambda d, b, c: jnp.maximum(blk(d, b, c) * hpc - 1, 0)
    nxt = lambda d, b, c: jnp.minimum((blk(d, b, c) + 1) * hpc, last_halo)
    kern = functools.partial(_scan_kernel, ncc=ncc, ncl=ncl)
    return pl.pallas_call(
        kern,
        grid=(2, tl.bsz, ncc + ncl),
        in_specs=[
            pl.BlockSpec((q, di), lambda d, b, c: (blk(d, b, c), 1)),
            pl.BlockSpec((CONV_HALO, di), lambda d, b, c: (prev(d, b, c), 1)),
            pl.BlockSpec((CONV_HALO, di), lambda d, b, c: (nxt(d, b, c), 1)),
            pl.BlockSpec((q, 2 * gn), lambda d, b, c: (blk(d, b, c), bc_block0 + d)),
            pl.BlockSpec((CONV_HALO, 2 * gn), lambda d, b, c: (prev(d, b, c), bc_block0 + d)),
            pl.BlockSpec((CONV_HALO, 2 * gn), lambda d, b, c: (nxt(d, b, c), bc_block0 + d)),
            pl.BlockSpec((None, q, n_heads), lambda d, b, c: (d, blk(d, b, c), 0)),
            pl.BlockSpec((None, SSD_CONV, conv_ch), lambda d, b, c: (d, 0, 0)),
            pl.BlockSpec((None, 1, conv_ch), lambda d, b, c: (d, 0, 0)),
            pl.BlockSpec((None, 8, n_heads), lambda d, b, c: (d, 0, 0)),
            pl.BlockSpec((None, 1, di), lambda d, b, c: (d, 0, 0)),
        ],
        out_specs=pl.BlockSpec((None, q, di), lambda d, b, c: (d, blk(d, b, c), 0)),
        out_shape=jax.ShapeDtypeStruct((2, n_tok, di), BF16),
        scratch_shapes=[
            pltpu.VMEM((q + 2 * CONV_HALO, conv_ch), F32),
            pltpu.VMEM((q, gn), BF16),
            pltpu.VMEM((q, gn), BF16),
            pltpu.VMEM((q, gn), F32),
            pltpu.VMEM((SSD_GROUPS, SSD_STATE, q), F32),
            pltpu.VMEM((n_heads, q), F32),
            pltpu.VMEM((n_heads, q), F32),
            pltpu.VMEM((SSD_GROUPS, SSD_STATE, di // SSD_GROUPS), F32),
        ],
        compiler_params=_cparams(("arbitrary", "arbitrary", "arbitrary")),
        name="ssd_scan",
    )(proj, proj, proj, proj, proj, proj, dt2, conv_w, conv_b.reshape(2, 1, conv_ch), head_params, dskip_lanes)


def _outproj_kernel(y0_ref, y1_ref, z_ref, ng_ref, w_ref, x_ref, gate_ref, o_ref, ybuf):
    @pl.when(pl.program_id(1) == 0)
    def _prologue():
        gw = ybuf.shape[1] // SSD_GROUPS
        for grp in range(SSD_GROUPS):
            cols = slice(grp * gw, (grp + 1) * gw)
            y = (y0_ref[:, cols].astype(F32) + y1_ref[:, cols].astype(F32)) * _silu(z_ref[:, cols].astype(F32))
            ms = jnp.mean(y * y, axis=-1, keepdims=True)
            ybuf[:, cols] = (y * lax.rsqrt(ms + NORM_EPS) * ng_ref[:, cols]).astype(BF16)

    o_ref[...] = x_ref[...] + gate_ref[...] * jnp.dot(ybuf[...], w_ref[...], preferred_element_type=F32)


def _outproj(tl, xall, modtab, layer, y2, proj, norm_g, out_w):
    d = tl.d
    di = out_w.shape[0]
    tn = OUTPROJ_TN
    gate_spec = pl.BlockSpec(
        (None, 1, tn), lambda i, j: ((layer * COND_ROWS + tl.seg(i)) * N_MOD + 2, 0, j))
    return pl.pallas_call(
        _outproj_kernel,
        grid=(tl.n_tiles, d // tn),
        in_specs=[
            pl.BlockSpec((None, TM, di), lambda i, j: (0, i, 0)),
            pl.BlockSpec((None, TM, di), lambda i, j: (1, i, 0)),
            pl.BlockSpec((TM, di), lambda i, j: (i, 0)),
            pl.BlockSpec((1, di), lambda i, j: (0, 0)),
            pl.BlockSpec((di, tn), lambda i, j: (0, j)),
            pl.BlockSpec((TM, tn), lambda i, j: (i, j)),
            gate_spec,
        ],
        out_specs=pl.BlockSpec((TM, tn), lambda i, j: (i, j)),
        out_shape=jax.ShapeDtypeStruct((tl.n_tok, d), F32),
        scratch_shapes=[pltpu.VMEM((TM, di), BF16)],
        compiler_params=_cparams(("arbitrary", "arbitrary")),
        name="ssd_outproj",
    )(y2, y2, proj, norm_g.reshape(1, di), out_w, xall, modtab)


def _ssd_layer(tl, xall, modtab, layer, norm_g, in_w_all, j, conv_w, conv_b, a_log, dt_bias, d_skip, ssd_norm_g,
               out_w):
    n_heads = a_log.shape[-1]
    di = n_heads * SSD_HEAD_DIM
    gn = SSD_GROUPS * SSD_STATE
    dir_cols = 2 * gn + n_heads
    base0 = 2 * di
    base1 = base0 + dir_cols
    n_main = base0 + 2 * gn
    in_w = in_w_all[j]
    w_tail = in_w[:, base1:base1 + 2 * gn].astype(BF16)
    w_dt = jnp.concatenate(
        [in_w[:, base0 + 2 * gn:base0 + dir_cols], in_w[:, base1 + 2 * gn:base1 + dir_cols]], axis=1).astype(BF16)
    h, dt_raw = _normmod(tl, xall, modtab, layer, norm_g, w_dt)
    proj = _inproj(tl, h, in_w_all, j, n_main, w_tail)
    dt2 = dt_raw.reshape(tl.n_tok, 2, n_heads).transpose(1, 0, 2)
    head_params = jnp.zeros((2, 8, n_heads), F32).at[:, 0].set(a_log).at[:, 1].set(dt_bias)
    dskip_lanes = jnp.repeat(d_skip, SSD_HEAD_DIM, axis=-1).reshape(2, 1, di)
    y2 = _ssd_scan(tl, proj, dt2, conv_w, conv_b, head_params, dskip_lanes)
    return _outproj(tl, xall, modtab, layer, y2, proj, ssd_norm_g, out_w.astype(BF16))


def kernel(x, c, ctx, c_ctx, mod_w, mod_b, norm_mix_g, norm_ffn_g, pool_w, pool_b, pool_scale, ssd_in_w,
           ssd_conv_w, ssd_conv_b, ssd_a_log, ssd_dt_bias, ssd_d, ssd_norm_g, ssd_out_w, router_w, router_bias,
           moe_w_gate, moe_w_up, moe_w_down, final_norm_g):
    bsz, seq, d = x.shape
    ctx_len = ctx.shape[1]
    depth = mod_w.shape[0]
    assert bsz + 1 <= COND_ROWS
    tl = _Tiles(bsz, ctx_len, seq, d)
    cond = jnp.concatenate([c_ctx[None, :], c, jnp.zeros((COND_ROWS - 1 - bsz, d), F32)], axis=0)
    modtab = _adaln_table(cond, mod_w, mod_b)
    xall = jnp.concatenate([ctx.reshape(bsz * ctx_len, d), x.reshape(bsz * seq, d)], axis=0)
    for i in range(depth):
        j = i // 2
        if i % 2 == 0:
            xall = _pool_layer(tl, xall, modtab, i, norm_mix_g[i], pool_w[j], pool_b[j], pool_scale[j])
        else:
            xall = _ssd_layer(tl, xall, modtab, i, norm_mix_g[i], ssd_in_w, j, ssd_conv_w[j], ssd_conv_b[j],
                              ssd_a_log[j], ssd_dt_bias[j], ssd_d[j], ssd_norm_g[j], ssd_out_w[j])
        xall = _moe_layer(tl, xall, modtab, i, norm_ffn_g[i], router_w, router_bias,
                          moe_w_gate, moe_w_up, moe_w_down, final_norm_g, i == depth - 1)
    return xall.reshape(bsz, seq, d)
```

```python
import functools

import jax
import jax.numpy as jnp
from jax import lax
from jax.experimental import pallas as pl
from jax.experimental.pallas import tpu as pltpu

F32 = jnp.float32
BF16 = jnp.bfloat16

GRID_W = 64
NORM_EPS = 1e-6
N_MOD = 6
POOL_GROUPS = 4
POOL_WINDOWS_1D = (2, 4, 8, 16)
POOL_WINDOWS_2D = ((1, 2), (2, 2), (2, 4), (4, 4))
SSD_HEAD_DIM = 64
SSD_GROUPS = 8
SSD_STATE = 128
SSD_CONV = 4
SSD_CHUNK = 128
N_EXPERTS = 16
N_EXPERT_GROUPS = 4
EXPERTS_PER_GROUP = 4
TOP_K = 2

TM = 512
MOE_ROWS = 256
COND_ROWS = 8
VMEM_LIMIT = 56 * 1024 * 1024


def _silu(v):
    return v / (1.0 + jnp.exp(-v))


def _norm_mod(x, g, shift, scale):
    ms = jnp.mean(x * x, axis=-1, keepdims=True)
    y = x * lax.rsqrt(ms + NORM_EPS) * g
    return y * (1.0 + scale) + shift


def _cparams(sem):
    return pltpu.CompilerParams(dimension_semantics=sem, vmem_limit_bytes=VMEM_LIMIT)


def _adaln_kernel(cond_ref, w_ref, b_ref, o_ref):
    s = _silu(cond_ref[...]).astype(BF16)
    o_ref[...] = jnp.dot(s, w_ref[...].astype(BF16), preferred_element_type=F32) + b_ref[...]


def _adaln_table(cond, mod_w, mod_b):
    depth, d, nd = mod_w.shape
    tn = 1024 if nd % 1024 == 0 else nd
    out = pl.pallas_call(
        _adaln_kernel,
        grid=(depth, nd // tn),
        in_specs=[
            pl.BlockSpec((COND_ROWS, d), lambda l, j: (0, 0)),
            pl.BlockSpec((None, d, tn), lambda l, j: (l, 0, j)),
            pl.BlockSpec((None, 1, tn), lambda l, j: (l, 0, j)),
        ],
        out_specs=pl.BlockSpec((None, COND_ROWS, tn), lambda l, j: (l, 0, j)),
        out_shape=jax.ShapeDtypeStruct((depth, COND_ROWS, nd), F32),
        compiler_params=_cparams(("arbitrary", "arbitrary")),
        name="adaln",
    )(cond, mod_w, mod_b.reshape(depth, 1, nd))
    return out.reshape(depth * COND_ROWS * N_MOD, 1, d)


class _Tiles:
    def __init__(self, bsz, ctx_len, seq, d):
        assert bsz * ctx_len == TM and seq % TM == 0
        self.bsz, self.ctx_len, self.seq, self.d = bsz, ctx_len, seq, d
        self.n_ctx = bsz * ctx_len
        self.n_tok = self.n_ctx + bsz * seq
        self.n_tiles = self.n_tok // TM
        self.tiles_per_batch = seq // TM

    def seg(self, i):
        return jnp.where(i == 0, 0, 1 + (i - 1) // self.tiles_per_batch)

    def mod_spec(self, layer, k, tile_of=lambda *idx: idx[0]):
        def imap(*idx):
            return ((layer * COND_ROWS + self.seg(tile_of(*idx))) * N_MOD + k, 0, 0)
        return pl.BlockSpec((None, 1, self.d), imap)


POOL_MARGIN = 8
POOL_PREV = 2 * GRID_W
POOL_NEXT = GRID_W


def _pool_kernel(xp_ref, xc_ref, xn_ref, g_ref, sh_ref, sc_ref, gate_ref, w_ref, b_ref, ps_ref,
                 o_ref, hbuf, cpbuf, dbuf, *, ctx_len, tiles_per_batch, rows_per_batch):
    i = pl.program_id(0)
    gd = w_ref.shape[-1]
    g, sh, sc = g_ref[...], sh_ref[...], sc_ref[...]
    base = POOL_MARGIN + POOL_PREV
    zeros_m = jnp.zeros((POOL_MARGIN, hbuf.shape[1]), F32)
    hbuf[0:POOL_MARGIN, :] = zeros_m
    hbuf[base + TM + POOL_NEXT:base + TM + POOL_NEXT + POOL_MARGIN, :] = zeros_m
    hbuf[POOL_MARGIN:base, :] = _norm_mod(xp_ref[...], g, sh, sc)
    hbuf[base:base + TM, :] = _norm_mod(xc_ref[...], g, sh, sc)
    hbuf[base + TM:base + TM + POOL_NEXT, :] = _norm_mod(xn_ref[...], g, sh, sc)

    def finish(grp):
        cols = slice(grp * gd, (grp + 1) * gd)
        y = jnp.dot(dbuf[...], w_ref[grp].astype(BF16), preferred_element_type=F32)
        y = (y + b_ref[:, cols]) * ps_ref[:, cols]
        o_ref[:, cols] = xc_ref[:, cols] + gate_ref[:, cols] * y

    @pl.when(i == 0)
    def _ctx():
        ch = GRID_W
        for grp in range(POOL_GROUPS):
            cols = slice(grp * gd, (grp + 1) * gd)
            w = POOL_WINDOWS_1D[grp]
            lo, hi = w // 2, w - 1 - w // 2
            for k in range(TM // ch):
                pos = (k * ch) % ctx_len + lax.broadcasted_iota(jnp.int32, (ch, 1), 0)
                acc = jnp.zeros((ch, gd), F32)
                for dd in range(-lo, hi + 1):
                    v = hbuf[base + k * ch + dd:base + k * ch + dd + ch, cols]
                    ok = (pos + dd >= 0) & (pos + dd < ctx_len)
                    acc = acc + jnp.where(ok, v, 0.0)
                cnt = jnp.minimum(pos + hi, ctx_len - 1) - jnp.maximum(pos - lo, 0) + 1
                t = acc / cnt.astype(F32)
                hcur = hbuf[base + k * ch:base + (k + 1) * ch, cols]
                dbuf[k * ch:(k + 1) * ch, :] = (t - hcur).astype(BF16)
            finish(grp)

    @pl.when(i > 0)
    def _latent():
        tile_in_batch = (i - 1) % tiles_per_batch
        row0 = tile_in_batch * (TM // GRID_W)
        col = lax.broadcasted_iota(jnp.int32, (GRID_W, 1), 0)
        n_rows = TM // GRID_W
        for grp in range(POOL_GROUPS):
            cols = slice(grp * gd, (grp + 1) * gd)
            wr, wc = POOL_WINDOWS_2D[grp]
            lo_r, hi_r = wr // 2, wr - 1 - wr // 2
            lo_c, hi_c = wc // 2, wc - 1 - wc // 2
            cnt_c = (jnp.minimum(col + hi_c, GRID_W - 1) - jnp.maximum(col - lo_c, 0) + 1).astype(F32)
            for rr in range(2 - lo_r, 2 + n_rows + hi_r):
                start = POOL_MARGIN + rr * GRID_W
                acc = jnp.zeros((GRID_W, gd), F32)
                for dc in range(-lo_c, hi_c + 1):
                    v = hbuf[start + dc:start + dc + GRID_W, cols]
                    ok = (col + dc >= 0) & (col + dc < GRID_W)
                    acc = acc + jnp.where(ok, v, 0.0)
                cpbuf[rr * GRID_W:(rr + 1) * GRID_W, :] = acc / cnt_c
            for r in range(n_rows):
                acc = jnp.zeros((GRID_W, gd), F32)
                cnt_r = jnp.zeros((GRID_W, 1), F32)
                for dr in range(-lo_r, hi_r + 1):
                    grow = row0 + r + dr
                    ok = (grow >= 0) & (grow < rows_per_batch)
                    v = cpbuf[(2 + r + dr) * GRID_W:(3 + r + dr) * GRID_W, :]
                    acc = acc + jnp.where(ok, v, 0.0)
                    cnt_r = cnt_r + jnp.where(ok, 1.0, 0.0)
                hcur = hbuf[base + r * GRID_W:base + (r + 1) * GRID_W, cols]
                dbuf[r * GRID_W:(r + 1) * GRID_W, :] = (acc / cnt_r - hcur).astype(BF16)
            finish(grp)


def _pool_layer(tl, xall, modtab, layer, norm_g, pool_w, pool_b, pool_scale):
    d = tl.d
    gd = d // POOL_GROUPS
    n_prev_blocks = TM // POOL_PREV
    n_next_blocks = TM // POOL_NEXT
    last_next = tl.n_tok // POOL_NEXT - 1
    kern = functools.partial(_pool_kernel, ctx_len=tl.ctx_len, tiles_per_batch=tl.tiles_per_batch,
                             rows_per_batch=tl.seq // GRID_W)
    vec = pl.BlockSpec((1, d), lambda i: (0, 0))
    buf_rows = 2 * POOL_MARGIN + POOL_PREV + TM + POOL_NEXT
    return pl.pallas_call(
        kern,
        grid=(tl.n_tiles,),
        in_specs=[
            pl.BlockSpec((POOL_PREV, d), lambda i: (jnp.maximum(i * n_prev_blocks - 1, 0), 0)),
            pl.BlockSpec((TM, d), lambda i: (i, 0)),
            pl.BlockSpec((POOL_NEXT, d), lambda i: (jnp.minimum((i + 1) * n_next_blocks, last_next), 0)),
            vec,
            tl.mod_spec(layer, 0), tl.mod_spec(layer, 1), tl.mod_spec(layer, 2),
            pl.BlockSpec((POOL_GROUPS, gd, gd), lambda i: (0, 0, 0)),
            vec, vec,
        ],
        out_specs=pl.BlockSpec((TM, d), lambda i: (i, 0)),
        out_shape=jax.ShapeDtypeStruct((tl.n_tok, d), F32),
        scratch_shapes=[
            pltpu.VMEM((buf_rows, d), F32),
            pltpu.VMEM((POOL_PREV + TM + POOL_NEXT, gd), F32),
            pltpu.VMEM((TM, gd), BF16),
        ],
        compiler_params=_cparams(("arbitrary",)),
        name="pool_layer",
    )(xall, xall, xall, norm_g.reshape(1, d), modtab, modtab, modtab, pool_w,
      pool_b.reshape(1, d), pool_scale.reshape(1, d))


def _router_kernel(x_ref, g_ref, sh_ref, sc_ref, rwt_ref, rb_ref, h_ref, r_ref):
    h = _norm_mod(x_ref[...], g_ref[...], sh_ref[...], sc_ref[...])
    h_ref[...] = h
    logits = lax.dot_general(rwt_ref[...], h, (((1,), (1,)), ((), ())),
                             precision=lax.Precision.HIGHEST, preferred_element_type=F32)
    scores = 1.0 / (1.0 + jnp.exp(-logits))
    sel = scores + rb_ref[...]
    tm = sel.shape[1]
    srow = [scores[e:e + 1, :] for e in range(N_EXPERTS)]
    vrow = [sel[e:e + 1, :] for e in range(N_EXPERTS)]
    best_g = jnp.zeros((1, tm), jnp.int32)
    best_s = None
    for grp in range(N_EXPERT_GROUPS):
        v = vrow[grp * EXPERTS_PER_GROUP:(grp + 1) * EXPERTS_PER_GROUP]
        gs = None
        for a in range(EXPERTS_PER_GROUP):
            for b in range(a + 1, EXPERTS_PER_GROUP):
                p = v[a] + v[b]
                gs = p if gs is None else jnp.maximum(gs, p)
        if best_s is None:
            best_s = gs
        else:
            upd = gs > best_s
            best_s = jnp.where(upd, gs, best_s)
            best_g = jnp.where(upd, grp, best_g)
    neg = jnp.full((1, tm), -jnp.inf, F32)
    masked = [jnp.where(best_g == (e // EXPERTS_PER_GROUP), vrow[e], neg) for e in range(N_EXPERTS)]
    m1, i1, s1 = masked[0], jnp.zeros((1, tm), jnp.int32), srow[0]
    for e in range(1, N_EXPERTS):
        upd = masked[e] > m1
        m1 = jnp.where(upd, masked[e], m1)
        i1 = jnp.where(upd, e, i1)
        s1 = jnp.where(upd, srow[e], s1)
    m2, i2, s2 = neg, jnp.full((1, tm), -1, jnp.int32), jnp.zeros((1, tm), F32)
    for e in range(N_EXPERTS):
        cand = i1 != e
        upd = cand & ((masked[e] > m2) | (i2 < 0))
        m2 = jnp.where(upd, masked[e], m2)
        i2 = jnp.where(upd, e, i2)
        s2 = jnp.where(upd, srow[e], s2)
    tot = s1 + s2
    w1, w2 = s1 / tot, s2 / tot
    first_lo = i1 < i2
    e_lo = jnp.where(first_lo, i1, i2).astype(F32)
    e_hi = jnp.where(first_lo, i2, i1).astype(F32)
    w_lo = jnp.where(first_lo, w1, w2)
    w_hi = jnp.where(first_lo, w2, w1)
    zero = jnp.zeros((1, tm), F32)
    r_ref[...] = jnp.concatenate([e_lo, e_hi, w_lo, w_hi, zero, zero, zero, zero], axis=0)


def _router(tl, xall, modtab, layer, norm_g, router_w, router_bias):
    d = tl.d
    vec = pl.BlockSpec((1, d), lambda i: (0, 0))
    return pl.pallas_call(
        _router_kernel,
        grid=(tl.n_tiles,),
        in_specs=[
            pl.BlockSpec((TM, d), lambda i: (i, 0)),
            vec, tl.mod_spec(layer, 3), tl.mod_spec(layer, 4),
            pl.BlockSpec((N_EXPERTS, d), lambda i: (0, 0)),
            pl.BlockSpec((N_EXPERTS, 1), lambda i: (0, 0)),
        ],
        out_specs=[
            pl.BlockSpec((TM, d), lambda i: (i, 0)),
            pl.BlockSpec((8, TM), lambda i: (0, i)),
        ],
        out_shape=[
            jax.ShapeDtypeStruct((tl.n_tok, d), F32),
            jax.ShapeDtypeStruct((8, tl.n_tok), F32),
        ],
        compiler_params=_cparams(("arbitrary",)),
        name="moe_router",
    )(xall, norm_g.reshape(1, d), modtab, modtab, router_w.T, router_bias.reshape(N_EXPERTS, 1))


_PAIR_ORDER = ((0, 1), (0, 2), (0, 3), (1, 3), (1, 2), (2, 3))
_PAIR_SLOT_A = (0, 0, 0, 1, 1, 3)
_PAIR_SLOT_B = (1, 2, 3, 3, 2, 2)
N_PAIR_CLASSES = N_EXPERT_GROUPS * len(_PAIR_ORDER)


def _moe_tiles(n_tok):
    return -(-(n_tok + N_PAIR_CLASSES * (MOE_ROWS - 1)) // MOE_ROWS) + 1


def _dispatch(route, n_tok):
    e_lo = route[0].astype(jnp.int32)
    e_hi = route[1].astype(jnp.int32)
    grp = e_lo // EXPERTS_PER_GROUP
    pair_of = [0] * (EXPERTS_PER_GROUP * EXPERTS_PER_GROUP)
    for j, (a, b) in enumerate(_PAIR_ORDER):
        pair_of[a * EXPERTS_PER_GROUP + b] = j
    local = (e_lo % EXPERTS_PER_GROUP) * EXPERTS_PER_GROUP + e_hi % EXPERTS_PER_GROUP
    cls = grp * len(_PAIR_ORDER) + jnp.asarray(pair_of, jnp.int32)[local]
    n_pair = len(_PAIR_ORDER)
    cls_ids = jnp.arange(N_PAIR_CLASSES, dtype=jnp.int32)
    slot_a_tab = (cls_ids // n_pair) * EXPERTS_PER_GROUP + jnp.asarray(_PAIR_SLOT_A, jnp.int32)[cls_ids % n_pair]
    slot_b_tab = (cls_ids // n_pair) * EXPERTS_PER_GROUP + jnp.asarray(_PAIR_SLOT_B, jnp.int32)[cls_ids % n_pair]
    onehot = (cls[:, None] == cls_ids[None, :]).astype(jnp.int32)
    csum = jnp.cumsum(onehot, axis=0)
    rank = jnp.sum(csum * onehot, axis=1) - 1
    cnt = csum[-1]
    ntile = (cnt + MOE_ROWS - 1) // MOE_ROWS
    tile_end = jnp.cumsum(ntile)
    tile_start = tile_end - ntile
    pos = (jnp.sum(onehot * tile_start[None, :], axis=1) * MOE_ROWS + rank).astype(jnp.int32)
    n_tiles = _moe_tiles(n_tok)
    n_used = tile_end[-1]
    tiles = jnp.arange(n_tiles, dtype=jnp.int32)
    tcls = jnp.sum((jnp.minimum(tiles, n_used - 1)[:, None] >= tile_end[None, :]).astype(jnp.int32), axis=1)
    ea = slot_a_tab[tcls]
    eb = slot_b_tab[tcls]
    prev = jnp.maximum(tiles - 1, 0)
    need_a = (tiles < n_used) & ((tiles == 0) | (ea != ea[prev]))
    need_b = (tiles < n_used) & ((tiles == 0) | (eb != eb[prev]))
    first_load = jnp.where(need_a, ea, jnp.where(need_b, eb, -1))
    big = jnp.int32(n_tiles)
    cand = jnp.where(need_a | need_b, tiles, big)
    next_ge = lax.cummin(cand[::-1])[::-1]
    next_gt = jnp.concatenate([next_ge[1:], big[None]])
    next_load = jnp.where(next_gt < big, first_load[jnp.minimum(next_gt, n_tiles - 1)], -1).astype(jnp.int32)
    pair = cls % n_pair
    a_is_lo = functools.reduce(
        jnp.logical_or, [pair == j for j in range(n_pair) if _PAIR_SLOT_A[j] == _PAIR_ORDER[j][0]])
    w_a = jnp.where(a_is_lo, route[2], route[3])
    w_b = jnp.where(a_is_lo, route[3], route[2])
    rows = jnp.stack([jnp.arange(n_tok, dtype=jnp.int32).astype(F32), w_a, w_b], axis=1)
    slot_rows = jnp.zeros((n_tiles * MOE_ROWS, 3), F32).at[pos].set(rows)
    slot_tok = slot_rows[:, 0].astype(jnp.int32)
    slot_w = slot_rows[:, 1:3]
    return (pos, slot_tok, slot_w, ea.astype(jnp.int32), eb.astype(jnp.int32), next_load,
            n_used.reshape(1).astype(jnp.int32))


def _row_copy(src_hbm, row, dst, r, sem):
    return pltpu.make_async_copy(src_hbm.at[pl.ds(row, 1), :], dst.at[pl.ds(r, 1), :], sem)


def _wait_rows(src_hbm, dst, sem):
    pltpu.make_async_copy(src_hbm.at[pl.ds(0, dst.shape[0]), :], dst, sem).wait()


def _ffn_kernel(ea_ref, eb_ref, nl_ref, nused_ref, stok_ref, h_hbm, sw_ref, wg_hbm, wu_hbm, wd_hbm, o_ref,
                xbuf, wgs, wus, wds, wga, wua, wda, wgb, wub, wdb, sem, wsem, *, layer):
    t = pl.program_id(0)
    slot = t % 2
    n_used = nused_ref[0]
    prev = jnp.maximum(t - 1, 0)
    need_a = (t < n_used) & ((t == 0) | (ea_ref[t] != ea_ref[prev]))
    need_b = (t < n_used) & ((t == 0) | (eb_ref[t] != eb_ref[prev]))

    def weight_copies(e):
        return (pltpu.make_async_copy(wg_hbm.at[layer, e], wgs, wsem.at[0]),
                pltpu.make_async_copy(wu_hbm.at[layer, e], wus, wsem.at[1]),
                pltpu.make_async_copy(wd_hbm.at[layer, e], wds, wsem.at[2]))

    def start_stage(e):
        for cp in weight_copies(e):
            cp.start()

    def take_staged(e, wg, wu, wd):
        for cp in weight_copies(e):
            cp.wait()
        wg[...] = wgs[...].astype(BF16)
        wu[...] = wus[...].astype(BF16)
        wd[...] = wds[...].astype(BF16)

    @pl.when(t == 0)
    def _prime():
        start_stage(ea_ref[0])
        for r in range(MOE_ROWS):
            _row_copy(h_hbm, stok_ref[r], xbuf.at[0], r, sem.at[0]).start()

    @pl.when(need_a)
    def _load_a():
        take_staged(ea_ref[t], wga, wua, wda)

        @pl.when(need_b)
        def _then_b():
            start_stage(eb_ref[t])

        @pl.when(jnp.logical_not(need_b) & (nl_ref[t] >= 0))
        def _then_next():
            start_stage(nl_ref[t])

    @pl.when(need_b)
    def _load_b():
        take_staged(eb_ref[t], wgb, wub, wdb)

        @pl.when(nl_ref[t] >= 0)
        def _then_next():
            start_stage(nl_ref[t])

    @pl.when(t <= n_used)
    def _wait():
        _wait_rows(h_hbm, xbuf.at[slot], sem.at[slot])

    @pl.when(t < n_used)
    def _compute():
        for r in range(MOE_ROWS):
            _row_copy(h_hbm, stok_ref[(t + 1) * MOE_ROWS + r], xbuf.at[1 - slot], r, sem.at[1 - slot]).start()
        xb = xbuf[slot].astype(BF16)
        sw = sw_ref[...]

        def expert(wg, wu, wd, gate):
            hg = jnp.dot(xb, wg[...], preferred_element_type=F32)
            hu = jnp.dot(xb, wu[...], preferred_element_type=F32)
            act = (_silu(hg) * hu * gate).astype(BF16)
            return jnp.dot(act, wd[...], preferred_element_type=F32)

        o_ref[...] = expert(wga, wua, wda, sw[:, 0:1]) + expert(wgb, wub, wdb, sw[:, 1:2])

    @pl.when(t >= n_used)
    def _idle():
        o_ref[...] = jnp.zeros_like(o_ref)


def _moe_ffn(n_tok, h2, slot_tok, slot_w, ea, eb, next_load, n_used, layer, wg, wu, wd):
    _, _, d, f = wg.shape
    n_tiles = _moe_tiles(n_tok)
    any_spec = pl.BlockSpec(memory_space=pl.ANY)
    wbufs = lambda dt: [pltpu.VMEM((d, f), dt), pltpu.VMEM((d, f), dt), pltpu.VMEM((f, d), dt)]
    grid_spec = pltpu.PrefetchScalarGridSpec(
        num_scalar_prefetch=5,
        grid=(n_tiles,),
        in_specs=[any_spec, pl.BlockSpec((MOE_ROWS, 2), lambda t, *_: (t, 0)), any_spec, any_spec, any_spec],
        out_specs=pl.BlockSpec((MOE_ROWS, d), lambda t, *_: (t, 0)),
        scratch_shapes=[pltpu.VMEM((2, MOE_ROWS, d), F32)] + wbufs(F32) + wbufs(BF16) + wbufs(BF16) + [
            pltpu.SemaphoreType.DMA((2,)),
            pltpu.SemaphoreType.DMA((3,)),
        ],
    )
    return pl.pallas_call(
        functools.partial(_ffn_kernel, layer=layer),
        grid_spec=grid_spec,
        out_shape=jax.ShapeDtypeStruct((n_tiles * MOE_ROWS, d), F32),
        compiler_params=_cparams(("arbitrary",)),
        name="moe_ffn",
    )(ea, eb, next_load, n_used, slot_tok, h2, slot_w, wg, wu, wd)


COMBINE_TM = 256


def _combine_kernel(p_ref, x_ref, gate_ref, fg_ref, y_hbm, o_ref, ybuf, sem, *, final):
    i = pl.program_id(0)
    n_i = pl.num_programs(0)
    slot = i % 2
    tm = COMBINE_TM

    def gather(tile, dst, dsem):
        def body(r, carry):
            _row_copy(y_hbm, p_ref[tile * tm + r], dst, r, dsem).start()
            return carry
        lax.fori_loop(0, tm, body, 0, unroll=8)

    @pl.when(i == 0)
    def _prime():
        gather(0, ybuf.at[0], sem.at[0])

    @pl.when(i + 1 < n_i)
    def _prefetch():
        gather(i + 1, ybuf.at[1 - slot], sem.at[1 - slot])

    _wait_rows(y_hbm, ybuf.at[slot], sem.at[slot])

    def body(g, carry):
        r0 = pl.multiple_of(g * 8, 8)
        x = x_ref[pl.ds(r0, 8), :] + gate_ref[...] * ybuf[slot, pl.ds(r0, 8), :]
        if final:
            ms = jnp.mean(x * x, axis=-1, keepdims=True)
            x = x * lax.rsqrt(ms + NORM_EPS) * fg_ref[...]
        o_ref[pl.ds(r0, 8), :] = x
        return carry
    lax.fori_loop(0, tm // 8, body, 0, unroll=4)


def _moe_combine(tl, xall, y_sorted, pos, modtab, layer, final_g, final):
    d = tl.d
    tm = COMBINE_TM
    per_tile = TM // tm
    skip_steps = tl.n_ctx // tm if final else 0
    grid_spec = pltpu.PrefetchScalarGridSpec(
        num_scalar_prefetch=1,
        grid=(tl.n_tok // tm,),
        in_specs=[
            pl.BlockSpec((tm, d), lambda i, p: (i, 0)),
            tl.mod_spec(layer, 5, tile_of=lambda i, p: i // per_tile),
            pl.BlockSpec((1, d), lambda i, p: (0, 0)),
            pl.BlockSpec(memory_space=pl.ANY),
        ],
        out_specs=pl.BlockSpec((tm, d), lambda i, p: (jnp.maximum(i - skip_steps, 0), 0)),
        scratch_shapes=[
            pltpu.VMEM((2, tm, d), F32),
            pltpu.SemaphoreType.DMA((2,)),
        ],
    )
    return pl.pallas_call(
        functools.partial(_combine_kernel, final=final),
        grid_spec=grid_spec,
        out_shape=jax.ShapeDtypeStruct((tl.n_tok - skip_steps * tm, d), F32),
        compiler_params=_cparams(("arbitrary",)),
        name="moe_combine",
    )(pos, xall, modtab, final_g.reshape(1, d), y_sorted)


def _moe_layer(tl, xall, modtab, layer, norm_g, router_w, router_bias, wg, wu, wd, final_g, final):
    h2, route = _router(tl, xall, modtab, layer, norm_g, router_w, router_bias)
    pos, slot_tok, slot_w, ea, eb, next_load, n_used = _dispatch(route, tl.n_tok)
    y_sorted = _moe_ffn(tl.n_tok, h2, slot_tok, slot_w, ea, eb, next_load, n_used, layer, wg, wu, wd)
    return _moe_combine(tl, xall, y_sorted, pos, modtab, layer, final_g, final)


INPROJ_TN = 1024
INPROJ_M_TILES = 8
OUTPROJ_TN = 512
CONV_HALO = 16


def _normmod_kernel(x_ref, g_ref, sh_ref, sc_ref, wdt_ref, h_ref, dt_ref):
    h = _norm_mod(x_ref[...], g_ref[...], sh_ref[...], sc_ref[...]).astype(BF16)
    h_ref[...] = h
    dt_ref[...] = jnp.dot(h, wdt_ref[...], preferred_element_type=F32)


def _normmod(tl, xall, modtab, layer, norm_g, w_dt):
    d = tl.d
    n_dt = w_dt.shape[1]
    return pl.pallas_call(
        _normmod_kernel,
        grid=(tl.n_tiles,),
        in_specs=[pl.BlockSpec((TM, d), lambda i: (i, 0)), pl.BlockSpec((1, d), lambda i: (0, 0)),
                  tl.mod_spec(layer, 0), tl.mod_spec(layer, 1), pl.BlockSpec((d, n_dt), lambda i: (0, 0))],
        out_specs=[pl.BlockSpec((TM, d), lambda i: (i, 0)), pl.BlockSpec((TM, n_dt), lambda i: (i, 0))],
        out_shape=[jax.ShapeDtypeStruct((tl.n_tok, d), BF16), jax.ShapeDtypeStruct((tl.n_tok, n_dt), F32)],
        compiler_params=_cparams(("arbitrary",)),
        name="ssd_normmod",
    )(xall, norm_g.reshape(1, d), modtab, modtab, w_dt)


def _inproj_kernel(h_ref, w_ref, wt_ref, p_ref, wb, *, n_main_blocks):
    n = pl.program_id(0)

    @pl.when((pl.program_id(1) == 0) & (n < n_main_blocks))
    def _main_columns():
        wb[...] = w_ref[...].astype(BF16)

    @pl.when((pl.program_id(1) == 0) & (n >= n_main_blocks))
    def _tail_columns():
        wb[...] = wt_ref[...]

    p_ref[...] = jnp.dot(h_ref[...], wb[...], preferred_element_type=F32).astype(BF16)


def _inproj(tl, h, in_w_all, j, n_main, w_tail):
    d = tl.d
    tn = INPROJ_TN
    n_main_blocks = n_main // tn
    n_tail_blocks = w_tail.shape[1] // tn
    tm = tl.n_tok // INPROJ_M_TILES if tl.n_tok % (16 * INPROJ_M_TILES) == 0 else TM
    return pl.pallas_call(
        functools.partial(_inproj_kernel, n_main_blocks=n_main_blocks),
        grid=(n_main_blocks + n_tail_blocks, tl.n_tok // tm),
        in_specs=[
            pl.BlockSpec((tm, d), lambda n, i: (i, 0)),
            pl.BlockSpec((None, d, tn), lambda n, i: (j, 0, jnp.minimum(n, n_main_blocks - 1))),
            pl.BlockSpec((d, tn), lambda n, i: (0, jnp.maximum(n - n_main_blocks, 0))),
        ],
        out_specs=pl.BlockSpec((tm, tn), lambda n, i: (i, n)),
        out_shape=jax.ShapeDtypeStruct((tl.n_tok, n_main + w_tail.shape[1]), BF16),
        scratch_shapes=[pltpu.VMEM((d, tn), BF16)],
        compiler_params=_cparams(("arbitrary", "arbitrary")),
        name="ssd_inproj",
    )(h, in_w_all, w_tail)


def _scan_kernel(xs_ref, xsp_ref, xsn_ref, bc_ref, bcp_ref, bcn_ref, dt_ref, cw_ref, cb_ref, hp_ref,
                 dsk_ref, y_ref, ubuf, bm, cm, cmf, bmt, acst, wt, state, *, ncc, ncl):
    q = SSD_CHUNK
    di = xs_ref.shape[1]
    n_grp = SSD_GROUPS
    blocks_per_group = di // n_grp // 128
    d = pl.program_id(0)
    c = pl.program_id(2)
    is_ctx = c < ncc
    fwd = d == 0
    tch = jnp.where(is_ctx, jnp.where(fwd, c, ncc - 1 - c), jnp.where(fwd, c - ncc, ncl - 1 - (c - ncc)))
    first = tch == 0
    last = tch == jnp.where(is_ctx, ncc - 1, ncl - 1)

    @pl.when(c == 0)
    def _reset():
        state[...] = jnp.zeros_like(state)

    hl = CONV_HALO
    ubuf[0:hl, 0:di] = jnp.where(first, 0.0, xsp_ref[...].astype(F32))
    ubuf[hl:hl + q, 0:di] = xs_ref[...].astype(F32)
    ubuf[hl + q:hl + q + hl, 0:di] = jnp.where(last, 0.0, xsn_ref[...].astype(F32))
    ubuf[0:hl, di:] = jnp.where(first, 0.0, bcp_ref[...].astype(F32))
    ubuf[hl:hl + q, di:] = bc_ref[...].astype(F32)
    ubuf[hl + q:hl + q + hl, di:] = jnp.where(last, 0.0, bcn_ref[...].astype(F32))

    def conv_block(col):
        acc = cb_ref[:, pl.ds(col, 128)]
        for k in range(SSD_CONV):
            off = hl + k - SSD_CONV // 2
            acc = acc + cw_ref[k:k + 1, pl.ds(col, 128)] * ubuf[off:off + q, pl.ds(col, 128)]
        return _silu(acc)

    hp = hp_ref[...]
    a_coef = -jnp.exp(hp[0:1])
    raw = dt_ref[...] + hp[1:2]
    dt = jnp.maximum(raw, 0.0) + jnp.log(1.0 + jnp.exp(-jnp.abs(raw)))
    a = dt * a_coef
    li = lax.broadcasted_iota(jnp.int32, (q, q), 0)
    si = lax.broadcasted_iota(jnp.int32, (q, q), 1)
    tri = jnp.where(fwd, li - si, si - li) >= 0
    trib = jnp.where(tri, 1.0, 0.0).astype(BF16)
    a_hi = a.astype(BF16)
    r1 = a - a_hi.astype(F32)
    a_mid = r1.astype(BF16)
    a_lo = (r1 - a_mid.astype(F32)).astype(BF16)
    acs = (jnp.dot(trib, a_hi, preferred_element_type=F32) + jnp.dot(trib, a_mid, preferred_element_type=F32)
           + jnp.dot(trib, a_lo, preferred_element_type=F32))
    tot = jnp.sum(a, axis=0, keepdims=True)
    eacs = jnp.exp(acs)
    etot = jnp.exp(tot)
    acst[...] = (acs - jnp.log(dt)).T
    wt[...] = (dt * jnp.exp(tot - acs)).T

    eye = jnp.where(li == si, 1.0, 0.0).astype(BF16)

    def b_body(j, carry):
        col = pl.multiple_of(j * 128, 128)
        vb = conv_block(di + col).astype(BF16)
        bm[:, pl.ds(col, 128)] = vb
        bmt[j] = lax.dot_general(eye, vb, (((1,), (1,)), ((), ())), preferred_element_type=F32)
        return carry
    lax.fori_loop(0, n_grp, b_body, 0, unroll=2)

    def c_body(j, carry):
        col = pl.multiple_of(j * 128, 128)
        v = conv_block(di + n_grp * SSD_STATE + col)
        cmf[:, pl.ds(col, 128)] = v
        cm[:, pl.ds(col, 128)] = v.astype(BF16)
        return carry
    lax.fori_loop(0, n_grp, c_body, 0, unroll=2)

    lane = lax.broadcasted_iota(jnp.int32, (q, 128), 1)
    left = lane < SSD_HEAD_DIM
    left_row = left[0:1, :]
    for pb in range(di // 128):
        grp = pb // blocks_per_group
        gl = slice(grp * SSD_STATE, (grp + 1) * SSD_STATE)
        pc = slice(pb * 128, (pb + 1) * 128)
        sc = slice((pb % blocks_per_group) * 128, (pb % blocks_per_group + 1) * 128)
        if pb % blocks_per_group == 0:
            cbm = lax.dot_general(cm[:, gl], bm[:, gl], (((1,), (1,)), ((), ())), preferred_element_type=F32)
        h0, h1 = 2 * pb, 2 * pb + 1
        ms, cs, bts = [], [], []
        for h in (h0, h1):
            colb = jnp.broadcast_to(acs[:, h:h + 1], (q, q))
            arg = jnp.where(tri, colb - acst[h:h + 1, :], -jnp.inf)
            ms.append((cbm * jnp.exp(arg)).astype(BF16))
            cs.append((cmf[:, gl] * jnp.broadcast_to(eacs[:, h:h + 1], (q, SSD_STATE))).astype(BF16))
            bts.append((bmt[grp] * wt[h:h + 1, :]).astype(BF16))
        xf = conv_block(pb * 128)
        xp = xf.astype(BF16)
        zero = jnp.zeros_like(xp)
        xe, xo = jnp.where(left, xp, zero), jnp.where(left, zero, xp)
        s_old = state[grp, :, sc]
        sb = s_old.astype(BF16)
        se, so = jnp.where(left, sb, zero), jnp.where(left, zero, sb)
        lhs = jnp.concatenate(ms + cs, axis=1)
        rhs = jnp.concatenate([xe, xo, se, so], axis=0)
        y = jnp.dot(lhs, rhs, preferred_element_type=F32) + xf * dsk_ref[:, pc]
        y_ref[:, pc] = y.astype(BF16)
        et = jnp.where(left_row, etot[:, h0:h0 + 1], etot[:, h1:h1 + 1])
        upd = jnp.dot(jnp.concatenate(bts, axis=1), jnp.concatenate([xe, xo], axis=0), preferred_element_type=F32)
        state[grp, :, sc] = s_old * et + upd


def _ssd_scan(tl, proj, dt2, conv_w, conv_b, head_params, dskip_lanes):
    n_tok = tl.n_tok
    q = SSD_CHUNK
    n_heads = dt2.shape[-1]
    di = n_heads * SSD_HEAD_DIM
    gn = SSD_GROUPS * SSD_STATE
    conv_ch = di + 2 * gn
    assert (2 * di) % (2 * gn) == 0 and (di // SSD_GROUPS) % 128 == 0
    ncc, ncl = tl.ctx_len // q, tl.seq // q
    bc_block0 = 2 * di // (2 * gn)
    hpc = q // CONV_HALO
    last_halo = n_tok // CONV_HALO - 1

    def blk(d, b, c):
        fwd = d == 0
        is_ctx = c < ncc
        t_ctx = jnp.where(fwd, c, ncc - 1 - c)
        t_lat = jnp.where(fwd, c - ncc, ncl - 1 - (c - ncc))
        return jnp.where(is_ctx, b * ncc + t_ctx, tl.bsz * ncc + b * ncl + t_lat)

    prev = lambda d, b, c: jnp.maximum(blk(d, b, c) * hpc - 1, 0)
    nxt = lambda d, b, c: jnp.minimum((blk(d, b, c) + 1) * hpc, last_halo)
    kern = functools.partial(_scan_kernel, ncc=ncc, ncl=ncl)
    return pl.pallas_call(
        kern,
        grid=(2, tl.bsz, ncc + ncl),
        in_specs=[
            pl.BlockSpec((q, di), lambda d, b, c: (blk(d, b, c), 1)),
            pl.BlockSpec((CONV_HALO, di), lambda d, b, c: (prev(d, b, c), 1)),
            pl.BlockSpec((CONV_HALO, di), lambda d, b, c: (nxt(d, b, c), 1)),
            pl.BlockSpec((q, 2 * gn), lambda d, b, c: (blk(d, b, c), bc_block0 + d)),
            pl.BlockSpec((CONV_HALO, 2 * gn), lambda d, b, c: (prev(d, b, c), bc_block0 + d)),
            pl.BlockSpec((CONV_HALO, 2 * gn), lambda d, b, c: (nxt(d, b, c), bc_block0 + d)),
            pl.BlockSpec((None, q, n_heads), lambda d, b, c: (d, blk(d, b, c), 0)),
            pl.BlockSpec((None, SSD_CONV, conv_ch), lambda d, b, c: (d, 0, 0)),
            pl.BlockSpec((None, 1, conv_ch), lambda d, b, c: (d, 0, 0)),
            pl.BlockSpec((None, 8, n_heads), lambda d, b, c: (d, 0, 0)),
            pl.BlockSpec((None, 1, di), lambda d, b, c: (d, 0, 0)),
        ],
        out_specs=pl.BlockSpec((None, q, di), lambda d, b, c: (d, blk(d, b, c), 0)),
        out_shape=jax.ShapeDtypeStruct((2, n_tok, di), BF16),
        scratch_shapes=[
            pltpu.VMEM((q + 2 * CONV_HALO, conv_ch), F32),
            pltpu.VMEM((q, gn), BF16),
            pltpu.VMEM((q, gn), BF16),
            pltpu.VMEM((q, gn), F32),
            pltpu.VMEM((SSD_GROUPS, SSD_STATE, q), F32),
            pltpu.VMEM((n_heads, q), F32),
            pltpu.VMEM((n_heads, q), F32),
            pltpu.VMEM((SSD_GROUPS, SSD_STATE, di // SSD_GROUPS), F32),
        ],
        compiler_params=_cparams(("arbitrary", "arbitrary", "arbitrary")),
        name="ssd_scan",
    )(proj, proj, proj, proj, proj, proj, dt2, conv_w, conv_b.reshape(2, 1, conv_ch), head_params, dskip_lanes)


def _outproj_kernel(y0_ref, y1_ref, z_ref, ng_ref, w_ref, x_ref, gate_ref, o_ref, ybuf):
    @pl.when(pl.program_id(1) == 0)
    def _prologue():
        gw = ybuf.shape[1] // SSD_GROUPS
        for grp in range(SSD_GROUPS):
            cols = slice(grp * gw, (grp + 1) * gw)
            y = (y0_ref[:, cols].astype(F32) + y1_ref[:, cols].astype(F32)) * _silu(z_ref[:, cols].astype(F32))
            ms = jnp.mean(y * y, axis=-1, keepdims=True)
            ybuf[:, cols] = (y * lax.rsqrt(ms + NORM_EPS) * ng_ref[:, cols]).astype(BF16)

    o_ref[...] = x_ref[...] + gate_ref[...] * jnp.dot(ybuf[...], w_ref[...], preferred_element_type=F32)


def _outproj(tl, xall, modtab, layer, y2, proj, norm_g, out_w):
    d = tl.d
    di = out_w.shape[0]
    tn = OUTPROJ_TN
    gate_spec = pl.BlockSpec(
        (None, 1, tn), lambda i, j: ((layer * COND_ROWS + tl.seg(i)) * N_MOD + 2, 0, j))
    return pl.pallas_call(
        _outproj_kernel,
        grid=(tl.n_tiles, d // tn),
        in_specs=[
            pl.BlockSpec((None, TM, di), lambda i, j: (0, i, 0)),
            pl.BlockSpec((None, TM, di), lambda i, j: (1, i, 0)),
            pl.BlockSpec((TM, di), lambda i, j: (i, 0)),
            pl.BlockSpec((1, di), lambda i, j: (0, 0)),
            pl.BlockSpec((di, tn), lambda i, j: (0, j)),
            pl.BlockSpec((TM, tn), lambda i, j: (i, j)),
            gate_spec,
        ],
        out_specs=pl.BlockSpec((TM, tn), lambda i, j: (i, j)),
        out_shape=jax.ShapeDtypeStruct((tl.n_tok, d), F32),
        scratch_shapes=[pltpu.VMEM((TM, di), BF16)],
        compiler_params=_cparams(("arbitrary", "arbitrary")),
        name="ssd_outproj",
    )(y2, y2, proj, norm_g.reshape(1, di), out_w, xall, modtab)


def _ssd_layer(tl, xall, modtab, layer, norm_g, in_w_all, j, conv_w, conv_b, a_log, dt_bias, d_skip, ssd_norm_g,
               out_w):
    n_heads = a_log.shape[-1]
    di = n_heads * SSD_HEAD_DIM
    gn = SSD_GROUPS * SSD_STATE
    dir_cols = 2 * gn + n_heads
    base0 = 2 * di
    base1 = base0 + dir_cols
    n_main = base0 + 2 * gn
    in_w = in_w_all[j]
    w_tail = in_w[:, base1:base1 + 2 * gn].astype(BF16)
    w_dt = jnp.concatenate(
        [in_w[:, base0 + 2 * gn:base0 + dir_cols], in_w[:, base1 + 2 * gn:base1 + dir_cols]], axis=1).astype(BF16)
    h, dt_raw = _normmod(tl, xall, modtab, layer, norm_g, w_dt)
    proj = _inproj(tl, h, in_w_all, j, n_main, w_tail)
    dt2 = dt_raw.reshape(tl.n_tok, 2, n_heads).transpose(1, 0, 2)
    head_params = jnp.zeros((2, 8, n_heads), F32).at[:, 0].set(a_log).at[:, 1].set(dt_bias)
    dskip_lanes = jnp.repeat(d_skip, SSD_HEAD_DIM, axis=-1).reshape(2, 1, di)
    y2 = _ssd_scan(tl, proj, dt2, conv_w, conv_b, head_params, dskip_lanes)
    return _outproj(tl, xall, modtab, layer, y2, proj, ssd_norm_g, out_w.astype(BF16))


def kernel(x, c, ctx, c_ctx, mod_w, mod_b, norm_mix_g, norm_ffn_g, pool_w, pool_b, pool_scale, ssd_in_w,
           ssd_conv_w, ssd_conv_b, ssd_a_log, ssd_dt_bias, ssd_d, ssd_norm_g, ssd_out_w, router_w, router_bias,
           moe_w_gate, moe_w_up, moe_w_down, final_norm_g):
    bsz, seq, d = x.shape
    ctx_len = ctx.shape[1]
    depth = mod_w.shape[0]
    assert bsz + 1 <= COND_ROWS
    tl = _Tiles(bsz, ctx_len, seq, d)
    cond = jnp.concatenate([c_ctx[None, :], c, jnp.zeros((COND_ROWS - 1 - bsz, d), F32)], axis=0)
    modtab = _adaln_table(cond, mod_w, mod_b)
    xall = jnp.concatenate([ctx.reshape(bsz * ctx_len, d), x.reshape(bsz * seq, d)], axis=0)
    for i in range(depth):
        j = i // 2
        if i % 2 == 0:
            xall = _pool_layer(tl, xall, modtab, i, norm_mix_g[i], pool_w[j], pool_b[j], pool_scale[j])
        else:
            xall = _ssd_layer(tl, xall, modtab, i, norm_mix_g[i], ssd_in_w, j, ssd_conv_w[j], ssd_conv_b[j],
                              ssd_a_log[j], ssd_dt_bias[j], ssd_d[j], ssd_norm_g[j], ssd_out_w[j])
        xall = _moe_layer(tl, xall, modtab, i, norm_ffn_g[i], router_w, router_bias,
                          moe_w_gate, moe_w_up, moe_w_down, final_norm_g, i == depth - 1)
    return xall.reshape(bsz, seq, d)
```

```python
import functools

import jax
import jax.numpy as jnp
from jax import lax
from jax.experimental import pallas as pl
from jax.experimental.pallas import tpu as pltpu

F32 = jnp.float32
BF16 = jnp.bfloat16

GRID_W = 64
NORM_EPS = 1e-6
N_MOD = 6
POOL_GROUPS = 4
POOL_WINDOWS_1D = (2, 4, 8, 16)
POOL_WINDOWS_2D = ((1, 2), (2, 2), (2, 4), (4, 4))
SSD_HEAD_DIM = 64
SSD_GROUPS = 8
SSD_STATE = 128
SSD_CONV = 4
SSD_CHUNK = 128
N_EXPERTS = 16
N_EXPERT_GROUPS = 4
EXPERTS_PER_GROUP = 4
TOP_K = 2

TM = 512
MOE_ROWS = 128
COND_ROWS = 8
NORMMOD_ROWS = 32
VMEM_LIMIT = 56 * 1024 * 1024


def _silu(v):
    return v / (1.0 + jnp.exp(-v))


def _norm_mod(x, g, shift, scale):
    ms = jnp.mean(x * x, axis=-1, keepdims=True)
    y = x * lax.rsqrt(ms + NORM_EPS) * g
    return y * (1.0 + scale) + shift


def _cparams(sem):
    return pltpu.CompilerParams(dimension_semantics=sem, vmem_limit_bytes=VMEM_LIMIT)


def _adaln_kernel(cond_ref, w_ref, b_ref, o_ref):
    s = _silu(cond_ref[...]).astype(BF16)
    o_ref[...] = jnp.dot(s, w_ref[...].astype(BF16), preferred_element_type=F32) + b_ref[...]


def _adaln_table(cond, mod_w, mod_b):
    depth, d, nd = mod_w.shape
    tn = 1024 if nd % 1024 == 0 else nd
    out = pl.pallas_call(
        _adaln_kernel,
        grid=(depth, nd // tn),
        in_specs=[
            pl.BlockSpec((COND_ROWS, d), lambda l, j: (0, 0)),
            pl.BlockSpec((None, d, tn), lambda l, j: (l, 0, j)),
            pl.BlockSpec((None, 1, tn), lambda l, j: (l, 0, j)),
        ],
        out_specs=pl.BlockSpec((None, COND_ROWS, tn), lambda l, j: (l, 0, j)),
        out_shape=jax.ShapeDtypeStruct((depth, COND_ROWS, nd), F32),
        compiler_params=_cparams(("arbitrary", "arbitrary")),
        name="adaln",
    )(cond, mod_w, mod_b.reshape(depth, 1, nd))
    return out.reshape(depth * COND_ROWS * N_MOD, 1, d)


class _Tiles:
    def __init__(self, bsz, ctx_len, seq, d):
        assert bsz * ctx_len == TM and seq % TM == 0
        self.bsz, self.ctx_len, self.seq, self.d = bsz, ctx_len, seq, d
        self.n_ctx = bsz * ctx_len
        self.n_tok = self.n_ctx + bsz * seq
        self.n_tiles = self.n_tok // TM
        self.tiles_per_batch = seq // TM

    def seg(self, i):
        return jnp.where(i == 0, 0, 1 + (i - 1) // self.tiles_per_batch)

    def mod_spec(self, layer, k, tile_of=lambda *idx: idx[0]):
        def imap(*idx):
            return ((layer * COND_ROWS + self.seg(tile_of(*idx))) * N_MOD + k, 0, 0)
        return pl.BlockSpec((None, 1, self.d), imap)


POOL_MARGIN = 8
POOL_PREV = 2 * GRID_W
POOL_NEXT = GRID_W


def _pool_kernel(xp_ref, xc_ref, xn_ref, g_ref, sh_ref, sc_ref, gate_ref, w_ref, b_ref, ps_ref,
                 o_ref, hbuf, cpbuf, dbuf, *, ctx_len, tiles_per_batch, rows_per_batch):
    i = pl.program_id(0)
    gd = w_ref.shape[-1]
    g, sh, sc = g_ref[...], sh_ref[...], sc_ref[...]
    base = POOL_MARGIN + POOL_PREV
    zeros_m = jnp.zeros((POOL_MARGIN, hbuf.shape[1]), F32)
    hbuf[0:POOL_MARGIN, :] = zeros_m
    hbuf[base + TM + POOL_NEXT:base + TM + POOL_NEXT + POOL_MARGIN, :] = zeros_m
    for src, dst0 in ((xp_ref, POOL_MARGIN), (xc_ref, base), (xn_ref, base + TM)):
        for r0 in range(0, src.shape[0], NORMMOD_ROWS):
            hbuf[dst0 + r0:dst0 + r0 + NORMMOD_ROWS, :] = _norm_mod(src[r0:r0 + NORMMOD_ROWS, :], g, sh, sc)

    def finish(grp):
        cols = slice(grp * gd, (grp + 1) * gd)
        y = jnp.dot(dbuf[...], w_ref[grp].astype(BF16), preferred_element_type=F32)
        y = (y + b_ref[:, cols]) * ps_ref[:, cols]
        o_ref[:, cols] = xc_ref[:, cols] + gate_ref[:, cols] * y

    @pl.when(i == 0)
    def _ctx():
        ch = GRID_W
        for grp in range(POOL_GROUPS):
            cols = slice(grp * gd, (grp + 1) * gd)
            w = POOL_WINDOWS_1D[grp]
            lo, hi = w // 2, w - 1 - w // 2
            for k in range(TM // ch):
                pos = (k * ch) % ctx_len + lax.broadcasted_iota(jnp.int32, (ch, 1), 0)
                acc = jnp.zeros((ch, gd), F32)
                for dd in range(-lo, hi + 1):
                    v = hbuf[base + k * ch + dd:base + k * ch + dd + ch, cols]
                    ok = (pos + dd >= 0) & (pos + dd < ctx_len)
                    acc = acc + jnp.where(ok, v, 0.0)
                cnt = jnp.minimum(pos + hi, ctx_len - 1) - jnp.maximum(pos - lo, 0) + 1
                t = acc / cnt.astype(F32)
                hcur = hbuf[base + k * ch:base + (k + 1) * ch, cols]
                dbuf[k * ch:(k + 1) * ch, :] = (t - hcur).astype(BF16)
            finish(grp)

    @pl.when(i > 0)
    def _latent():
        tile_in_batch = (i - 1) % tiles_per_batch
        row0 = tile_in_batch * (TM // GRID_W)
        col = lax.broadcasted_iota(jnp.int32, (GRID_W, 1), 0)
        n_rows = TM // GRID_W
        for grp in range(POOL_GROUPS):
            cols = slice(grp * gd, (grp + 1) * gd)
            wr, wc = POOL_WINDOWS_2D[grp]
            lo_r, hi_r = wr // 2, wr - 1 - wr // 2
            lo_c, hi_c = wc // 2, wc - 1 - wc // 2
            cnt_c = (jnp.minimum(col + hi_c, GRID_W - 1) - jnp.maximum(col - lo_c, 0) + 1).astype(F32)
            for rr in range(2 - lo_r, 2 + n_rows + hi_r):
                start = POOL_MARGIN + rr * GRID_W
                acc = jnp.zeros((GRID_W, gd), F32)
                for dc in range(-lo_c, hi_c + 1):
                    v = hbuf[start + dc:start + dc + GRID_W, cols]
                    ok = (col + dc >= 0) & (col + dc < GRID_W)
                    acc = acc + jnp.where(ok, v, 0.0)
                cpbuf[rr * GRID_W:(rr + 1) * GRID_W, :] = acc / cnt_c
            for r in range(n_rows):
                acc = jnp.zeros((GRID_W, gd), F32)
                cnt_r = jnp.zeros((GRID_W, 1), F32)
                for dr in range(-lo_r, hi_r + 1):
                    grow = row0 + r + dr
                    ok = (grow >= 0) & (grow < rows_per_batch)
                    v = cpbuf[(2 + r + dr) * GRID_W:(3 + r + dr) * GRID_W, :]
                    acc = acc + jnp.where(ok, v, 0.0)
                    cnt_r = cnt_r + jnp.where(ok, 1.0, 0.0)
                hcur = hbuf[base + r * GRID_W:base + (r + 1) * GRID_W, cols]
                dbuf[r * GRID_W:(r + 1) * GRID_W, :] = (acc / cnt_r - hcur).astype(BF16)
            finish(grp)


def _pool_layer(tl, xall, modtab, layer, norm_g, pool_w, pool_b, pool_scale):
    d = tl.d
    gd = d // POOL_GROUPS
    n_prev_blocks = TM // POOL_PREV
    n_next_blocks = TM // POOL_NEXT
    last_next = tl.n_tok // POOL_NEXT - 1
    kern = functools.partial(_pool_kernel, ctx_len=tl.ctx_len, tiles_per_batch=tl.tiles_per_batch,
                             rows_per_batch=tl.seq // GRID_W)
    vec = pl.BlockSpec((1, d), lambda i: (0, 0))
    buf_rows = 2 * POOL_MARGIN + POOL_PREV + TM + POOL_NEXT
    return pl.pallas_call(
        kern,
        grid=(tl.n_tiles,),
        in_specs=[
            pl.BlockSpec((POOL_PREV, d), lambda i: (jnp.maximum(i * n_prev_blocks - 1, 0), 0)),
            pl.BlockSpec((TM, d), lambda i: (i, 0)),
            pl.BlockSpec((POOL_NEXT, d), lambda i: (jnp.minimum((i + 1) * n_next_blocks, last_next), 0)),
            vec,
            tl.mod_spec(layer, 0), tl.mod_spec(layer, 1), tl.mod_spec(layer, 2),
            pl.BlockSpec((POOL_GROUPS, gd, gd), lambda i: (0, 0, 0)),
            vec, vec,
        ],
        out_specs=pl.BlockSpec((TM, d), lambda i: (i, 0)),
        out_shape=jax.ShapeDtypeStruct((tl.n_tok, d), F32),
        scratch_shapes=[
            pltpu.VMEM((buf_rows, d), F32),
            pltpu.VMEM((POOL_PREV + TM + POOL_NEXT, gd), F32),
            pltpu.VMEM((TM, gd), BF16),
        ],
        compiler_params=_cparams(("arbitrary",)),
        name="pool_layer",
    )(xall, xall, xall, norm_g.reshape(1, d), modtab, modtab, modtab, pool_w,
      pool_b.reshape(1, d), pool_scale.reshape(1, d))


def _router_kernel(x_ref, g_ref, sh_ref, sc_ref, rwt_ref, rb_ref, h_ref, r_ref):
    for r0 in range(0, x_ref.shape[0], NORMMOD_ROWS):
        rows = slice(r0, r0 + NORMMOD_ROWS)
        h_ref[rows, :] = _norm_mod(x_ref[rows, :], g_ref[...], sh_ref[...], sc_ref[...])
    logits = lax.dot_general(rwt_ref[...], h_ref[...], (((1,), (1,)), ((), ())),
                             precision=lax.Precision.HIGHEST, preferred_element_type=F32)
    scores = 1.0 / (1.0 + jnp.exp(-logits))
    sel = scores + rb_ref[...]
    tm = sel.shape[1]
    srow = [scores[e:e + 1, :] for e in range(N_EXPERTS)]
    vrow = [sel[e:e + 1, :] for e in range(N_EXPERTS)]
    best_g = jnp.zeros((1, tm), jnp.int32)
    best_s = None
    for grp in range(N_EXPERT_GROUPS):
        v = vrow[grp * EXPERTS_PER_GROUP:(grp + 1) * EXPERTS_PER_GROUP]
        gs = None
        for a in range(EXPERTS_PER_GROUP):
            for b in range(a + 1, EXPERTS_PER_GROUP):
                p = v[a] + v[b]
                gs = p if gs is None else jnp.maximum(gs, p)
        if best_s is None:
            best_s = gs
        else:
            upd = gs > best_s
            best_s = jnp.where(upd, gs, best_s)
            best_g = jnp.where(upd, grp, best_g)
    neg = jnp.full((1, tm), -jnp.inf, F32)
    masked = [jnp.where(best_g == (e // EXPERTS_PER_GROUP), vrow[e], neg) for e in range(N_EXPERTS)]
    m1, i1, s1 = masked[0], jnp.zeros((1, tm), jnp.int32), srow[0]
    for e in range(1, N_EXPERTS):
        upd = masked[e] > m1
        m1 = jnp.where(upd, masked[e], m1)
        i1 = jnp.where(upd, e, i1)
        s1 = jnp.where(upd, srow[e], s1)
    m2, i2, s2 = neg, jnp.full((1, tm), -1, jnp.int32), jnp.zeros((1, tm), F32)
    for e in range(N_EXPERTS):
        cand = i1 != e
        upd = cand & ((masked[e] > m2) | (i2 < 0))
        m2 = jnp.where(upd, masked[e], m2)
        i2 = jnp.where(upd, e, i2)
        s2 = jnp.where(upd, srow[e], s2)
    tot = s1 + s2
    w1, w2 = s1 / tot, s2 / tot
    first_lo = i1 < i2
    e_lo = jnp.where(first_lo, i1, i2).astype(F32)
    e_hi = jnp.where(first_lo, i2, i1).astype(F32)
    w_lo = jnp.where(first_lo, w1, w2)
    w_hi = jnp.where(first_lo, w2, w1)
    zero = jnp.zeros((1, tm), F32)
    r_ref[...] = jnp.concatenate([e_lo, e_hi, w_lo, w_hi, zero, zero, zero, zero], axis=0)


def _router(tl, xall, modtab, layer, norm_g, router_w, router_bias):
    d = tl.d
    vec = pl.BlockSpec((1, d), lambda i: (0, 0))
    return pl.pallas_call(
        _router_kernel,
        grid=(tl.n_tiles,),
        in_specs=[
            pl.BlockSpec((TM, d), lambda i: (i, 0)),
            vec, tl.mod_spec(layer, 3), tl.mod_spec(layer, 4),
            pl.BlockSpec((N_EXPERTS, d), lambda i: (0, 0)),
            pl.BlockSpec((N_EXPERTS, 1), lambda i: (0, 0)),
        ],
        out_specs=[
            pl.BlockSpec((TM, d), lambda i: (i, 0)),
            pl.BlockSpec((8, TM), lambda i: (0, i)),
        ],
        out_shape=[
            jax.ShapeDtypeStruct((tl.n_tok, d), F32),
            jax.ShapeDtypeStruct((8, tl.n_tok), F32),
        ],
        compiler_params=_cparams(("arbitrary",)),
        name="moe_router",
    )(xall, norm_g.reshape(1, d), modtab, modtab, router_w.T, router_bias.reshape(N_EXPERTS, 1))


_PAIR_ORDER = ((0, 1), (0, 2), (0, 3), (1, 3), (1, 2), (2, 3))
_PAIR_SLOT_A = (0, 0, 0, 1, 1, 3)
_PAIR_SLOT_B = (1, 2, 3, 3, 2, 2)
N_PAIR_CLASSES = N_EXPERT_GROUPS * len(_PAIR_ORDER)


def _moe_tiles(n_tok):
    return -(-(n_tok + N_PAIR_CLASSES * (MOE_ROWS - 1)) // MOE_ROWS) + 1


def _dispatch(route, n_tok):
    e_lo = route[0].astype(jnp.int32)
    e_hi = route[1].astype(jnp.int32)
    grp = e_lo // EXPERTS_PER_GROUP
    pair_of = [0] * (EXPERTS_PER_GROUP * EXPERTS_PER_GROUP)
    for j, (a, b) in enumerate(_PAIR_ORDER):
        pair_of[a * EXPERTS_PER_GROUP + b] = j
    local = (e_lo % EXPERTS_PER_GROUP) * EXPERTS_PER_GROUP + e_hi % EXPERTS_PER_GROUP
    cls = grp * len(_PAIR_ORDER) + jnp.asarray(pair_of, jnp.int32)[local]
    n_pair = len(_PAIR_ORDER)
    cls_ids = jnp.arange(N_PAIR_CLASSES, dtype=jnp.int32)
    slot_a_tab = (cls_ids // n_pair) * EXPERTS_PER_GROUP + jnp.asarray(_PAIR_SLOT_A, jnp.int32)[cls_ids % n_pair]
    slot_b_tab = (cls_ids // n_pair) * EXPERTS_PER_GROUP + jnp.asarray(_PAIR_SLOT_B, jnp.int32)[cls_ids % n_pair]
    onehot = (cls[:, None] == cls_ids[None, :]).astype(jnp.int32)
    csum = jnp.cumsum(onehot, axis=0)
    rank = jnp.sum(csum * onehot, axis=1) - 1
    cnt = csum[-1]
    ntile = (cnt + MOE_ROWS - 1) // MOE_ROWS
    tile_end = jnp.cumsum(ntile)
    tile_start = tile_end - ntile
    pos = (jnp.sum(onehot * tile_start[None, :], axis=1) * MOE_ROWS + rank).astype(jnp.int32)
    n_tiles = _moe_tiles(n_tok)
    n_used = tile_end[-1]
    tiles = jnp.arange(n_tiles, dtype=jnp.int32)
    tcls = jnp.sum((jnp.minimum(tiles, n_used - 1)[:, None] >= tile_end[None, :]).astype(jnp.int32), axis=1)
    ea = slot_a_tab[tcls]
    eb = slot_b_tab[tcls]
    prev = jnp.maximum(tiles - 1, 0)
    need_a = (tiles < n_used) & ((tiles == 0) | (ea != ea[prev]))
    need_b = (tiles < n_used) & ((tiles == 0) | (eb != eb[prev]))
    first_load = jnp.where(need_a, ea, jnp.where(need_b, eb, -1))
    big = jnp.int32(n_tiles)
    cand = jnp.where(need_a | need_b, tiles, big)
    next_ge = lax.cummin(cand[::-1])[::-1]
    next_gt = jnp.concatenate([next_ge[1:], big[None]])
    next_load = jnp.where(next_gt < big, first_load[jnp.minimum(next_gt, n_tiles - 1)], -1).astype(jnp.int32)
    pair = cls % n_pair
    a_is_lo = functools.reduce(
        jnp.logical_or, [pair == j for j in range(n_pair) if _PAIR_SLOT_A[j] == _PAIR_ORDER[j][0]])
    w_a = jnp.where(a_is_lo, route[2], route[3])
    w_b = jnp.where(a_is_lo, route[3], route[2])
    rows = jnp.stack([jnp.arange(n_tok, dtype=jnp.int32).astype(F32), w_a, w_b], axis=1)
    slot_rows = jnp.zeros((n_tiles * MOE_ROWS, 3), F32).at[pos].set(rows)
    slot_tok = slot_rows[:, 0].astype(jnp.int32)
    slot_w = slot_rows[:, 1:3]
    return (pos, slot_tok, slot_w, ea.astype(jnp.int32), eb.astype(jnp.int32), next_load,
            n_used.reshape(1).astype(jnp.int32))


def _row_copy(src_hbm, row, dst, r, sem):
    return pltpu.make_async_copy(src_hbm.at[pl.ds(row, 1), :], dst.at[pl.ds(r, 1), :], sem)


def _wait_rows(src_hbm, dst, sem):
    pltpu.make_async_copy(src_hbm.at[pl.ds(0, dst.shape[0]), :], dst, sem).wait()


def _ffn_kernel(ea_ref, eb_ref, nl_ref, nused_ref, stok_ref, h_hbm, sw_ref, wg_hbm, wu_hbm, wd_hbm, o_ref,
                xbuf, wgs, wus, wds, wga, wua, wda, wgb, wub, wdb, sem, wsem, *, layer):
    t = pl.program_id(0)
    slot = t % 2
    n_used = nused_ref[0]
    prev = jnp.maximum(t - 1, 0)
    need_a = (t < n_used) & ((t == 0) | (ea_ref[t] != ea_ref[prev]))
    need_b = (t < n_used) & ((t == 0) | (eb_ref[t] != eb_ref[prev]))

    def weight_copies(e):
        return (pltpu.make_async_copy(wg_hbm.at[layer, e], wgs, wsem.at[0]),
                pltpu.make_async_copy(wu_hbm.at[layer, e], wus, wsem.at[1]),
                pltpu.make_async_copy(wd_hbm.at[layer, e], wds, wsem.at[2]))

    def start_stage(e):
        for cp in weight_copies(e):
            cp.start()

    def take_staged(e, wg, wu, wd):
        for cp in weight_copies(e):
            cp.wait()
        wg[...] = wgs[...].astype(BF16)
        wu[...] = wus[...].astype(BF16)
        wd[...] = wds[...].astype(BF16)

    @pl.when(t == 0)
    def _prime():
        start_stage(ea_ref[0])
        for r in range(MOE_ROWS):
            _row_copy(h_hbm, stok_ref[r], xbuf.at[0], r, sem.at[0]).start()

    @pl.when(need_a)
    def _load_a():
        take_staged(ea_ref[t], wga, wua, wda)

        @pl.when(need_b)
        def _then_b():
            start_stage(eb_ref[t])

        @pl.when(jnp.logical_not(need_b) & (nl_ref[t] >= 0))
        def _then_next():
            start_stage(nl_ref[t])

    @pl.when(need_b)
    def _load_b():
        take_staged(eb_ref[t], wgb, wub, wdb)

        @pl.when(nl_ref[t] >= 0)
        def _then_next():
            start_stage(nl_ref[t])

    @pl.when(t <= n_used)
    def _wait():
        _wait_rows(h_hbm, xbuf.at[slot], sem.at[slot])

    @pl.when(t < n_used)
    def _compute():
        for r in range(MOE_ROWS):
            _row_copy(h_hbm, stok_ref[(t + 1) * MOE_ROWS + r], xbuf.at[1 - slot], r, sem.at[1 - slot]).start()
        xb = xbuf[slot].astype(BF16)
        sw = sw_ref[...]

        def expert(wg, wu, wd, gate):
            hg = jnp.dot(xb, wg[...], preferred_element_type=F32)
            hu = jnp.dot(xb, wu[...], preferred_element_type=F32)
            act = (_silu(hg) * hu * gate).astype(BF16)
            return jnp.dot(act, wd[...], preferred_element_type=F32)

        o_ref[...] = expert(wga, wua, wda, sw[:, 0:1]) + expert(wgb, wub, wdb, sw[:, 1:2])

    @pl.when(t >= n_used)
    def _idle():
        o_ref[...] = jnp.zeros_like(o_ref)


def _moe_ffn(n_tok, h2, slot_tok, slot_w, ea, eb, next_load, n_used, layer, wg, wu, wd):
    _, _, d, f = wg.shape
    n_tiles = _moe_tiles(n_tok)
    any_spec = pl.BlockSpec(memory_space=pl.ANY)
    wbufs = lambda dt: [pltpu.VMEM((d, f), dt), pltpu.VMEM((d, f), dt), pltpu.VMEM((f, d), dt)]
    grid_spec = pltpu.PrefetchScalarGridSpec(
        num_scalar_prefetch=5,
        grid=(n_tiles,),
        in_specs=[any_spec, pl.BlockSpec((MOE_ROWS, 2), lambda t, *_: (t, 0)), any_spec, any_spec, any_spec],
        out_specs=pl.BlockSpec((MOE_ROWS, d), lambda t, *_: (t, 0)),
        scratch_shapes=[pltpu.VMEM((2, MOE_ROWS, d), F32)] + wbufs(F32) + wbufs(BF16) + wbufs(BF16) + [
            pltpu.SemaphoreType.DMA((2,)),
            pltpu.SemaphoreType.DMA((3,)),
        ],
    )
    return pl.pallas_call(
        functools.partial(_ffn_kernel, layer=layer),
        grid_spec=grid_spec,
        out_shape=jax.ShapeDtypeStruct((n_tiles * MOE_ROWS, d), F32),
        compiler_params=_cparams(("arbitrary",)),
        name="moe_ffn",
    )(ea, eb, next_load, n_used, slot_tok, h2, slot_w, wg, wu, wd)


COMBINE_TM = 256


def _combine_kernel(p_ref, x_ref, gate_ref, fg_ref, y_hbm, o_ref, ybuf, sem, *, final):
    i = pl.program_id(0)
    n_i = pl.num_programs(0)
    slot = i % 2
    tm = COMBINE_TM

    def gather(tile, dst, dsem):
        def body(r, carry):
            _row_copy(y_hbm, p_ref[tile * tm + r], dst, r, dsem).start()
            return carry
        lax.fori_loop(0, tm, body, 0, unroll=8)

    @pl.when(i == 0)
    def _prime():
        gather(0, ybuf.at[0], sem.at[0])

    @pl.when(i + 1 < n_i)
    def _prefetch():
        gather(i + 1, ybuf.at[1 - slot], sem.at[1 - slot])

    _wait_rows(y_hbm, ybuf.at[slot], sem.at[slot])

    def body(g, carry):
        r0 = pl.multiple_of(g * 8, 8)
        x = x_ref[pl.ds(r0, 8), :] + gate_ref[...] * ybuf[slot, pl.ds(r0, 8), :]
        if final:
            ms = jnp.mean(x * x, axis=-1, keepdims=True)
            x = x * lax.rsqrt(ms + NORM_EPS) * fg_ref[...]
        o_ref[pl.ds(r0, 8), :] = x
        return carry
    lax.fori_loop(0, tm // 8, body, 0, unroll=4)


def _moe_combine(tl, xall, y_sorted, pos, modtab, layer, final_g, final):
    d = tl.d
    tm = COMBINE_TM
    per_tile = TM // tm
    skip_steps = tl.n_ctx // tm if final else 0
    grid_spec = pltpu.PrefetchScalarGridSpec(
        num_scalar_prefetch=1,
        grid=(tl.n_tok // tm,),
        in_specs=[
            pl.BlockSpec((tm, d), lambda i, p: (i, 0)),
            tl.mod_spec(layer, 5, tile_of=lambda i, p: i // per_tile),
            pl.BlockSpec((1, d), lambda i, p: (0, 0)),
            pl.BlockSpec(memory_space=pl.ANY),
        ],
        out_specs=pl.BlockSpec((tm, d), lambda i, p: (jnp.maximum(i - skip_steps, 0), 0)),
        scratch_shapes=[
            pltpu.VMEM((2, tm, d), F32),
            pltpu.SemaphoreType.DMA((2,)),
        ],
    )
    return pl.pallas_call(
        functools.partial(_combine_kernel, final=final),
        grid_spec=grid_spec,
        out_shape=jax.ShapeDtypeStruct((tl.n_tok - skip_steps * tm, d), F32),
        compiler_params=_cparams(("arbitrary",)),
        name="moe_combine",
    )(pos, xall, modtab, final_g.reshape(1, d), y_sorted)


def _moe_layer(tl, xall, modtab, layer, norm_g, router_w, router_bias, wg, wu, wd, final_g, final):
    h2, route = _router(tl, xall, modtab, layer, norm_g, router_w, router_bias)
    pos, slot_tok, slot_w, ea, eb, next_load, n_used = _dispatch(route, tl.n_tok)
    y_sorted = _moe_ffn(tl.n_tok, h2, slot_tok, slot_w, ea, eb, next_load, n_used, layer, wg, wu, wd)
    return _moe_combine(tl, xall, y_sorted, pos, modtab, layer, final_g, final)


INPROJ_TN = 1024
INPROJ_M_TILES = 8
OUTPROJ_TN = 512
CONV_HALO = 16


def _normmod_kernel(x_ref, g_ref, sh_ref, sc_ref, wdt_ref, h_ref, dt_ref):
    for r0 in range(0, x_ref.shape[0], NORMMOD_ROWS):
        rows = slice(r0, r0 + NORMMOD_ROWS)
        h_ref[rows, :] = _norm_mod(x_ref[rows, :], g_ref[...], sh_ref[...], sc_ref[...]).astype(BF16)
    dt_ref[...] = jnp.dot(h_ref[...], wdt_ref[...], preferred_element_type=F32)


def _normmod(tl, xall, modtab, layer, norm_g, w_dt):
    d = tl.d
    n_dt = w_dt.shape[1]
    return pl.pallas_call(
        _normmod_kernel,
        grid=(tl.n_tiles,),
        in_specs=[pl.BlockSpec((TM, d), lambda i: (i, 0)), pl.BlockSpec((1, d), lambda i: (0, 0)),
                  tl.mod_spec(layer, 0), tl.mod_spec(layer, 1), pl.BlockSpec((d, n_dt), lambda i: (0, 0))],
        out_specs=[pl.BlockSpec((TM, d), lambda i: (i, 0)), pl.BlockSpec((TM, n_dt), lambda i: (i, 0))],
        out_shape=[jax.ShapeDtypeStruct((tl.n_tok, d), BF16), jax.ShapeDtypeStruct((tl.n_tok, n_dt), F32)],
        compiler_params=_cparams(("arbitrary",)),
        name="ssd_normmod",
    )(xall, norm_g.reshape(1, d), modtab, modtab, w_dt)


def _inproj_kernel(h_ref, w_ref, wt_ref, p_ref, wb, *, n_main_blocks):
    n = pl.program_id(0)

    @pl.when((pl.program_id(1) == 0) & (n < n_main_blocks))
    def _main_columns():
        wb[...] = w_ref[...].astype(BF16)

    @pl.when((pl.program_id(1) == 0) & (n >= n_main_blocks))
    def _tail_columns():
        wb[...] = wt_ref[...]

    p_ref[...] = jnp.dot(h_ref[...], wb[...], preferred_element_type=F32).astype(BF16)


def _inproj(tl, h, in_w_all, j, n_main, w_tail):
    d = tl.d
    tn = INPROJ_TN
    n_main_blocks = n_main // tn
    n_tail_blocks = w_tail.shape[1] // tn
    tm = tl.n_tok // INPROJ_M_TILES if tl.n_tok % (16 * INPROJ_M_TILES) == 0 else TM
    return pl.pallas_call(
        functools.partial(_inproj_kernel, n_main_blocks=n_main_blocks),
        grid=(n_main_blocks + n_tail_blocks, tl.n_tok // tm),
        in_specs=[
            pl.BlockSpec((tm, d), lambda n, i: (i, 0)),
            pl.BlockSpec((None, d, tn), lambda n, i: (j, 0, jnp.minimum(n, n_main_blocks - 1))),
            pl.BlockSpec((d, tn), lambda n, i: (0, jnp.maximum(n - n_main_blocks, 0))),
        ],
        out_specs=pl.BlockSpec((tm, tn), lambda n, i: (i, n)),
        out_shape=jax.ShapeDtypeStruct((tl.n_tok, n_main + w_tail.shape[1]), BF16),
        scratch_shapes=[pltpu.VMEM((d, tn), BF16)],
        compiler_params=_cparams(("arbitrary", "arbitrary")),
        name="ssd_inproj",
    )(h, in_w_all, w_tail)


def _scan_kernel(xs_ref, xsp_ref, xsn_ref, bc_ref, bcp_ref, bcn_ref, dt_ref, cw_ref, cb_ref, hp_ref,
                 dsk_ref, y_ref, ubuf, bm, cm, cmf, bmt, acst, wt, state, *, ncc, ncl):
    q = SSD_CHUNK
    di = xs_ref.shape[1]
    n_grp = SSD_GROUPS
    blocks_per_group = di // n_grp // 128
    d = pl.program_id(0)
    c = pl.program_id(2)
    is_ctx = c < ncc
    fwd = d == 0
    tch = jnp.where(is_ctx, jnp.where(fwd, c, ncc - 1 - c), jnp.where(fwd, c - ncc, ncl - 1 - (c - ncc)))
    first = tch == 0
    last = tch == jnp.where(is_ctx, ncc - 1, ncl - 1)

    @pl.when(c == 0)
    def _reset():
        state[...] = jnp.zeros_like(state)

    hl = CONV_HALO
    ubuf[0:hl, 0:di] = jnp.where(first, 0.0, xsp_ref[...].astype(F32))
    ubuf[hl:hl + q, 0:di] = xs_ref[...].astype(F32)
    ubuf[hl + q:hl + q + hl, 0:di] = jnp.where(last, 0.0, xsn_ref[...].astype(F32))
    ubuf[0:hl, di:] = jnp.where(first, 0.0, bcp_ref[...].astype(F32))
    ubuf[hl:hl + q, di:] = bc_ref[...].astype(F32)
    ubuf[hl + q:hl + q + hl, di:] = jnp.where(last, 0.0, bcn_ref[...].astype(F32))

    def conv_block(col):
        acc = cb_ref[:, pl.ds(col, 128)]
        for k in range(SSD_CONV):
            off = hl + k - SSD_CONV // 2
            acc = acc + cw_ref[k:k + 1, pl.ds(col, 128)] * ubuf[off:off + q, pl.ds(col, 128)]
        return _silu(acc)

    hp = hp_ref[...]
    a_coef = -jnp.exp(hp[0:1])
    raw = dt_ref[...] + hp[1:2]
    dt = jnp.maximum(raw, 0.0) + jnp.log(1.0 + jnp.exp(-jnp.abs(raw)))
    a = dt * a_coef
    li = lax.broadcasted_iota(jnp.int32, (q, q), 0)
    si = lax.broadcasted_iota(jnp.int32, (q, q), 1)
    tri = jnp.where(fwd, li - si, si - li) >= 0
    trib = jnp.where(tri, 1.0, 0.0).astype(BF16)
    a_hi = a.astype(BF16)
    r1 = a - a_hi.astype(F32)
    a_mid = r1.astype(BF16)
    a_lo = (r1 - a_mid.astype(F32)).astype(BF16)
    acs = (jnp.dot(trib, a_hi, preferred_element_type=F32) + jnp.dot(trib, a_mid, preferred_element_type=F32)
           + jnp.dot(trib, a_lo, preferred_element_type=F32))
    tot = jnp.sum(a, axis=0, keepdims=True)
    eacs = jnp.exp(acs)
    etot = jnp.exp(tot)
    acst[...] = (acs - jnp.log(dt)).T
    wt[...] = (dt * jnp.exp(tot - acs)).T

    eye = jnp.where(li == si, 1.0, 0.0).astype(BF16)

    def b_body(j, carry):
        col = pl.multiple_of(j * 128, 128)
        vb = conv_block(di + col).astype(BF16)
        bm[:, pl.ds(col, 128)] = vb
        bmt[j] = lax.dot_general(eye, vb, (((1,), (1,)), ((), ())), preferred_element_type=F32)
        return carry
    lax.fori_loop(0, n_grp, b_body, 0, unroll=2)

    def c_body(j, carry):
        col = pl.multiple_of(j * 128, 128)
        v = conv_block(di + n_grp * SSD_STATE + col)
        cmf[:, pl.ds(col, 128)] = v
        cm[:, pl.ds(col, 128)] = v.astype(BF16)
        return carry
    lax.fori_loop(0, n_grp, c_body, 0, unroll=2)

    lane = lax.broadcasted_iota(jnp.int32, (q, 128), 1)
    left = lane < SSD_HEAD_DIM
    left_row = left[0:1, :]
    for pb in range(di // 128):
        grp = pb // blocks_per_group
        gl = slice(grp * SSD_STATE, (grp + 1) * SSD_STATE)
        pc = slice(pb * 128, (pb + 1) * 128)
        sc = slice((pb % blocks_per_group) * 128, (pb % blocks_per_group + 1) * 128)
        if pb % blocks_per_group == 0:
            cbm = lax.dot_general(cm[:, gl], bm[:, gl], (((1,), (1,)), ((), ())), preferred_element_type=F32)
        h0, h1 = 2 * pb, 2 * pb + 1
        ms, cs, bts = [], [], []
        for h in (h0, h1):
            colb = jnp.broadcast_to(acs[:, h:h + 1], (q, q))
            arg = jnp.where(tri, colb - acst[h:h + 1, :], -jnp.inf)
            ms.append((cbm * jnp.exp(arg)).astype(BF16))
            cs.append((cmf[:, gl] * jnp.broadcast_to(eacs[:, h:h + 1], (q, SSD_STATE))).astype(BF16))
            bts.append((bmt[grp] * wt[h:h + 1, :]).astype(BF16))
        xf = conv_block(pb * 128)
        xp = xf.astype(BF16)
        zero = jnp.zeros_like(xp)
        xe, xo = jnp.where(left, xp, zero), jnp.where(left, zero, xp)
        s_old = state[grp, :, sc]
        sb = s_old.astype(BF16)
        se, so = jnp.where(left, sb, zero), jnp.where(left, zero, sb)
        lhs = jnp.concatenate(ms + cs, axis=1)
        rhs = jnp.concatenate([xe, xo, se, so], axis=0)
        y = jnp.dot(lhs, rhs, preferred_element_type=F32) + xf * dsk_ref[:, pc]
        y_ref[:, pc] = y.astype(BF16)
        et = jnp.where(left_row, etot[:, h0:h0 + 1], etot[:, h1:h1 + 1])
        upd = jnp.dot(jnp.concatenate(bts, axis=1), jnp.concatenate([xe, xo], axis=0), preferred_element_type=F32)
        state[grp, :, sc] = s_old * et + upd


def _ssd_scan(tl, proj, dt2, conv_w, conv_b, head_params, dskip_lanes):
    n_tok = tl.n_tok
    q = SSD_CHUNK
    n_heads = dt2.shape[-1]
    di = n_heads * SSD_HEAD_DIM
    gn = SSD_GROUPS * SSD_STATE
    conv_ch = di + 2 * gn
    assert (2 * di) % (2 * gn) == 0 and (di // SSD_GROUPS) % 128 == 0
    ncc, ncl = tl.ctx_len // q, tl.seq // q
    bc_block0 = 2 * di // (2 * gn)
    hpc = q // CONV_HALO
    last_halo = n_tok // CONV_HALO - 1

    def blk(d, b, c):
        fwd = d == 0
        is_ctx = c < ncc
        t_ctx = jnp.where(fwd, c, ncc - 1 - c)
        t_lat = jnp.where(fwd, c - ncc, ncl - 1 - (c - ncc))
        return jnp.where(is_ctx, b * ncc + t_ctx, tl.bsz * ncc + b * ncl + t_lat)

    prev = lambda d, b, c: jnp.maximum(blk(d, b, c) * hpc - 1, 0)
    nxt = lambda d, b, c: jnp.minimum((blk(d, b, c) + 1) * hpc, last_halo)
    kern = functools.partial(_scan_kernel, ncc=ncc, ncl=ncl)
    return pl.pallas_call(
        kern,
        grid=(2, tl.bsz, ncc + ncl),
        in_specs=[
            pl.BlockSpec((q, di), lambda d, b, c: (blk(d, b, c), 1)),
            pl.BlockSpec((CONV_HALO, di), lambda d, b, c: (prev(d, b, c), 1)),
            pl.BlockSpec((CONV_HALO, di), lambda d, b, c: (nxt(d, b, c), 1)),
            pl.BlockSpec((q, 2 * gn), lambda d, b, c: (blk(d, b, c), bc_block0 + d)),
            pl.BlockSpec((CONV_HALO, 2 * gn), lambda d, b, c: (prev(d, b, c), bc_block0 + d)),
            pl.BlockSpec((CONV_HALO, 2 * gn), lambda d, b, c: (nxt(d, b, c), bc_block0 + d)),
            pl.BlockSpec((None, q, n_heads), lambda d, b, c: (d, blk(d, b, c), 0)),
            pl.BlockSpec((None, SSD_CONV, conv_ch), lambda d, b, c: (d, 0, 0)),
            pl.BlockSpec((None, 1, conv_ch), lambda d, b, c: (d, 0, 0)),
            pl.BlockSpec((None, 8, n_heads), lambda d, b, c: (d, 0, 0)),
            pl.BlockSpec((None, 1, di), lambda d, b, c: (d, 0, 0)),
        ],
        out_specs=pl.BlockSpec((None, q, di), lambda d, b, c: (d, blk(d, b, c), 0)),
        out_shape=jax.ShapeDtypeStruct((2, n_tok, di), BF16),
        scratch_shapes=[
            pltpu.VMEM((q + 2 * CONV_HALO, conv_ch), F32),
            pltpu.VMEM((q, gn), BF16),
            pltpu.VMEM((q, gn), BF16),
            pltpu.VMEM((q, gn), F32),
            pltpu.VMEM((SSD_GROUPS, SSD_STATE, q), F32),
            pltpu.VMEM((n_heads, q), F32),
            pltpu.VMEM((n_heads, q), F32),
            pltpu.VMEM((SSD_GROUPS, SSD_STATE, di // SSD_GROUPS), F32),
        ],
        compiler_params=_cparams(("arbitrary", "arbitrary", "arbitrary")),
        name="ssd_scan",
    )(proj, proj, proj, proj, proj, proj, dt2, conv_w, conv_b.reshape(2, 1, conv_ch), head_params, dskip_lanes)


def _outproj_kernel(y0_ref, y1_ref, z_ref, ng_ref, w_ref, x_ref, gate_ref, o_ref, ybuf):
    @pl.when(pl.program_id(1) == 0)
    def _prologue():
        gw = ybuf.shape[1] // SSD_GROUPS
        for grp in range(SSD_GROUPS):
            cols = slice(grp * gw, (grp + 1) * gw)
            y = (y0_ref[:, cols].astype(F32) + y1_ref[:, cols].astype(F32)) * _silu(z_ref[:, cols].astype(F32))
            ms = jnp.mean(y * y, axis=-1, keepdims=True)
            ybuf[:, cols] = (y * lax.rsqrt(ms + NORM_EPS) * ng_ref[:, cols]).astype(BF16)

    o_ref[...] = x_ref[...] + gate_ref[...] * jnp.dot(ybuf[...], w_ref[...], preferred_element_type=F32)


def _outproj(tl, xall, modtab, layer, y2, proj, norm_g, out_w):
    d = tl.d
    di = out_w.shape[0]
    tn = OUTPROJ_TN
    gate_spec = pl.BlockSpec(
        (None, 1, tn), lambda i, j: ((layer * COND_ROWS + tl.seg(i)) * N_MOD + 2, 0, j))
    return pl.pallas_call(
        _outproj_kernel,
        grid=(tl.n_tiles, d // tn),
        in_specs=[
            pl.BlockSpec((None, TM, di), lambda i, j: (0, i, 0)),
            pl.BlockSpec((None, TM, di), lambda i, j: (1, i, 0)),
            pl.BlockSpec((TM, di), lambda i, j: (i, 0)),
            pl.BlockSpec((1, di), lambda i, j: (0, 0)),
            pl.BlockSpec((di, tn), lambda i, j: (0, j)),
            pl.BlockSpec((TM, tn), lambda i, j: (i, j)),
            gate_spec,
        ],
        out_specs=pl.BlockSpec((TM, tn), lambda i, j: (i, j)),
        out_shape=jax.ShapeDtypeStruct((tl.n_tok, d), F32),
        scratch_shapes=[pltpu.VMEM((TM, di), BF16)],
        compiler_params=_cparams(("arbitrary", "arbitrary")),
        name="ssd_outproj",
    )(y2, y2, proj, norm_g.reshape(1, di), out_w, xall, modtab)


def _ssd_layer(tl, xall, modtab, layer, norm_g, in_w_all, j, conv_w, conv_b, a_log, dt_bias, d_skip, ssd_norm_g,
               out_w):
    n_heads = a_log.shape[-1]
    di = n_heads * SSD_HEAD_DIM
    gn = SSD_GROUPS * SSD_STATE
    dir_cols = 2 * gn + n_heads
    base0 = 2 * di
    base1 = base0 + dir_cols
    n_main = base0 + 2 * gn
    in_w = in_w_all[j]
    w_tail = in_w[:, base1:base1 + 2 * gn].astype(BF16)
    w_dt = jnp.concatenate(
        [in_w[:, base0 + 2 * gn:base0 + dir_cols], in_w[:, base1 + 2 * gn:base1 + dir_cols]], axis=1).astype(BF16)
    h, dt_raw = _normmod(tl, xall, modtab, layer, norm_g, w_dt)
    proj = _inproj(tl, h, in_w_all, j, n_main, w_tail)
    dt2 = dt_raw.reshape(tl.n_tok, 2, n_heads).transpose(1, 0, 2)
    head_params = jnp.zeros((2, 8, n_heads), F32).at[:, 0].set(a_log).at[:, 1].set(dt_bias)
    dskip_lanes = jnp.repeat(d_skip, SSD_HEAD_DIM, axis=-1).reshape(2, 1, di)
    y2 = _ssd_scan(tl, proj, dt2, conv_w, conv_b, head_params, dskip_lanes)
    return _outproj(tl, xall, modtab, layer, y2, proj, ssd_norm_g, out_w.astype(BF16))


def kernel(x, c, ctx, c_ctx, mod_w, mod_b, norm_mix_g, norm_ffn_g, pool_w, pool_b, pool_scale, ssd_in_w,
           ssd_conv_w, ssd_conv_b, ssd_a_log, ssd_dt_bias, ssd_d, ssd_norm_g, ssd_out_w, router_w, router_bias,
           moe_w_gate, moe_w_up, moe_w_down, final_norm_g):
    bsz, seq, d = x.shape
    ctx_len = ctx.shape[1]
    depth = mod_w.shape[0]
    assert bsz + 1 <= COND_ROWS
    tl = _Tiles(bsz, ctx_len, seq, d)
    cond = jnp.concatenate([c_ctx[None, :], c, jnp.zeros((COND_ROWS - 1 - bsz, d), F32)], axis=0)
    modtab = _adaln_table(cond, mod_w, mod_b)
    xall = jnp.concatenate([ctx.reshape(bsz * ctx_len, d), x.reshape(bsz * seq, d)], axis=0)
    for i in range(depth):
        j = i // 2
        if i % 2 == 0:
            xall = _pool_layer(tl, xall, modtab, i, norm_mix_g[i], pool_w[j], pool_b[j], pool_scale[j])
        else:
            xall = _ssd_layer(tl, xall, modtab, i, norm_mix_g[i], ssd_in_w, j, ssd_conv_w[j], ssd_conv_b[j],
                              ssd_a_log[j], ssd_dt_bias[j], ssd_d[j], ssd_norm_g[j], ssd_out_w[j])
        xall = _moe_layer(tl, xall, modtab, i, norm_ffn_g[i], router_w, router_bias,
                          moe_w_gate, moe_w_up, moe_w_down, final_norm_g, i == depth - 1)
    return xall.reshape(bsz, seq, d)
```

```python
import functools

import jax
import jax.numpy as jnp
from jax import lax
from jax.experimental import pallas as pl
from jax.experimental.pallas import tpu as pltpu

F32 = jnp.float32
BF16 = jnp.bfloat16

GRID_W = 64
NORM_EPS = 1e-6
N_MOD = 6
POOL_GROUPS = 4
POOL_WINDOWS_1D = (2, 4, 8, 16)
POOL_WINDOWS_2D = ((1, 2), (2, 2), (2, 4), (4, 4))
SSD_HEAD_DIM = 64
SSD_GROUPS = 8
SSD_STATE = 128
SSD_CONV = 4
SSD_CHUNK = 128
N_EXPERTS = 16
N_EXPERT_GROUPS = 4
EXPERTS_PER_GROUP = 4
TOP_K = 2

TM = 512
MOE_ROWS = 128
COND_ROWS = 8
NORMMOD_ROWS = 32
VMEM_LIMIT = 56 * 1024 * 1024


def _silu(v):
    return v / (1.0 + jnp.exp(-v))


def _norm_mod(x, g, shift, scale):
    ms = jnp.mean(x * x, axis=-1, keepdims=True)
    y = x * lax.rsqrt(ms + NORM_EPS) * g
    return y * (1.0 + scale) + shift


def _cparams(sem):
    return pltpu.CompilerParams(dimension_semantics=sem, vmem_limit_bytes=VMEM_LIMIT)


def _adaln_kernel(cond_ref, w_ref, b_ref, o_ref):
    s = _silu(cond_ref[...]).astype(BF16)
    o_ref[...] = jnp.dot(s, w_ref[...].astype(BF16), preferred_element_type=F32) + b_ref[...]


def _adaln_table(cond, mod_w, mod_b):
    depth, d, nd = mod_w.shape
    tn = 1024 if nd % 1024 == 0 else nd
    out = pl.pallas_call(
        _adaln_kernel,
        grid=(depth, nd // tn),
        in_specs=[
            pl.BlockSpec((COND_ROWS, d), lambda l, j: (0, 0)),
            pl.BlockSpec((None, d, tn), lambda l, j: (l, 0, j)),
            pl.BlockSpec((None, 1, tn), lambda l, j: (l, 0, j)),
        ],
        out_specs=pl.BlockSpec((None, COND_ROWS, tn), lambda l, j: (l, 0, j)),
        out_shape=jax.ShapeDtypeStruct((depth, COND_ROWS, nd), F32),
        compiler_params=_cparams(("arbitrary", "arbitrary")),
        name="adaln",
    )(cond, mod_w, mod_b.reshape(depth, 1, nd))
    return out.reshape(depth * COND_ROWS * N_MOD, 1, d)


class _Tiles:
    def __init__(self, bsz, ctx_len, seq, d):
        assert bsz * ctx_len == TM and seq % TM == 0
        self.bsz, self.ctx_len, self.seq, self.d = bsz, ctx_len, seq, d
        self.n_ctx = bsz * ctx_len
        self.n_tok = self.n_ctx + bsz * seq
        self.n_tiles = self.n_tok // TM
        self.tiles_per_batch = seq // TM

    def seg(self, i):
        return jnp.where(i == 0, 0, 1 + (i - 1) // self.tiles_per_batch)

    def mod_spec(self, layer, k, tile_of=lambda *idx: idx[0]):
        def imap(*idx):
            return ((layer * COND_ROWS + self.seg(tile_of(*idx))) * N_MOD + k, 0, 0)
        return pl.BlockSpec((None, 1, self.d), imap)


POOL_MARGIN = 8
POOL_PREV = 2 * GRID_W
POOL_NEXT = GRID_W


def _pool_kernel(xp_ref, xc_ref, xn_ref, g_ref, sh_ref, sc_ref, gate_ref, w_ref, b_ref, ps_ref,
                 o_ref, hbuf, cpbuf, dbuf, *, ctx_len, tiles_per_batch, rows_per_batch):
    i = pl.program_id(0)
    gd = w_ref.shape[-1]
    g, sh, sc = g_ref[...], sh_ref[...], sc_ref[...]
    base = POOL_MARGIN + POOL_PREV
    zeros_m = jnp.zeros((POOL_MARGIN, hbuf.shape[1]), F32)
    hbuf[0:POOL_MARGIN, :] = zeros_m
    hbuf[base + TM + POOL_NEXT:base + TM + POOL_NEXT + POOL_MARGIN, :] = zeros_m
    for src, dst0 in ((xp_ref, POOL_MARGIN), (xc_ref, base), (xn_ref, base + TM)):
        for r0 in range(0, src.shape[0], NORMMOD_ROWS):
            hbuf[dst0 + r0:dst0 + r0 + NORMMOD_ROWS, :] = _norm_mod(src[r0:r0 + NORMMOD_ROWS, :], g, sh, sc)

    def finish(grp):
        cols = slice(grp * gd, (grp + 1) * gd)
        y = jnp.dot(dbuf[...], w_ref[grp].astype(BF16), preferred_element_type=F32)
        y = (y + b_ref[:, cols]) * ps_ref[:, cols]
        o_ref[:, cols] = xc_ref[:, cols] + gate_ref[:, cols] * y

    @pl.when(i == 0)
    def _ctx():
        ch = GRID_W
        for grp in range(POOL_GROUPS):
            cols = slice(grp * gd, (grp + 1) * gd)
            w = POOL_WINDOWS_1D[grp]
            lo, hi = w // 2, w - 1 - w // 2
            for k in range(TM // ch):
                pos = (k * ch) % ctx_len + lax.broadcasted_iota(jnp.int32, (ch, 1), 0)
                acc = jnp.zeros((ch, gd), F32)
                for dd in range(-lo, hi + 1):
                    v = hbuf[base + k * ch + dd:base + k * ch + dd + ch, cols]
                    ok = (pos + dd >= 0) & (pos + dd < ctx_len)
                    acc = acc + jnp.where(ok, v, 0.0)
                cnt = jnp.minimum(pos + hi, ctx_len - 1) - jnp.maximum(pos - lo, 0) + 1
                t = acc / cnt.astype(F32)
                hcur = hbuf[base + k * ch:base + (k + 1) * ch, cols]
                dbuf[k * ch:(k + 1) * ch, :] = (t - hcur).astype(BF16)
            finish(grp)

    @pl.when(i > 0)
    def _latent():
        tile_in_batch = (i - 1) % tiles_per_batch
        row0 = tile_in_batch * (TM // GRID_W)
        col = lax.broadcasted_iota(jnp.int32, (GRID_W, 1), 0)
        n_rows = TM // GRID_W
        for grp in range(POOL_GROUPS):
            cols = slice(grp * gd, (grp + 1) * gd)
            wr, wc = POOL_WINDOWS_2D[grp]
            lo_r, hi_r = wr // 2, wr - 1 - wr // 2
            lo_c, hi_c = wc // 2, wc - 1 - wc // 2
            cnt_c = (jnp.minimum(col + hi_c, GRID_W - 1) - jnp.maximum(col - lo_c, 0) + 1).astype(F32)
            for rr in range(2 - lo_r, 2 + n_rows + hi_r):
                start = POOL_MARGIN + rr * GRID_W
                acc = jnp.zeros((GRID_W, gd), F32)
                for dc in range(-lo_c, hi_c + 1):
                    v = hbuf[start + dc:start + dc + GRID_W, cols]
                    ok = (col + dc >= 0) & (col + dc < GRID_W)
                    acc = acc + jnp.where(ok, v, 0.0)
                cpbuf[rr * GRID_W:(rr + 1) * GRID_W, :] = acc / cnt_c
            for r in range(n_rows):
                acc = jnp.zeros((GRID_W, gd), F32)
                cnt_r = jnp.zeros((GRID_W, 1), F32)
                for dr in range(-lo_r, hi_r + 1):
                    grow = row0 + r + dr
                    ok = (grow >= 0) & (grow < rows_per_batch)
                    v = cpbuf[(2 + r + dr) * GRID_W:(3 + r + dr) * GRID_W, :]
                    acc = acc + jnp.where(ok, v, 0.0)
                    cnt_r = cnt_r + jnp.where(ok, 1.0, 0.0)
                hcur = hbuf[base + r * GRID_W:base + (r + 1) * GRID_W, cols]
                dbuf[r * GRID_W:(r + 1) * GRID_W, :] = (acc / cnt_r - hcur).astype(BF16)
            finish(grp)


def _pool_layer(tl, xall, modtab, layer, norm_g, pool_w, pool_b, pool_scale):
    d = tl.d
    gd = d // POOL_GROUPS
    n_prev_blocks = TM // POOL_PREV
    n_next_blocks = TM // POOL_NEXT
    last_next = tl.n_tok // POOL_NEXT - 1
    kern = functools.partial(_pool_kernel, ctx_len=tl.ctx_len, tiles_per_batch=tl.tiles_per_batch,
                             rows_per_batch=tl.seq // GRID_W)
    vec = pl.BlockSpec((1, d), lambda i: (0, 0))
    buf_rows = 2 * POOL_MARGIN + POOL_PREV + TM + POOL_NEXT
    return pl.pallas_call(
        kern,
        grid=(tl.n_tiles,),
        in_specs=[
            pl.BlockSpec((POOL_PREV, d), lambda i: (jnp.maximum(i * n_prev_blocks - 1, 0), 0)),
            pl.BlockSpec((TM, d), lambda i: (i, 0)),
            pl.BlockSpec((POOL_NEXT, d), lambda i: (jnp.minimum((i + 1) * n_next_blocks, last_next), 0)),
            vec,
            tl.mod_spec(layer, 0), tl.mod_spec(layer, 1), tl.mod_spec(layer, 2),
            pl.BlockSpec((POOL_GROUPS, gd, gd), lambda i: (0, 0, 0)),
            vec, vec,
        ],
        out_specs=pl.BlockSpec((TM, d), lambda i: (i, 0)),
        out_shape=jax.ShapeDtypeStruct((tl.n_tok, d), F32),
        scratch_shapes=[
            pltpu.VMEM((buf_rows, d), F32),
            pltpu.VMEM((POOL_PREV + TM + POOL_NEXT, gd), F32),
            pltpu.VMEM((TM, gd), BF16),
        ],
        compiler_params=_cparams(("arbitrary",)),
        name="pool_layer",
    )(xall, xall, xall, norm_g.reshape(1, d), modtab, modtab, modtab, pool_w,
      pool_b.reshape(1, d), pool_scale.reshape(1, d))


def _router_kernel(x_ref, g_ref, sh_ref, sc_ref, rwt_ref, rb_ref, h_ref, r_ref):
    for r0 in range(0, x_ref.shape[0], NORMMOD_ROWS):
        rows = slice(r0, r0 + NORMMOD_ROWS)
        h_ref[rows, :] = _norm_mod(x_ref[rows, :], g_ref[...], sh_ref[...], sc_ref[...])
    logits = lax.dot_general(rwt_ref[...], h_ref[...], (((1,), (1,)), ((), ())),
                             precision=lax.Precision.HIGHEST, preferred_element_type=F32)
    scores = 1.0 / (1.0 + jnp.exp(-logits))
    sel = scores + rb_ref[...]
    tm = sel.shape[1]
    srow = [scores[e:e + 1, :] for e in range(N_EXPERTS)]
    vrow = [sel[e:e + 1, :] for e in range(N_EXPERTS)]
    best_g = jnp.zeros((1, tm), jnp.int32)
    best_s = None
    for grp in range(N_EXPERT_GROUPS):
        v = vrow[grp * EXPERTS_PER_GROUP:(grp + 1) * EXPERTS_PER_GROUP]
        gs = None
        for a in range(EXPERTS_PER_GROUP):
            for b in range(a + 1, EXPERTS_PER_GROUP):
                p = v[a] + v[b]
                gs = p if gs is None else jnp.maximum(gs, p)
        if best_s is None:
            best_s = gs
        else:
            upd = gs > best_s
            best_s = jnp.where(upd, gs, best_s)
            best_g = jnp.where(upd, grp, best_g)
    neg = jnp.full((1, tm), -jnp.inf, F32)
    masked = [jnp.where(best_g == (e // EXPERTS_PER_GROUP), vrow[e], neg) for e in range(N_EXPERTS)]
    m1, i1, s1 = masked[0], jnp.zeros((1, tm), jnp.int32), srow[0]
    for e in range(1, N_EXPERTS):
        upd = masked[e] > m1
        m1 = jnp.where(upd, masked[e], m1)
        i1 = jnp.where(upd, e, i1)
        s1 = jnp.where(upd, srow[e], s1)
    m2, i2, s2 = neg, jnp.full((1, tm), -1, jnp.int32), jnp.zeros((1, tm), F32)
    for e in range(N_EXPERTS):
        cand = i1 != e
        upd = cand & ((masked[e] > m2) | (i2 < 0))
        m2 = jnp.where(upd, masked[e], m2)
        i2 = jnp.where(upd, e, i2)
        s2 = jnp.where(upd, srow[e], s2)
    tot = s1 + s2
    w1, w2 = s1 / tot, s2 / tot
    first_lo = i1 < i2
    e_lo = jnp.where(first_lo, i1, i2).astype(F32)
    e_hi = jnp.where(first_lo, i2, i1).astype(F32)
    w_lo = jnp.where(first_lo, w1, w2)
    w_hi = jnp.where(first_lo, w2, w1)
    zero = jnp.zeros((1, tm), F32)
    r_ref[...] = jnp.concatenate([e_lo, e_hi, w_lo, w_hi, zero, zero, zero, zero], axis=0)


def _router(tl, xall, modtab, layer, norm_g, router_w, router_bias):
    d = tl.d
    vec = pl.BlockSpec((1, d), lambda i: (0, 0))
    return pl.pallas_call(
        _router_kernel,
        grid=(tl.n_tiles,),
        in_specs=[
            pl.BlockSpec((TM, d), lambda i: (i, 0)),
            vec, tl.mod_spec(layer, 3), tl.mod_spec(layer, 4),
            pl.BlockSpec((N_EXPERTS, d), lambda i: (0, 0)),
            pl.BlockSpec((N_EXPERTS, 1), lambda i: (0, 0)),
        ],
        out_specs=[
            pl.BlockSpec((TM, d), lambda i: (i, 0)),
            pl.BlockSpec((8, TM), lambda i: (0, i)),
        ],
        out_shape=[
            jax.ShapeDtypeStruct((tl.n_tok, d), F32),
            jax.ShapeDtypeStruct((8, tl.n_tok), F32),
        ],
        compiler_params=_cparams(("arbitrary",)),
        name="moe_router",
    )(xall, norm_g.reshape(1, d), modtab, modtab, router_w.T, router_bias.reshape(N_EXPERTS, 1))


_PAIR_ORDER = ((0, 1), (0, 2), (0, 3), (1, 3), (1, 2), (2, 3))
_PAIR_SLOT_A = (0, 0, 0, 1, 1, 3)
_PAIR_SLOT_B = (1, 2, 3, 3, 2, 2)
N_PAIR_CLASSES = N_EXPERT_GROUPS * len(_PAIR_ORDER)


def _moe_tiles(n_tok):
    return -(-(n_tok + N_PAIR_CLASSES * (MOE_ROWS - 1)) // MOE_ROWS) + 1


def _dispatch(route, n_tok):
    e_lo = route[0].astype(jnp.int32)
    e_hi = route[1].astype(jnp.int32)
    grp = e_lo // EXPERTS_PER_GROUP
    pair_of = [0] * (EXPERTS_PER_GROUP * EXPERTS_PER_GROUP)
    for j, (a, b) in enumerate(_PAIR_ORDER):
        pair_of[a * EXPERTS_PER_GROUP + b] = j
    local = (e_lo % EXPERTS_PER_GROUP) * EXPERTS_PER_GROUP + e_hi % EXPERTS_PER_GROUP
    cls = grp * len(_PAIR_ORDER) + jnp.asarray(pair_of, jnp.int32)[local]
    n_pair = len(_PAIR_ORDER)
    cls_ids = jnp.arange(N_PAIR_CLASSES, dtype=jnp.int32)
    slot_a_tab = (cls_ids // n_pair) * EXPERTS_PER_GROUP + jnp.asarray(_PAIR_SLOT_A, jnp.int32)[cls_ids % n_pair]
    slot_b_tab = (cls_ids // n_pair) * EXPERTS_PER_GROUP + jnp.asarray(_PAIR_SLOT_B, jnp.int32)[cls_ids % n_pair]
    onehot = (cls[:, None] == cls_ids[None, :]).astype(jnp.int32)
    csum = jnp.cumsum(onehot, axis=0)
    rank = jnp.sum(csum * onehot, axis=1) - 1
    cnt = csum[-1]
    ntile = (cnt + MOE_ROWS - 1) // MOE_ROWS
    tile_end = jnp.cumsum(ntile)
    tile_start = tile_end - ntile
    pos = (jnp.sum(onehot * tile_start[None, :], axis=1) * MOE_ROWS + rank).astype(jnp.int32)
    n_tiles = _moe_tiles(n_tok)
    n_used = tile_end[-1]
    tiles = jnp.arange(n_tiles, dtype=jnp.int32)
    tcls = jnp.sum((jnp.minimum(tiles, n_used - 1)[:, None] >= tile_end[None, :]).astype(jnp.int32), axis=1)
    ea = slot_a_tab[tcls]
    eb = slot_b_tab[tcls]
    prev = jnp.maximum(tiles - 1, 0)
    need_a = (tiles < n_used) & ((tiles == 0) | (ea != ea[prev]))
    need_b = (tiles < n_used) & ((tiles == 0) | (eb != eb[prev]))
    first_load = jnp.where(need_a, ea, jnp.where(need_b, eb, -1))
    big = jnp.int32(n_tiles)
    cand = jnp.where(need_a | need_b, tiles, big)
    next_ge = lax.cummin(cand[::-1])[::-1]
    next_gt = jnp.concatenate([next_ge[1:], big[None]])
    next_load = jnp.where(next_gt < big, first_load[jnp.minimum(next_gt, n_tiles - 1)], -1).astype(jnp.int32)
    pair = cls % n_pair
    a_is_lo = functools.reduce(
        jnp.logical_or, [pair == j for j in range(n_pair) if _PAIR_SLOT_A[j] == _PAIR_ORDER[j][0]])
    w_a = jnp.where(a_is_lo, route[2], route[3])
    w_b = jnp.where(a_is_lo, route[3], route[2])
    rows = jnp.stack([jnp.arange(n_tok, dtype=jnp.int32).astype(F32), w_a, w_b], axis=1)
    slot_rows = jnp.zeros((n_tiles * MOE_ROWS, 3), F32).at[pos].set(rows)
    slot_tok = slot_rows[:, 0].astype(jnp.int32)
    slot_w = slot_rows[:, 1:3]
    return (pos, slot_tok, slot_w, ea.astype(jnp.int32), eb.astype(jnp.int32), next_load,
            n_used.reshape(1).astype(jnp.int32))


def _row_copy(src_hbm, row, dst, r, sem):
    return pltpu.make_async_copy(src_hbm.at[pl.ds(row, 1), :], dst.at[pl.ds(r, 1), :], sem)


def _wait_rows(src_hbm, dst, sem):
    pltpu.make_async_copy(src_hbm.at[pl.ds(0, dst.shape[0]), :], dst, sem).wait()


def _ffn_kernel(ea_ref, eb_ref, nl_ref, nused_ref, stok_ref, h_hbm, sw_ref, wg_hbm, wu_hbm, wd_hbm, o_ref,
                xbuf, wgs, wus, wds, wga, wua, wda, wgb, wub, wdb, sem, wsem, *, layer):
    t = pl.program_id(0)
    slot = t % 2
    n_used = nused_ref[0]
    prev = jnp.maximum(t - 1, 0)
    need_a = (t < n_used) & ((t == 0) | (ea_ref[t] != ea_ref[prev]))
    need_b = (t < n_used) & ((t == 0) | (eb_ref[t] != eb_ref[prev]))

    def weight_copies(e):
        return (pltpu.make_async_copy(wg_hbm.at[layer, e], wgs, wsem.at[0]),
                pltpu.make_async_copy(wu_hbm.at[layer, e], wus, wsem.at[1]),
                pltpu.make_async_copy(wd_hbm.at[layer, e], wds, wsem.at[2]))

    def start_stage(e):
        for cp in weight_copies(e):
            cp.start()

    def take_staged(e, wg, wu, wd):
        for cp in weight_copies(e):
            cp.wait()
        wg[...] = wgs[...].astype(BF16)
        wu[...] = wus[...].astype(BF16)
        wd[...] = wds[...].astype(BF16)

    @pl.when(t == 0)
    def _prime():
        start_stage(ea_ref[0])
        for r in range(MOE_ROWS):
            _row_copy(h_hbm, stok_ref[r], xbuf.at[0], r, sem.at[0]).start()

    @pl.when(need_a)
    def _load_a():
        take_staged(ea_ref[t], wga, wua, wda)

        @pl.when(need_b)
        def _then_b():
            start_stage(eb_ref[t])

        @pl.when(jnp.logical_not(need_b) & (nl_ref[t] >= 0))
        def _then_next():
            start_stage(nl_ref[t])

    @pl.when(need_b)
    def _load_b():
        take_staged(eb_ref[t], wgb, wub, wdb)

        @pl.when(nl_ref[t] >= 0)
        def _then_next():
            start_stage(nl_ref[t])

    @pl.when(t <= n_used)
    def _wait():
        _wait_rows(h_hbm, xbuf.at[slot], sem.at[slot])

    @pl.when(t < n_used)
    def _compute():
        for r in range(MOE_ROWS):
            _row_copy(h_hbm, stok_ref[(t + 1) * MOE_ROWS + r], xbuf.at[1 - slot], r, sem.at[1 - slot]).start()
        xb = xbuf[slot].astype(BF16)
        sw = sw_ref[...]

        def expert(wg, wu, wd, gate):
            hg = jnp.dot(xb, wg[...], preferred_element_type=F32)
            hu = jnp.dot(xb, wu[...], preferred_element_type=F32)
            act = (_silu(hg) * hu * gate).astype(BF16)
            return jnp.dot(act, wd[...], preferred_element_type=F32)

        o_ref[...] = expert(wga, wua, wda, sw[:, 0:1]) + expert(wgb, wub, wdb, sw[:, 1:2])

    @pl.when(t >= n_used)
    def _idle():
        o_ref[...] = jnp.zeros_like(o_ref)


def _moe_ffn(n_tok, h2, slot_tok, slot_w, ea, eb, next_load, n_used, layer, wg, wu, wd):
    _, _, d, f = wg.shape
    n_tiles = _moe_tiles(n_tok)
    any_spec = pl.BlockSpec(memory_space=pl.ANY)
    wbufs = lambda dt: [pltpu.VMEM((d, f), dt), pltpu.VMEM((d, f), dt), pltpu.VMEM((f, d), dt)]
    grid_spec = pltpu.PrefetchScalarGridSpec(
        num_scalar_prefetch=5,
        grid=(n_tiles,),
        in_specs=[any_spec, pl.BlockSpec((MOE_ROWS, 2), lambda t, *_: (t, 0)), any_spec, any_spec, any_spec],
        out_specs=pl.BlockSpec((MOE_ROWS, d), lambda t, *_: (t, 0)),
        scratch_shapes=[pltpu.VMEM((2, MOE_ROWS, d), F32)] + wbufs(F32) + wbufs(BF16) + wbufs(BF16) + [
            pltpu.SemaphoreType.DMA((2,)),
            pltpu.SemaphoreType.DMA((3,)),
        ],
    )
    return pl.pallas_call(
        functools.partial(_ffn_kernel, layer=layer),
        grid_spec=grid_spec,
        out_shape=jax.ShapeDtypeStruct((n_tiles * MOE_ROWS, d), F32),
        compiler_params=_cparams(("arbitrary",)),
        name="moe_ffn",
    )(ea, eb, next_load, n_used, slot_tok, h2, slot_w, wg, wu, wd)


COMBINE_TM = 256


def _combine_kernel(p_ref, x_ref, gate_ref, fg_ref, y_hbm, o_ref, ybuf, sem, *, final):
    i = pl.program_id(0)
    n_i = pl.num_programs(0)
    slot = i % 2
    tm = COMBINE_TM

    def gather(tile, dst, dsem):
        def body(r, carry):
            _row_copy(y_hbm, p_ref[tile * tm + r], dst, r, dsem).start()
            return carry
        lax.fori_loop(0, tm, body, 0, unroll=8)

    @pl.when(i == 0)
    def _prime():
        gather(0, ybuf.at[0], sem.at[0])

    @pl.when(i + 1 < n_i)
    def _prefetch():
        gather(i + 1, ybuf.at[1 - slot], sem.at[1 - slot])

    _wait_rows(y_hbm, ybuf.at[slot], sem.at[slot])

    def body(g, carry):
        r0 = pl.multiple_of(g * 8, 8)
        x = x_ref[pl.ds(r0, 8), :] + gate_ref[...] * ybuf[slot, pl.ds(r0, 8), :]
        if final:
            ms = jnp.mean(x * x, axis=-1, keepdims=True)
            x = x * lax.rsqrt(ms + NORM_EPS) * fg_ref[...]
        o_ref[pl.ds(r0, 8), :] = x
        return carry
    lax.fori_loop(0, tm // 8, body, 0, unroll=4)


def _moe_combine(tl, xall, y_sorted, pos, modtab, layer, final_g, final):
    d = tl.d
    tm = COMBINE_TM
    per_tile = TM // tm
    skip_steps = tl.n_ctx // tm if final else 0
    grid_spec = pltpu.PrefetchScalarGridSpec(
        num_scalar_prefetch=1,
        grid=(tl.n_tok // tm,),
        in_specs=[
            pl.BlockSpec((tm, d), lambda i, p: (i, 0)),
            tl.mod_spec(layer, 5, tile_of=lambda i, p: i // per_tile),
            pl.BlockSpec((1, d), lambda i, p: (0, 0)),
            pl.BlockSpec(memory_space=pl.ANY),
        ],
        out_specs=pl.BlockSpec((tm, d), lambda i, p: (jnp.maximum(i - skip_steps, 0), 0)),
        scratch_shapes=[
            pltpu.VMEM((2, tm, d), F32),
            pltpu.SemaphoreType.DMA((2,)),
        ],
    )
    return pl.pallas_call(
        functools.partial(_combine_kernel, final=final),
        grid_spec=grid_spec,
        out_shape=jax.ShapeDtypeStruct((tl.n_tok - skip_steps * tm, d), F32),
        compiler_params=_cparams(("arbitrary",)),
        name="moe_combine",
    )(pos, xall, modtab, final_g.reshape(1, d), y_sorted)


def _moe_layer(tl, xall, modtab, layer, norm_g, router_w, router_bias, wg, wu, wd, final_g, final):
    h2, route = _router(tl, xall, modtab, layer, norm_g, router_w, router_bias)
    pos, slot_tok, slot_w, ea, eb, next_load, n_used = _dispatch(route, tl.n_tok)
    y_sorted = _moe_ffn(tl.n_tok, h2, slot_tok, slot_w, ea, eb, next_load, n_used, layer, wg, wu, wd)
    return _moe_combine(tl, xall, y_sorted, pos, modtab, layer, final_g, final)


INPROJ_TN = 1024
INPROJ_M_TILES = 8
OUTPROJ_TN = 512
CONV_HALO = 16


def _normmod_kernel(x_ref, g_ref, sh_ref, sc_ref, wdt_ref, h_ref, dt_ref):
    for r0 in range(0, x_ref.shape[0], NORMMOD_ROWS):
        rows = slice(r0, r0 + NORMMOD_ROWS)
        h_ref[rows, :] = _norm_mod(x_ref[rows, :], g_ref[...], sh_ref[...], sc_ref[...]).astype(BF16)
    dt_ref[...] = jnp.dot(h_ref[...], wdt_ref[...], preferred_element_type=F32)


def _normmod(tl, xall, modtab, layer, norm_g, w_dt):
    d = tl.d
    n_dt = w_dt.shape[1]
    return pl.pallas_call(
        _normmod_kernel,
        grid=(tl.n_tiles,),
        in_specs=[pl.BlockSpec((TM, d), lambda i: (i, 0)), pl.BlockSpec((1, d), lambda i: (0, 0)),
                  tl.mod_spec(layer, 0), tl.mod_spec(layer, 1), pl.BlockSpec((d, n_dt), lambda i: (0, 0))],
        out_specs=[pl.BlockSpec((TM, d), lambda i: (i, 0)), pl.BlockSpec((TM, n_dt), lambda i: (i, 0))],
        out_shape=[jax.ShapeDtypeStruct((tl.n_tok, d), BF16), jax.ShapeDtypeStruct((tl.n_tok, n_dt), F32)],
        compiler_params=_cparams(("arbitrary",)),
        name="ssd_normmod",
    )(xall, norm_g.reshape(1, d), modtab, modtab, w_dt)


def _inproj_kernel(h_ref, w_ref, wt_ref, p_ref, wb, *, n_main_blocks):
    n = pl.program_id(0)

    @pl.when((pl.program_id(1) == 0) & (n < n_main_blocks))
    def _main_columns():
        wb[...] = w_ref[...].astype(BF16)

    @pl.when((pl.program_id(1) == 0) & (n >= n_main_blocks))
    def _tail_columns():
        wb[...] = wt_ref[...]

    p_ref[...] = jnp.dot(h_ref[...], wb[...], preferred_element_type=F32).astype(BF16)


def _inproj(tl, h, in_w_all, j, n_main, w_tail):
    d = tl.d
    tn = INPROJ_TN
    n_main_blocks = n_main // tn
    n_tail_blocks = w_tail.shape[1] // tn
    tm = tl.n_tok // INPROJ_M_TILES if tl.n_tok % (16 * INPROJ_M_TILES) == 0 else TM
    return pl.pallas_call(
        functools.partial(_inproj_kernel, n_main_blocks=n_main_blocks),
        grid=(n_main_blocks + n_tail_blocks, tl.n_tok // tm),
        in_specs=[
            pl.BlockSpec((tm, d), lambda n, i: (i, 0)),
            pl.BlockSpec((None, d, tn), lambda n, i: (j, 0, jnp.minimum(n, n_main_blocks - 1))),
            pl.BlockSpec((d, tn), lambda n, i: (0, jnp.maximum(n - n_main_blocks, 0))),
        ],
        out_specs=pl.BlockSpec((tm, tn), lambda n, i: (i, n)),
        out_shape=jax.ShapeDtypeStruct((tl.n_tok, n_main + w_tail.shape[1]), BF16),
        scratch_shapes=[pltpu.VMEM((d, tn), BF16)],
        compiler_params=_cparams(("arbitrary", "arbitrary")),
        name="ssd_inproj",
    )(h, in_w_all, w_tail)


def _scan_kernel(xs_ref, xsp_ref, xsn_ref, bc_ref, bcp_ref, bcn_ref, dt_ref, cw_ref, cb_ref, hp_ref,
                 dsk_ref, y_ref, ubuf, bm, cm, cmf, bmt, acst, wt, state, *, ncc, ncl):
    q = SSD_CHUNK
    di = xs_ref.shape[1]
    n_grp = SSD_GROUPS
    blocks_per_group = di // n_grp // 128
    d = pl.program_id(0)
    c = pl.program_id(2)
    is_ctx = c < ncc
    fwd = d == 0
    tch = jnp.where(is_ctx, jnp.where(fwd, c, ncc - 1 - c), jnp.where(fwd, c - ncc, ncl - 1 - (c - ncc)))
    first = tch == 0
    last = tch == jnp.where(is_ctx, ncc - 1, ncl - 1)

    @pl.when(c == 0)
    def _reset():
        state[...] = jnp.zeros_like(state)

    hl = CONV_HALO
    ubuf[0:hl, 0:di] = jnp.where(first, 0.0, xsp_ref[...].astype(F32))
    ubuf[hl:hl + q, 0:di] = xs_ref[...].astype(F32)
    ubuf[hl + q:hl + q + hl, 0:di] = jnp.where(last, 0.0, xsn_ref[...].astype(F32))
    ubuf[0:hl, di:] = jnp.where(first, 0.0, bcp_ref[...].astype(F32))
    ubuf[hl:hl + q, di:] = bc_ref[...].astype(F32)
    ubuf[hl + q:hl + q + hl, di:] = jnp.where(last, 0.0, bcn_ref[...].astype(F32))

    def conv_block(col):
        acc = cb_ref[:, pl.ds(col, 128)]
        for k in range(SSD_CONV):
            off = hl + k - SSD_CONV // 2
            acc = acc + cw_ref[k:k + 1, pl.ds(col, 128)] * ubuf[off:off + q, pl.ds(col, 128)]
        return _silu(acc)

    hp = hp_ref[...]
    a_coef = -jnp.exp(hp[0:1])
    raw = dt_ref[...] + hp[1:2]
    dt = jnp.maximum(raw, 0.0) + jnp.log(1.0 + jnp.exp(-jnp.abs(raw)))
    a = dt * a_coef
    li = lax.broadcasted_iota(jnp.int32, (q, q), 0)
    si = lax.broadcasted_iota(jnp.int32, (q, q), 1)
    tri = jnp.where(fwd, li - si, si - li) >= 0
    trib = jnp.where(tri, 1.0, 0.0).astype(BF16)
    a_hi = a.astype(BF16)
    r1 = a - a_hi.astype(F32)
    a_mid = r1.astype(BF16)
    a_lo = (r1 - a_mid.astype(F32)).astype(BF16)
    acs = (jnp.dot(trib, a_hi, preferred_element_type=F32) + jnp.dot(trib, a_mid, preferred_element_type=F32)
           + jnp.dot(trib, a_lo, preferred_element_type=F32))
    tot = jnp.sum(a, axis=0, keepdims=True)
    eacs = jnp.exp(acs)
    etot = jnp.exp(tot)
    acst[...] = (acs - jnp.log(dt)).T
    wt[...] = (dt * jnp.exp(tot - acs)).T

    eye = jnp.where(li == si, 1.0, 0.0).astype(BF16)

    for j in range(n_grp):
        col = j * 128
        vb = conv_block(di + col).astype(BF16)
        bm[:, col:col + 128] = vb
        bmt[j] = lax.dot_general(eye, vb, (((1,), (1,)), ((), ())), preferred_element_type=F32)
        v = conv_block(di + n_grp * SSD_STATE + col)
        cmf[:, col:col + 128] = v
        cm[:, col:col + 128] = v.astype(BF16)

    lane = lax.broadcasted_iota(jnp.int32, (q, 128), 1)
    left = lane < SSD_HEAD_DIM
    left_row = left[0:1, :]
    for pb in range(di // 128):
        grp = pb // blocks_per_group
        gl = slice(grp * SSD_STATE, (grp + 1) * SSD_STATE)
        pc = slice(pb * 128, (pb + 1) * 128)
        sc = slice((pb % blocks_per_group) * 128, (pb % blocks_per_group + 1) * 128)
        if pb % blocks_per_group == 0:
            cbm = lax.dot_general(cm[:, gl], bm[:, gl], (((1,), (1,)), ((), ())), preferred_element_type=F32)
        h0, h1 = 2 * pb, 2 * pb + 1
        ms, cs, bts = [], [], []
        for h in (h0, h1):
            colb = jnp.broadcast_to(acs[:, h:h + 1], (q, q))
            arg = jnp.where(tri, colb - acst[h:h + 1, :], -jnp.inf)
            ms.append((cbm * jnp.exp(arg)).astype(BF16))
            cs.append((cmf[:, gl] * jnp.broadcast_to(eacs[:, h:h + 1], (q, SSD_STATE))).astype(BF16))
            bts.append((bmt[grp] * wt[h:h + 1, :]).astype(BF16))
        xf = conv_block(pb * 128)
        xp = xf.astype(BF16)
        zero = jnp.zeros_like(xp)
        xe, xo = jnp.where(left, xp, zero), jnp.where(left, zero, xp)
        s_old = state[grp, :, sc]
        sb = s_old.astype(BF16)
        se, so = jnp.where(left, sb, zero), jnp.where(left, zero, sb)
        lhs = jnp.concatenate(ms + cs, axis=1)
        rhs = jnp.concatenate([xe, xo, se, so], axis=0)
        y = jnp.dot(lhs, rhs, preferred_element_type=F32) + xf * dsk_ref[:, pc]
        y_ref[:, pc] = y.astype(BF16)
        et = jnp.where(left_row, etot[:, h0:h0 + 1], etot[:, h1:h1 + 1])
        upd = jnp.dot(jnp.concatenate(bts, axis=1), jnp.concatenate([xe, xo], axis=0), preferred_element_type=F32)
        state[grp, :, sc] = s_old * et + upd


def _ssd_scan(tl, proj, dt2, conv_w, conv_b, head_params, dskip_lanes):
    n_tok = tl.n_tok
    q = SSD_CHUNK
    n_heads = dt2.shape[-1]
    di = n_heads * SSD_HEAD_DIM
    gn = SSD_GROUPS * SSD_STATE
    conv_ch = di + 2 * gn
    assert (2 * di) % (2 * gn) == 0 and (di // SSD_GROUPS) % 128 == 0
    ncc, ncl = tl.ctx_len // q, tl.seq // q
    bc_block0 = 2 * di // (2 * gn)
    hpc = q // CONV_HALO
    last_halo = n_tok // CONV_HALO - 1

    def blk(d, b, c):
        fwd = d == 0
        is_ctx = c < ncc
        t_ctx = jnp.where(fwd, c, ncc - 1 - c)
        t_lat = jnp.where(fwd, c - ncc, ncl - 1 - (c - ncc))
        return jnp.where(is_ctx, b * ncc + t_ctx, tl.bsz * ncc + b * ncl + t_lat)

    prev = lambda d, b, c: jnp.maximum(blk(d, b, c) * hpc - 1, 0)
    nxt = lambda d, b, c: jnp.minimum((blk(d, b, c) + 1) * hpc, last_halo)
    kern = functools.partial(_scan_kernel, ncc=ncc, ncl=ncl)
    return pl.pallas_call(
        kern,
        grid=(2, tl.bsz, ncc + ncl),
        in_specs=[
            pl.BlockSpec((q, di), lambda d, b, c: (blk(d, b, c), 1)),
            pl.BlockSpec((CONV_HALO, di), lambda d, b, c: (prev(d, b, c), 1)),
            pl.BlockSpec((CONV_HALO, di), lambda d, b, c: (nxt(d, b, c), 1)),
            pl.BlockSpec((q, 2 * gn), lambda d, b, c: (blk(d, b, c), bc_block0 + d)),
            pl.BlockSpec((CONV_HALO, 2 * gn), lambda d, b, c: (prev(d, b, c), bc_block0 + d)),
            pl.BlockSpec((CONV_HALO, 2 * gn), lambda d, b, c: (nxt(d, b, c), bc_block0 + d)),
            pl.BlockSpec((None, q, n_heads), lambda d, b, c: (d, blk(d, b, c), 0)),
            pl.BlockSpec((None, SSD_CONV, conv_ch), lambda d, b, c: (d, 0, 0)),
            pl.BlockSpec((None, 1, conv_ch), lambda d, b, c: (d, 0, 0)),
            pl.BlockSpec((None, 8, n_heads), lambda d, b, c: (d, 0, 0)),
            pl.BlockSpec((None, 1, di), lambda d, b, c: (d, 0, 0)),
        ],
        out_specs=pl.BlockSpec((None, q, di), lambda d, b, c: (d, blk(d, b, c), 0)),
        out_shape=jax.ShapeDtypeStruct((2, n_tok, di), BF16),
        scratch_shapes=[
            pltpu.VMEM((q + 2 * CONV_HALO, conv_ch), F32),
            pltpu.VMEM((q, gn), BF16),
            pltpu.VMEM((q, gn), BF16),
            pltpu.VMEM((q, gn), F32),
            pltpu.VMEM((SSD_GROUPS, SSD_STATE, q), F32),
            pltpu.VMEM((n_heads, q), F32),
            pltpu.VMEM((n_heads, q), F32),
            pltpu.VMEM((SSD_GROUPS, SSD_STATE, di // SSD_GROUPS), F32),
        ],
        compiler_params=_cparams(("arbitrary", "arbitrary", "arbitrary")),
        name="ssd_scan",
    )(proj, proj, proj, proj, proj, proj, dt2, conv_w, conv_b.reshape(2, 1, conv_ch), head_params, dskip_lanes)


def _outproj_kernel(y0_ref, y1_ref, z_ref, ng_ref, w_ref, x_ref, gate_ref, o_ref, ybuf):
    @pl.when(pl.program_id(1) == 0)
    def _prologue():
        gw = ybuf.shape[1] // SSD_GROUPS
        for grp in range(SSD_GROUPS):
            cols = slice(grp * gw, (grp + 1) * gw)
            y = (y0_ref[:, cols].astype(F32) + y1_ref[:, cols].astype(F32)) * _silu(z_ref[:, cols].astype(F32))
            ms = jnp.mean(y * y, axis=-1, keepdims=True)
            ybuf[:, cols] = (y * lax.rsqrt(ms + NORM_EPS) * ng_ref[:, cols]).astype(BF16)

    o_ref[...] = x_ref[...] + gate_ref[...] * jnp.dot(ybuf[...], w_ref[...], preferred_element_type=F32)


def _outproj(tl, xall, modtab, layer, y2, proj, norm_g, out_w):
    d = tl.d
    di = out_w.shape[0]
    tn = OUTPROJ_TN
    gate_spec = pl.BlockSpec(
        (None, 1, tn), lambda i, j: ((layer * COND_ROWS + tl.seg(i)) * N_MOD + 2, 0, j))
    return pl.pallas_call(
        _outproj_kernel,
        grid=(tl.n_tiles, d // tn),
        in_specs=[
            pl.BlockSpec((None, TM, di), lambda i, j: (0, i, 0)),
            pl.BlockSpec((None, TM, di), lambda i, j: (1, i, 0)),
            pl.BlockSpec((TM, di), lambda i, j: (i, 0)),
            pl.BlockSpec((1, di), lambda i, j: (0, 0)),
            pl.BlockSpec((di, tn), lambda i, j: (0, j)),
            pl.BlockSpec((TM, tn), lambda i, j: (i, j)),
            gate_spec,
        ],
        out_specs=pl.BlockSpec((TM, tn), lambda i, j: (i, j)),
        out_shape=jax.ShapeDtypeStruct((tl.n_tok, d), F32),
        scratch_shapes=[pltpu.VMEM((TM, di), BF16)],
        compiler_params=_cparams(("arbitrary", "arbitrary")),
        name="ssd_outproj",
    )(y2, y2, proj, norm_g.reshape(1, di), out_w, xall, modtab)


def _ssd_layer(tl, xall, modtab, layer, norm_g, in_w_all, j, conv_w, conv_b, a_log, dt_bias, d_skip, ssd_norm_g,
               out_w):
    n_heads = a_log.shape[-1]
    di = n_heads * SSD_HEAD_DIM
    gn = SSD_GROUPS * SSD_STATE
    dir_cols = 2 * gn + n_heads
    base0 = 2 * di
    base1 = base0 + dir_cols
    n_main = base0 + 2 * gn
    in_w = in_w_all[j]
    w_tail = in_w[:, base1:base1 + 2 * gn].astype(BF16)
    w_dt = jnp.concatenate(
        [in_w[:, base0 + 2 * gn:base0 + dir_cols], in_w[:, base1 + 2 * gn:base1 + dir_cols]], axis=1).astype(BF16)
    h, dt_raw = _normmod(tl, xall, modtab, layer, norm_g, w_dt)
    proj = _inproj(tl, h, in_w_all, j, n_main, w_tail)
    dt2 = dt_raw.reshape(tl.n_tok, 2, n_heads).transpose(1, 0, 2)
    head_params = jnp.zeros((2, 8, n_heads), F32).at[:, 0].set(a_log).at[:, 1].set(dt_bias)
    dskip_lanes = jnp.repeat(d_skip, SSD_HEAD_DIM, axis=-1).reshape(2, 1, di)
    y2 = _ssd_scan(tl, proj, dt2, conv_w, conv_b, head_params, dskip_lanes)
    return _outproj(tl, xall, modtab, layer, y2, proj, ssd_norm_g, out_w.astype(BF16))


def kernel(x, c, ctx, c_ctx, mod_w, mod_b, norm_mix_g, norm_ffn_g, pool_w, pool_b, pool_scale, ssd_in_w,
           ssd_conv_w, ssd_conv_b, ssd_a_log, ssd_dt_bias, ssd_d, ssd_norm_g, ssd_out_w, router_w, router_bias,
           moe_w_gate, moe_w_up, moe_w_down, final_norm_g):
    bsz, seq, d = x.shape
    ctx_len = ctx.shape[1]
    depth = mod_w.shape[0]
    assert bsz + 1 <= COND_ROWS
    tl = _Tiles(bsz, ctx_len, seq, d)
    cond = jnp.concatenate([c_ctx[None, :], c, jnp.zeros((COND_ROWS - 1 - bsz, d), F32)], axis=0)
    modtab = _adaln_table(cond, mod_w, mod_b)
    xall = jnp.concatenate([ctx.reshape(bsz * ctx_len, d), x.reshape(bsz * seq, d)], axis=0)
    for i in range(depth):
        j = i // 2
        if i % 2 == 0:
            xall = _pool_layer(tl, xall, modtab, i, norm_mix_g[i], pool_w[j], pool_b[j], pool_scale[j])
        else:
            xall = _ssd_layer(tl, xall, modtab, i, norm_mix_g[i], ssd_in_w, j, ssd_conv_w[j], ssd_conv_b[j],
                              ssd_a_log[j], ssd_dt_bias[j], ssd_d[j], ssd_norm_g[j], ssd_out_w[j])
        xall = _moe_layer(tl, xall, modtab, i, norm_ffn_g[i], router_w, router_bias,
                          moe_w_gate, moe_w_up, moe_w_down, final_norm_g, i == depth - 1)
    return xall.reshape(bsz, seq, d)
```
